```python
import jax, jax.numpy as jnp
from jax import lax
import numpy as np

D_MODEL = 1024
BATCH = 4
SEQ = 8192
DEPTH = 1

HEAD_DIM = 64
FOX_HEADS = 8
SWA_HEADS = 8
SWA_KV_HEADS = 2
SWA_GROUP = SWA_HEADS // SWA_KV_HEADS
WINDOW = 128
Q_BLOCK = 128
ROPE_THETA = 10000.0
N_GROUPS = 8
EXPERTS_PER_GROUP = 8
N_EXPERTS = N_GROUPS * EXPERTS_PER_GROUP
TOP_K = 2
D_EXPERT = D_MODEL // 2
MOE_BLOCK = 128
EPS = 1e-6

FOX_W = FOX_HEADS * HEAD_DIM
SWA_W = SWA_HEADS * HEAD_DIM
SWA_KV_W = SWA_KV_HEADS * HEAD_DIM
IN_SPLITS = (FOX_W, FOX_W, FOX_W, FOX_HEADS, SWA_W, SWA_KV_W, SWA_KV_W, D_MODEL, D_MODEL)
N_IN = sum(IN_SPLITS)
SPLIT_POINTS = tuple(int(v) for v in np.cumsum(IN_SPLITS)[:-1])

kernel_name = "hybrid_fox_swa_sink_hier_moe"


def rms_norm(x, g):
    xf = x.astype(jnp.float32)
    y = xf * lax.rsqrt(jnp.mean(xf * xf, axis=-1, keepdims=True) + EPS)
    return (y * g.astype(jnp.float32)).astype(x.dtype)


def rotary(x, positions):
    half = HEAD_DIM // 2
    inv_freq = ROPE_THETA ** (-jnp.arange(half, dtype=jnp.float32) * 2.0 / HEAD_DIM)
    ang = positions.astype(jnp.float32)[..., None] * inv_freq
    cos = jnp.cos(ang)[:, :, None, :]
    sin = jnp.sin(ang)[:, :, None, :]
    xf = x.astype(jnp.float32)
    x1, x2 = xf[..., :half], xf[..., half:]
    return jnp.concatenate([x1 * cos - x2 * sin, x2 * cos + x1 * sin], axis=-1).astype(x.dtype)


def fox_attention(q, k, v, log_f):
    B, S, H, Dh = q.shape
    nb = S // Q_BLOCK
    c = jnp.cumsum(log_f, axis=1)
    c_k = jnp.transpose(c, (0, 2, 1))
    q_blocks = q.reshape(B, nb, Q_BLOCK, H, Dh).transpose(1, 0, 2, 3, 4)
    c_blocks = c.reshape(B, nb, Q_BLOCK, H).transpose(1, 0, 3, 2)
    key_pos = jnp.arange(S)
    scale = HEAD_DIM ** -0.5

    def one_block(args):
        qb, cb, i = args
        s = jnp.einsum('bqhd,bkhd->bhqk', qb, k).astype(jnp.float32) * scale
        s = s + cb[..., None] - c_k[:, :, None, :]
        q_pos = i * Q_BLOCK + jnp.arange(Q_BLOCK)
        causal = key_pos[None, :] <= q_pos[:, None]
        p = jax.nn.softmax(jnp.where(causal, s, -jnp.inf), axis=-1)
        return jnp.einsum('bhqk,bkhd->bqhd', p.astype(v.dtype), v)

    o = lax.map(one_block, (q_blocks, c_blocks, jnp.arange(nb)))
    return o.transpose(1, 0, 2, 3, 4).reshape(B, S, H * Dh)


def swa_sink_attention(q, k, v, sinks):
    B, S, _, Dh = q.shape
    nb = S // WINDOW
    qb = q.reshape(B, nb, WINDOW, SWA_KV_HEADS, SWA_GROUP, Dh)

    def with_prev(t):
        t = t.reshape(B, nb, WINDOW, SWA_KV_HEADS, Dh)
        prev = jnp.pad(t[:, :-1], ((0, 0), (1, 0), (0, 0), (0, 0), (0, 0)))
        return jnp.concatenate([prev, t], axis=2)

    kw, vw = with_prev(k), with_prev(v)
    s = jnp.einsum('bnqhgd,bnkhd->bnhgqk', qb, kw).astype(jnp.float32) * (HEAD_DIM ** -0.5)
    i = jnp.arange(WINDOW)[:, None]
    j = jnp.arange(2 * WINDOW)[None, :]
    dist = WINDOW + i - j
    band = (dist >= 0) & (dist < WINDOW)
    valid = jnp.where((jnp.arange(nb) == 0)[:, None, None], band & (j >= WINDOW), band)
    s = jnp.where(valid[None, :, None, None], s, -jnp.inf)
    sink = jnp.broadcast_to(
        sinks.astype(jnp.float32).reshape(SWA_KV_HEADS, SWA_GROUP)[None, None, :, :, None, None],
        s.shape[:-1] + (1,))
    p = jax.nn.softmax(jnp.concatenate([s, sink], axis=-1), axis=-1)[..., :-1]
    o = jnp.einsum('bnhgqk,bnkhd->bnqhgd', p.astype(v.dtype), vw)
    return o.reshape(B, S, SWA_HEADS * Dh)


def hybrid_mixer(h, positions, w_in, b_forget, b_gate, sinks, w_proj_fox, w_proj_swa, w_out):
    B, S, _ = h.shape
    proj = h @ w_in
    q_f, k_f, v_f, f_logit, q_s, k_s, v_s, gl_f, gl_s = jnp.split(proj, SPLIT_POINTS, axis=-1)
    heads = lambda t, n: t.reshape(B, S, n, HEAD_DIM)
    log_f = jax.nn.log_sigmoid(f_logit.astype(jnp.float32) + b_forget.astype(jnp.float32))
    o_f = fox_attention(heads(q_f, FOX_HEADS), heads(k_f, FOX_HEADS), heads(v_f, FOX_HEADS), log_f)
    o_s = swa_sink_attention(rotary(heads(q_s, SWA_HEADS), positions),
                             rotary(heads(k_s, SWA_KV_HEADS), positions),
                             heads(v_s, SWA_KV_HEADS), sinks)
    g_f = jax.nn.sigmoid(gl_f + b_gate[0])
    g_s = jax.nn.sigmoid(gl_s + b_gate[1])
    merged = g_f * (o_f @ w_proj_fox) + g_s * (o_s @ w_proj_swa)
    return merged @ w_out


def hier_moe(h, w_group, b_group, w_expert, b_expert, w1, w3, w2):
    B, S, D = h.shape
    T = B * S
    hf = h.reshape(T, D)
    g_prob = jax.nn.softmax((hf @ w_group).astype(jnp.float32) + b_group.astype(jnp.float32), axis=-1)
    g_w, g_idx = lax.top_k(g_prob, 1)
    e_logits = ((hf @ w_expert).astype(jnp.float32) + b_expert.astype(jnp.float32)
                ).reshape(T, N_GROUPS, EXPERTS_PER_GROUP)
    e_sel = e_logits[jnp.arange(T), g_idx[:, 0]]
    top_v, top_i = lax.top_k(e_sel, TOP_K)
    gate = (g_w * jax.nn.softmax(top_v, axis=-1)).reshape(-1)
    expert_id = (g_idx * EXPERTS_PER_GROUP + top_i).reshape(-1)
    token_id = jnp.repeat(jnp.arange(T, dtype=jnp.int32), TOP_K)
    A = T * TOP_K
    order = jnp.argsort(expert_id)
    e_sorted = expert_id[order]
    counts = jnp.bincount(expert_id, length=N_EXPERTS)
    padded = (counts + MOE_BLOCK - 1) // MOE_BLOCK * MOE_BLOCK
    start = jnp.cumsum(counts) - counts
    pad_end = jnp.cumsum(padded)
    pad_start = pad_end - padded
    dest = pad_start[e_sorted] + jnp.arange(A) - start[e_sorted]
    P = A + N_EXPERTS * MOE_BLOCK
    n_blk = P // MOE_BLOCK
    row_token = jnp.full((P,), T, jnp.int32).at[dest].set(token_id[order])
    row_gate = jnp.zeros((P,), jnp.float32).at[dest].set(gate[order])
    blk_expert = jnp.minimum(jnp.searchsorted(pad_end, jnp.arange(n_blk) * MOE_BLOCK, side='right'),
                             N_EXPERTS - 1)
    h_pad = jnp.concatenate([hf, jnp.zeros((1, D), hf.dtype)], axis=0)
    xs = h_pad[row_token].reshape(n_blk, MOE_BLOCK, D)

    def expert_block(args):
        xb, e = args
        return (jax.nn.silu(xb @ w1[e]) * (xb @ w3[e])) @ w2[e]

    ys = lax.map(expert_block, (xs, blk_expert)).reshape(P, D)
    out = jnp.zeros((T + 1, D), jnp.float32).at[row_token].add(row_gate[:, None] * ys.astype(jnp.float32))
    return out[:T].reshape(B, S, D).astype(h.dtype)


def setup_inputs(seed: int = 0) -> dict:
    key = jax.random.key(seed)
    ks = jax.random.split(key, 20)
    L, D = DEPTH, D_MODEL
    nrm = lambda k, shape, fan_in: jax.random.normal(k, shape, jnp.float32) * fan_in ** -0.5
    return {
        "x": jax.random.normal(ks[0], (BATCH, SEQ, D), jnp.float32),
        "positions": (jnp.arange(SEQ, dtype=jnp.int32)[None, :]
                      + jax.random.randint(ks[1], (BATCH, 1), 0, 1024, jnp.int32)),
        "attn_norm": 1.0 + 0.1 * jax.random.normal(ks[2], (L, D), jnp.float32),
        "w_in": nrm(ks[3], (L, D, N_IN), D),
        "b_forget": jax.random.uniform(ks[4], (L, FOX_HEADS), jnp.float32, 1.0, 4.0),
        "b_gate": 0.02 * jax.random.normal(ks[5], (L, 2, D), jnp.float32),
        "attn_sinks": jax.random.normal(ks[6], (L, SWA_HEADS), jnp.float32),
        "w_proj_fox": nrm(ks[7], (L, FOX_W, D), FOX_W),
        "w_proj_swa": nrm(ks[8], (L, SWA_W, D), SWA_W),
        "w_out": nrm(ks[9], (L, D, D), D),
        "ffn_norm": 1.0 + 0.1 * jax.random.normal(ks[10], (L, D), jnp.float32),
        "w_group": nrm(ks[11], (L, D, N_GROUPS), D),
        "b_group": 0.01 * jax.random.normal(ks[12], (L, N_GROUPS), jnp.float32),
        "w_expert": nrm(ks[13], (L, D, N_EXPERTS), D),
        "b_expert": 0.01 * jax.random.normal(ks[14], (L, N_EXPERTS), jnp.float32),
        "w1": nrm(ks[15], (L, N_EXPERTS, D, D_EXPERT), D),
        "w3": nrm(ks[16], (L, N_EXPERTS, D, D_EXPERT), D),
        "w2": nrm(ks[17], (L, N_EXPERTS, D_EXPERT, D), D_EXPERT),
        "final_norm": 1.0 + 0.1 * jax.random.normal(ks[18], (D,), jnp.float32),
    }


def reference(x, positions, attn_norm, w_in, b_forget, b_gate, attn_sinks, w_proj_fox, w_proj_swa,
              w_out, ffn_norm, w_group, b_group, w_expert, b_expert, w1, w3, w2, final_norm):
    for l in range(DEPTH):
        h = rms_norm(x, attn_norm[l])
        x = x + hybrid_mixer(h, positions, w_in[l], b_forget[l], b_gate[l], attn_sinks[l],
                             w_proj_fox[l], w_proj_swa[l], w_out[l])
        h = rms_norm(x, ffn_norm[l])
        x = x + hier_moe(h, w_group[l], b_group[l], w_expert[l], b_expert[l], w1[l], w3[l], w2[l])
    return rms_norm(x, final_norm)
```

```python
import functools

import numpy as np
import jax
import jax.numpy as jnp
from jax import lax
from jax.experimental import pallas as pl
from jax.experimental.pallas import tpu as pltpu

HEAD_DIM = 64
FOX_HEADS = 8
SWA_HEADS = 8
SWA_KV_HEADS = 2
WINDOW = 128
ROPE_THETA = 10000.0
N_GROUPS = 8
EXPERTS_PER_GROUP = 8
N_EXPERTS = N_GROUPS * EXPERTS_PER_GROUP
MOE_BLOCK = 128
EPS = 1e-6
LANES = 128
NEG_BIG = -1e30

FOX_W = FOX_HEADS * HEAD_DIM
SWA_W = SWA_HEADS * HEAD_DIM
SWA_KV_W = SWA_KV_HEADS * HEAD_DIM
SWA_HEAD_ORDER = (0, 4, 1, 5, 2, 6, 3, 7)

F32 = jnp.float32
BF16 = jnp.bfloat16
VMEM_LIMIT = 56 * 1024 * 1024


def _cparams(sem):
    return pltpu.CompilerParams(dimension_semantics=sem, vmem_limit_bytes=VMEM_LIMIT)


def _split3(x):
    hi = x.astype(BF16)
    r1 = x - hi.astype(F32)
    mid = r1.astype(BF16)
    lo = (r1 - mid.astype(F32)).astype(BF16)
    return hi, mid, lo


C_QF, C_KF, C_VF = 0, FOX_W, 2 * FOX_W
C_QS = 3 * FOX_W
C_QSR = C_QS + SWA_W
C_KS = C_QSR + SWA_W
C_KSR = C_KS + SWA_KV_W
C_VS = C_KSR + SWA_KV_W
C_GF = C_VS + SWA_KV_W


def _in_proj_kernel(x_ref, pos_ref, g_ref, w_ref, invf_ref, bf_ref, bg_ref,
                    qf_ref, kf_ref, vf_ref, qs_ref, ks_ref, vs_ref, gf_ref, gs_ref, c_ref,
                    carry_ref, *, d_model):
    j = pl.program_id(1)
    tm = x_ref.shape[0]
    c_gs = C_GF + d_model
    c_fl = c_gs + d_model

    @pl.when(j == 0)
    def _():
        carry_ref[...] = jnp.zeros_like(carry_ref)

    x = x_ref[...]
    ms = jnp.mean(x * x, axis=-1, keepdims=True)
    h = ((x * lax.rsqrt(ms + EPS)) * g_ref[...]).astype(BF16)

    def mm(lo, hi):
        return jnp.dot(h, w_ref[:, lo:hi], preferred_element_type=F32)

    qf_ref[...] = mm(C_QF, C_KF).astype(BF16)
    kf_ref[...] = mm(C_KF, C_VF).astype(BF16)
    vf_ref[...] = mm(C_VF, C_QS).astype(BF16)

    ang = pos_ref[...].astype(F32) * invf_ref[...]
    cos = jnp.cos(ang)
    sin = jnp.sin(ang)
    n_rep = SWA_W // LANES
    cos_q = jnp.concatenate([cos] * n_rep, axis=1)
    sin_q = jnp.concatenate([sin] * n_rep, axis=1)
    qs_ref[...] = (mm(C_QS, C_QSR) * cos_q + mm(C_QSR, C_KS) * sin_q).astype(BF16)
    ks_ref[...] = (mm(C_KS, C_KSR) * cos + mm(C_KSR, C_VS) * sin).astype(BF16)
    vs_ref[...] = mm(C_VS, C_GF).astype(BF16)

    gf_ref[...] = jax.nn.sigmoid(mm(C_GF, c_gs) + bg_ref[0:1, :]).astype(BF16)
    gs_ref[...] = jax.nn.sigmoid(mm(c_gs, c_fl) + bg_ref[1:2, :]).astype(BF16)

    z = mm(c_fl, c_fl + LANES) + bf_ref[...]
    lane = lax.broadcasted_iota(jnp.int32, (tm, LANES), 1)
    lf = jnp.minimum(z, 0.0) - jnp.log1p(jnp.exp(-jnp.abs(z)))
    lf = jnp.where(lane < FOX_HEADS, lf, 0.0)
    row = lax.broadcasted_iota(jnp.int32, (tm, tm), 0)
    col = lax.broadcasted_iota(jnp.int32, (tm, tm), 1)
    tri = jnp.where(row >= col, 1.0, 0.0).astype(BF16)
    hi, mid, lo = _split3(lf)
    c = (jnp.dot(tri, hi, preferred_element_type=F32)
         + jnp.dot(tri, mid, preferred_element_type=F32)
         + jnp.dot(tri, lo, preferred_element_type=F32)) + carry_ref[0:1, :]
    c_ref[...] = c
    carry_ref[0:1, :] = c[tm - 1:tm, :]


def _in_proj(x2, pos2, g_attn, w_all, invf, bf_pad, b_gate, *, batch, seq, tm):
    T, D = x2.shape
    nj = seq // tm
    n_all = w_all.shape[1]
    row_blk = lambda w: pl.BlockSpec((tm, w), lambda b, j: (b * nj + j, 0))
    const = lambda shape: pl.BlockSpec(shape, lambda b, j: (0, 0))
    out_shapes = (
        jax.ShapeDtypeStruct((T, FOX_W), BF16), jax.ShapeDtypeStruct((T, FOX_W), BF16),
        jax.ShapeDtypeStruct((T, FOX_W), BF16), jax.ShapeDtypeStruct((T, SWA_W), BF16),
        jax.ShapeDtypeStruct((T, SWA_KV_W), BF16), jax.ShapeDtypeStruct((T, SWA_KV_W), BF16),
        jax.ShapeDtypeStruct((T, D), BF16), jax.ShapeDtypeStruct((T, D), BF16),
        jax.ShapeDtypeStruct((T, LANES), F32),
    )
    return pl.pallas_call(
        functools.partial(_in_proj_kernel, d_model=D),
        grid=(batch, nj),
        in_specs=[row_blk(D), row_blk(1), const((1, D)), const((D, n_all)), const((1, LANES)),
                  const((1, LANES)), const((2, D))],
        out_specs=(row_blk(FOX_W), row_blk(FOX_W), row_blk(FOX_W), row_blk(SWA_W),
                   row_blk(SWA_KV_W), row_blk(SWA_KV_W), row_blk(D), row_blk(D), row_blk(LANES)),
        out_shape=out_shapes,
        scratch_shapes=[pltpu.VMEM((8, LANES), F32)],
        compiler_params=_cparams(("arbitrary", "arbitrary")),
        name="in_proj",
    )(x2, pos2, g_attn, w_all, invf, bf_pad, b_gate)


def _fox_kernel(q_ref, k_ref, v_ref, cq_ref, ck_ref, o_ref, m_ref, l_ref, acc_ref, *, tk):
    p = pl.program_id(1)
    i = pl.program_id(2)
    tq = q_ref.shape[0]
    n_sub = tq // tk
    q = q_ref[...]
    lane = lax.broadcasted_iota(jnp.int32, (tq, LANES), 1)
    zero = jnp.zeros_like(q)
    q_h = (jnp.where(lane < HEAD_DIM, q, zero), jnp.where(lane >= HEAD_DIM, q, zero))
    cq = cq_ref[...]
    cq_h = tuple(jnp.sum(jnp.where(lane == 2 * p + hh, cq, 0.0), axis=1, keepdims=True)
                 for hh in range(2))

    m_ref[...] = jnp.full_like(m_ref, NEG_BIG)
    l_ref[...] = jnp.zeros_like(l_ref)
    acc_ref[...] = jnp.zeros_like(acc_ref)

    def block(ks, masked_offset):
        k = k_ref[pl.ds(ks, tk), :]
        v = v_ref[pl.ds(ks, tk), :]
        ck = ck_ref[:, pl.ds(ks, tk)]
        for hh in range(2):
            s = lax.dot_general(q_h[hh], k, (((1,), (1,)), ((), ())),
                                preferred_element_type=F32)
            s = s + cq_h[hh] - ck[hh:hh + 1, :]
            if masked_offset is not None:
                r = lax.broadcasted_iota(jnp.int32, (tq, tk), 0)
                c = lax.broadcasted_iota(jnp.int32, (tq, tk), 1) + masked_offset
                s = jnp.where(c <= r, s, -jnp.inf)
            m_old = m_ref[hh]
            m_new = jnp.maximum(m_old, jnp.max(s, axis=1, keepdims=True))
            alpha = jnp.exp(m_old - m_new)
            pm = jnp.exp(s - m_new)
            l_ref[hh] = alpha * l_ref[hh] + jnp.sum(pm, axis=1, keepdims=True)
            acc_ref[hh] = alpha * acc_ref[hh] + jnp.dot(pm.astype(BF16), v,
                                                        preferred_element_type=F32)
            m_ref[hh] = m_new

    def body(jb, carry):
        block(pl.multiple_of(jb * tk, tk), None)
        return carry

    lax.fori_loop(0, i * n_sub, body, 0)
    for sub in range(n_sub):
        block(pl.multiple_of(i * tq + sub * tk, tk), sub * tk)

    inv0 = 1.0 / l_ref[0]
    inv1 = 1.0 / l_ref[1]
    o_ref[...] = jnp.where(lane < HEAD_DIM, acc_ref[0] * inv0, acc_ref[1] * inv1).astype(BF16)


def _fox_attn(qf, kf, vf, c_pad, c_rows, *, batch, seq, tq, tk):
    T = qf.shape[0]
    nq = seq // tq
    n_pair = FOX_HEADS // 2
    return pl.pallas_call(
        functools.partial(_fox_kernel, tk=tk),
        grid=(batch, n_pair, nq),
        in_specs=[
            pl.BlockSpec((tq, LANES), lambda b, p, i: (b * nq + i, p)),
            pl.BlockSpec((seq, LANES), lambda b, p, i: (b, p)),
            pl.BlockSpec((seq, LANES), lambda b, p, i: (b, p)),
            pl.BlockSpec((tq, LANES), lambda b, p, i: (b * nq + i, 0)),
            pl.BlockSpec((None, None, 2, seq), lambda b, p, i: (b, p, 0, 0)),
        ],
        out_specs=pl.BlockSpec((tq, LANES), lambda b, p, i: (b * nq + i, p)),
        out_shape=jax.ShapeDtypeStruct((T, FOX_W), BF16),
        scratch_shapes=[pltpu.VMEM((2, tq, 1), F32), pltpu.VMEM((2, tq, 1), F32),
                        pltpu.VMEM((2, tq, LANES), F32)],
        compiler_params=_cparams(("arbitrary", "arbitrary", "arbitrary")),
        name="fox_attn",
    )(qf, kf, vf, c_pad, c_rows)


def _swa_kernel(sink_ref, q_ref, kc_ref, kp_ref, vc_ref, vp_ref, o_ref):
    i = pl.program_id(1)
    tq = q_ref.shape[0]
    n_win = tq // WINDOW
    n_col = SWA_W // LANES
    lane = lax.broadcasted_iota(jnp.int32, (WINDOW, LANES), 1)
    low = lane < HEAD_DIM
    r = lax.broadcasted_iota(jnp.int32, (WINDOW, 2 * WINDOW), 0)
    c = lax.broadcasted_iota(jnp.int32, (WINDOW, 2 * WINDOW), 1)
    dist = WINDOW + r - c
    band = (dist >= 0) & (dist < WINDOW)
    for w in range(n_win):
        if w == 0:
            k_win = jnp.concatenate([kp_ref[...], kc_ref[0:WINDOW, :]], axis=0)
            v_win = jnp.concatenate([vp_ref[...], vc_ref[0:WINDOW, :]], axis=0)
            first = i == 0
            valid = band & ((c >= WINDOW) | jnp.logical_not(first))
        else:
            k_win = kc_ref[(w - 1) * WINDOW:(w + 1) * WINDOW, :]
            v_win = vc_ref[(w - 1) * WINDOW:(w + 1) * WINDOW, :]
            valid = band
        for col_i in range(n_col):
            qc = q_ref[w * WINDOW:(w + 1) * WINDOW, col_i * LANES:(col_i + 1) * LANES]
            outs = []
            for half in range(2):
                qm = jnp.where(low if half == 0 else jnp.logical_not(low), qc, jnp.zeros_like(qc))
                s = lax.dot_general(qm, k_win, (((1,), (1,)), ((), ())),
                                    preferred_element_type=F32)
                s = jnp.where(valid, s, -jnp.inf)
                sink = sink_ref[2 * col_i + half]
                m = jnp.maximum(jnp.max(s, axis=1, keepdims=True), sink)
                pm = jnp.exp(s - m)
                denom = jnp.sum(pm, axis=1, keepdims=True) + jnp.exp(sink - m)
                pm = pm / denom
                outs.append(jnp.dot(pm.astype(BF16), v_win, preferred_element_type=F32))
            o_ref[w * WINDOW:(w + 1) * WINDOW, col_i * LANES:(col_i + 1) * LANES] = (
                jnp.where(low, outs[0], outs[1]).astype(BF16))


def _swa_attn(sinks_perm, qs, ks, vs, *, batch, seq, tq):
    T = qs.shape[0]
    nq = seq // tq
    n_win = tq // WINDOW
    nb = seq // WINDOW
    cur = lambda w: pl.BlockSpec((tq, w), lambda b, i: (b * nq + i, 0))
    prev = pl.BlockSpec((WINDOW, SWA_KV_W),
                        lambda b, i: (jnp.maximum(b * nb + i * n_win - 1, 0), 0))
    return pl.pallas_call(
        _swa_kernel,
        grid=(batch, nq),
        in_specs=[pl.BlockSpec(memory_space=pltpu.SMEM), cur(SWA_W), cur(SWA_KV_W), prev,
                  cur(SWA_KV_W), prev],
        out_specs=cur(SWA_W),
        out_shape=jax.ShapeDtypeStruct((T, SWA_W), BF16),
        compiler_params=_cparams(("arbitrary", "arbitrary")),
        name="swa_attn",
    )(sinks_perm, qs, ks, ks, vs, vs)


ROUTE_E0, ROUTE_E1, ROUTE_G0, ROUTE_G1, ROUTE_R0, ROUTE_R1 = range(6)


def _mix_route_kernel(of_ref, os_ref, gf_ref, gs_ref, x_ref, wpf_ref, wps_ref, wout_ref, gn_ref,
                      wrh_ref, wrl_ref, br_ref, x1_ref, h2_ref, route_ref, cnt_ref, carry_ref):
    step = pl.program_id(0)
    tm = x_ref.shape[0]

    @pl.when(step == 0)
    def _():
        carry_ref[...] = jnp.zeros_like(carry_ref)

    a = jnp.dot(of_ref[...], wpf_ref[...], preferred_element_type=F32)
    b = jnp.dot(os_ref[...], wps_ref[...], preferred_element_type=F32)
    merged = gf_ref[...].astype(F32) * a + gs_ref[...].astype(F32) * b
    y = jnp.dot(merged.astype(BF16), wout_ref[...], preferred_element_type=F32)
    x1 = x_ref[...] + y
    x1_ref[...] = x1
    ms = jnp.mean(x1 * x1, axis=-1, keepdims=True)
    h2 = (x1 * lax.rsqrt(ms + EPS)) * gn_ref[...]
    h2_ref[...] = h2

    h_hi = h2.astype(BF16)
    h_lo = (h2 - h_hi.astype(F32)).astype(BF16)
    wrh = wrh_ref[...]
    logits = (jnp.dot(h_hi, wrh, preferred_element_type=F32)
              + jnp.dot(h_lo, wrh, preferred_element_type=F32)
              + jnp.dot(h_hi, wrl_ref[...], preferred_element_type=F32)) + br_ref[...]

    lane = lax.broadcasted_iota(jnp.int32, (tm, LANES), 1)
    is_g = lane < N_GROUPS
    gl = jnp.where(is_g, logits, -jnp.inf)
    gmax = jnp.max(gl, axis=1, keepdims=True)
    gexp = jnp.where(is_g, jnp.exp(gl - gmax), 0.0)
    gprob = gexp / jnp.sum(gexp, axis=1, keepdims=True)
    g_w = jnp.max(gprob, axis=1, keepdims=True)
    g_idx = jnp.min(jnp.where(is_g & (gprob == g_w), lane, LANES), axis=1, keepdims=True)

    e_lane = lane - N_GROUPS
    in_grp = (e_lane >= 0) & (e_lane < N_EXPERTS) & ((e_lane >> 3) == g_idx)
    el = jnp.where(in_grp, logits, -jnp.inf)
    v0 = jnp.max(el, axis=1, keepdims=True)
    i0 = jnp.min(jnp.where(in_grp & (el == v0), lane, LANES), axis=1, keepdims=True)
    el1 = jnp.where(lane == i0, -jnp.inf, el)
    v1 = jnp.max(el1, axis=1, keepdims=True)
    i1 = jnp.min(jnp.where(in_grp & (lane != i0) & (el1 == v1), lane, LANES), axis=1, keepdims=True)
    t = jnp.exp(v1 - v0)
    den = 1.0 + t
    gate0 = g_w * (1.0 / den)
    gate1 = g_w * (t / den)
    e0 = i0 - N_GROUPS
    e1 = i1 - N_GROUPS

    oh0 = jnp.where(lane == e0, 1.0, 0.0)
    oh1 = jnp.where(lane == e1, 1.0, 0.0)
    row = lax.broadcasted_iota(jnp.int32, (tm, tm), 0)
    col = lax.broadcasted_iota(jnp.int32, (tm, tm), 1)
    tri = jnp.where(row > col, 1.0, 0.0).astype(BF16)
    cs0 = jnp.sum(oh0, axis=0, keepdims=True)
    cs1 = jnp.sum(oh1, axis=0, keepdims=True)
    carry = carry_ref[0:1, :]
    pre0 = jnp.dot(tri, oh0.astype(BF16), preferred_element_type=F32) + carry
    pre1 = jnp.dot(tri, oh1.astype(BF16), preferred_element_type=F32) + (carry + cs0)
    rank0 = jnp.sum(oh0 * pre0, axis=1, keepdims=True)
    rank1 = jnp.sum(oh1 * pre1, axis=1, keepdims=True)
    new_carry = carry + cs0 + cs1
    carry_ref[0:1, :] = new_carry
    cnt_ref[...] = jnp.broadcast_to(new_carry, cnt_ref.shape)

    route = jnp.where(lane == ROUTE_E0, e0.astype(F32), 0.0)
    route = jnp.where(lane == ROUTE_E1, e1.astype(F32), route)
    route = jnp.where(lane == ROUTE_G0, gate0, route)
    route = jnp.where(lane == ROUTE_G1, gate1, route)
    route = jnp.where(lane == ROUTE_R0, rank0, route)
    route = jnp.where(lane == ROUTE_R1, rank1, route)
    route_ref[...] = route


def _mix_route(o_f, o_s, g_f, g_s, x2, wpf, wps, wout, g_ffn, wr_hi, wr_lo, b_r, *, tm):
    T, D = x2.shape
    row_blk = lambda w: pl.BlockSpec((tm, w), lambda i: (i, 0))
    const = lambda shape: pl.BlockSpec(shape, lambda i: (0, 0))
    return pl.pallas_call(
        _mix_route_kernel,
        grid=(T // tm,),
        in_specs=[row_blk(FOX_W), row_blk(SWA_W), row_blk(D), row_blk(D), row_blk(D),
                  const((FOX_W, D)), const((SWA_W, D)), const((D, D)), const((1, D)),
                  const((D, LANES)), const((D, LANES)), const((1, LANES))],
        out_specs=(row_blk(D), row_blk(D), row_blk(LANES), const((8, LANES))),
        out_shape=(jax.ShapeDtypeStruct((T, D), F32), jax.ShapeDtypeStruct((T, D), F32),
                   jax.ShapeDtypeStruct((T, LANES), F32), jax.ShapeDtypeStruct((8, LANES), F32)),
        scratch_shapes=[pltpu.VMEM((8, LANES), F32)],
        compiler_params=_cparams(("arbitrary",)),
        name="mix_route",
    )(o_f, o_s, g_f, g_s, x2, wpf, wps, wout, g_ffn, wr_hi, wr_lo, b_r)


def _dispatch_kernel(dest_ref, h_ref, xs_in_ref, xs_ref, sem):
    del xs_in_ref
    tb = h_ref.shape[0]

    def row_copy(t, k):
        d = dest_ref[2 * t + k]
        return pltpu.make_async_copy(h_ref.at[pl.ds(t, 1), :], xs_ref.at[pl.ds(d, 1), :], sem)

    def start(t, carry):
        row_copy(t, 0).start()
        row_copy(t, 1).start()
        return carry

    lax.fori_loop(0, tb, start, 0)

    def wait(t, carry):
        row_copy(t, 0).wait()
        row_copy(t, 1).wait()
        return carry

    lax.fori_loop(0, tb, wait, 0)


def _dispatch(dest_flat, h2, xs_init, *, tb):
    T, D = h2.shape
    return pl.pallas_call(
        _dispatch_kernel,
        grid=(T // tb,),
        in_specs=[pl.BlockSpec((2 * tb,), lambda i: (i,), memory_space=pltpu.SMEM),
                  pl.BlockSpec((tb, D), lambda i: (i, 0)),
                  pl.BlockSpec(memory_space=pl.ANY)],
        out_specs=pl.BlockSpec(memory_space=pl.ANY),
        out_shape=jax.ShapeDtypeStruct(xs_init.shape, xs_init.dtype),
        scratch_shapes=[pltpu.SemaphoreType.DMA(())],
        input_output_aliases={2: 0},
        compiler_params=_cparams(("arbitrary",)),
        name="dispatch",
    )(dest_flat, h2, xs_init)


def _experts_kernel(be_ref, nu_ref, xs_ref, w1_ref, w3_ref, w2_ref, ys_ref, w1b, w3b, w2b):
    i = pl.program_id(0)
    e = be_ref[i]
    e_prev = be_ref[jnp.maximum(i - 1, 0)]
    used = i < nu_ref[0]

    @pl.when(used & ((i == 0) | (e != e_prev)))
    def _():
        w1b[...] = w1_ref[...].astype(BF16)
        w3b[...] = w3_ref[...].astype(BF16)
        w2b[...] = w2_ref[...].astype(BF16)

    @pl.when(used)
    def _():
        xb = xs_ref[...].astype(BF16)
        h1 = jnp.dot(xb, w1b[...], preferred_element_type=F32)
        h3 = jnp.dot(xb, w3b[...], preferred_element_type=F32)
        act = (h1 * jax.nn.sigmoid(h1)) * h3
        ys_ref[...] = jnp.dot(act.astype(BF16), w2b[...], preferred_element_type=F32)

    @pl.when(jnp.logical_not(used))
    def _():
        ys_ref[...] = jnp.zeros_like(ys_ref)


def _experts(blk_expert, n_used, xs, w1, w3, w2):
    P, D = xs.shape
    de = w1.shape[-1]
    n_blk = P // MOE_BLOCK
    grid_spec = pltpu.PrefetchScalarGridSpec(
        num_scalar_prefetch=2,
        grid=(n_blk,),
        in_specs=[pl.BlockSpec((MOE_BLOCK, D), lambda i, be, nu: (i, 0)),
                  pl.BlockSpec((None, D, de), lambda i, be, nu: (be[i], 0, 0)),
                  pl.BlockSpec((None, D, de), lambda i, be, nu: (be[i], 0, 0)),
                  pl.BlockSpec((None, de, D), lambda i, be, nu: (be[i], 0, 0))],
        out_specs=pl.BlockSpec((MOE_BLOCK, D), lambda i, be, nu: (i, 0)),
        scratch_shapes=[pltpu.VMEM((D, de), BF16), pltpu.VMEM((D, de), BF16),
                        pltpu.VMEM((de, D), BF16)],
    )
    return pl.pallas_call(
        _experts_kernel,
        grid_spec=grid_spec,
        out_shape=jax.ShapeDtypeStruct((P, D), F32),
        compiler_params=_cparams(("arbitrary",)),
        name="experts",
    )(blk_expert, n_used, xs, w1, w3, w2)


def _combine_kernel(dest_ref, x1_ref, route_ref, gfin_ref, ys_ref, o_ref, ybuf, sem):
    tb = x1_ref.shape[0]

    def row_copy(t, k):
        d = dest_ref[2 * t + k]
        return pltpu.make_async_copy(ys_ref.at[pl.ds(d, 1), :], ybuf.at[k, pl.ds(t, 1), :], sem)

    def start(t, carry):
        row_copy(t, 0).start()
        row_copy(t, 1).start()
        return carry

    lax.fori_loop(0, tb, start, 0)

    def wait(t, carry):
        row_copy(t, 0).wait()
        row_copy(t, 1).wait()
        return carry

    lax.fori_loop(0, tb, wait, 0)

    route = route_ref[...]
    lane = lax.broadcasted_iota(jnp.int32, route.shape, 1)
    g0 = jnp.sum(jnp.where(lane == ROUTE_G0, route, 0.0), axis=1, keepdims=True)
    g1 = jnp.sum(jnp.where(lane == ROUTE_G1, route, 0.0), axis=1, keepdims=True)
    moe = g0 * ybuf[0] + g1 * ybuf[1]
    x2 = x1_ref[...] + moe
    ms = jnp.mean(x2 * x2, axis=-1, keepdims=True)
    o_ref[...] = (x2 * lax.rsqrt(ms + EPS)) * gfin_ref[...]


def _combine(dest_flat, x1, route, g_final, ys, *, tb):
    T, D = x1.shape
    return pl.pallas_call(
        _combine_kernel,
        grid=(T // tb,),
        in_specs=[pl.BlockSpec((2 * tb,), lambda i: (i,), memory_space=pltpu.SMEM),
                  pl.BlockSpec((tb, D), lambda i: (i, 0)),
                  pl.BlockSpec((tb, LANES), lambda i: (i, 0)),
                  pl.BlockSpec((1, D), lambda i: (0, 0)),
                  pl.BlockSpec(memory_space=pl.ANY)],
        out_specs=pl.BlockSpec((tb, D), lambda i: (i, 0)),
        out_shape=jax.ShapeDtypeStruct((T, D), F32),
        scratch_shapes=[pltpu.VMEM((2, tb, D), F32), pltpu.SemaphoreType.DMA(())],
        compiler_params=_cparams(("arbitrary",)),
        name="combine",
    )(dest_flat, x1, route, g_final, ys)


def _rot_half_cols(w, n_heads):
    d = w.shape[0]
    half = HEAD_DIM // 2
    w4 = w.reshape(d, n_heads, 2, half)
    return jnp.concatenate([-w4[:, :, 1:2, :], w4[:, :, 0:1, :]], axis=2).reshape(d, n_heads * HEAD_DIM)


def _pick(n, prefs):
    for p in prefs:
        if n % p == 0:
            return p
    return n


def _layer(x2, pos2, batch, seq, attn_norm, w_in, b_forget, b_gate, sinks, w_proj_fox, w_proj_swa,
           w_out, ffn_norm, w_group, b_group, w_expert, b_expert, w1, w3, w2, final_norm):
    T, D = x2.shape
    scale = HEAD_DIM ** -0.5

    sp = np.cumsum([FOX_W, FOX_W, FOX_W, FOX_HEADS, SWA_W, SWA_KV_W, SWA_KV_W, D, D])[:-1]
    wq_f, wk_f, wv_f, w_fl, wq_s, wk_s, wv_s, wg_f, wg_s = jnp.split(w_in, sp, axis=1)
    head_order = np.asarray(SWA_HEAD_ORDER)
    wq_s = wq_s.reshape(D, SWA_HEADS, HEAD_DIM)[:, head_order, :].reshape(D, SWA_W)
    w_fl_pad = jnp.pad(w_fl, ((0, 0), (0, LANES - FOX_HEADS)))
    w_all = jnp.concatenate(
        [wq_f * scale, wk_f, wv_f, wq_s * scale, _rot_half_cols(wq_s, SWA_HEADS) * scale,
         wk_s, _rot_half_cols(wk_s, SWA_KV_HEADS), wv_s, wg_f, wg_s, w_fl_pad], axis=1).astype(BF16)
    half = HEAD_DIM // 2
    inv_freq = ROPE_THETA ** (-jnp.arange(half, dtype=F32) * 2.0 / HEAD_DIM)
    invf = jnp.tile(inv_freq, LANES // half).reshape(1, LANES)
    bf_pad = jnp.pad(b_forget.astype(F32), (0, LANES - FOX_HEADS)).reshape(1, LANES)
    sinks_perm = sinks.astype(F32)[head_order]
    wps = w_proj_swa.reshape(SWA_HEADS, HEAD_DIM, D)[head_order].reshape(SWA_W, D).astype(BF16)
    wpf = w_proj_fox.astype(BF16)
    wout = w_out.astype(BF16)
    w_r = jnp.pad(jnp.concatenate([w_group, w_expert], axis=1).astype(F32),
                  ((0, 0), (0, LANES - N_GROUPS - N_EXPERTS)))
    wr_hi = w_r.astype(BF16)
    wr_lo = (w_r - wr_hi.astype(F32)).astype(BF16)
    b_r = jnp.pad(jnp.concatenate([b_group, b_expert]).astype(F32),
                  (0, LANES - N_GROUPS - N_EXPERTS)).reshape(1, LANES)

    tm1 = _pick(seq, (512, 256, 128))
    qf, kf, vf, qs, ks, vs, g_f, g_s, c_pad = _in_proj(
        x2, pos2, attn_norm.reshape(1, D), w_all, invf, bf_pad, b_gate.astype(F32),
        batch=batch, seq=seq, tm=tm1)
    c_rows = c_pad[:, :FOX_HEADS].reshape(batch, seq, FOX_HEADS // 2, 2).transpose(0, 2, 3, 1)
    tq = _pick(seq, (512, 256, 128))
    o_f = _fox_attn(qf, kf, vf, c_pad, c_rows, batch=batch, seq=seq, tq=tq, tk=tq)
    o_s = _swa_attn(sinks_perm, qs, ks, vs, batch=batch, seq=seq, tq=_pick(seq, (512, 256, 128)))

    tm3 = _pick(T, (512, 256, 128))
    x1, h2, route, counts = _mix_route(o_f, o_s, g_f, g_s, x2, wpf, wps, wout,
                                       ffn_norm.reshape(1, D), wr_hi, wr_lo, b_r, tm=tm3)

    cnt = counts[0, :N_EXPERTS].astype(jnp.int32)
    padded = (cnt + MOE_BLOCK - 1) // MOE_BLOCK * MOE_BLOCK
    pad_end = jnp.cumsum(padded)
    pad_start = pad_end - padded
    e_ids = route[:, ROUTE_E0:ROUTE_E1 + 1].astype(jnp.int32)
    ranks = route[:, ROUTE_R0:ROUTE_R1 + 1].astype(jnp.int32)
    dest_flat = (pad_start[e_ids] + ranks).reshape(-1)
    P = 2 * T + N_EXPERTS * MOE_BLOCK
    n_blk = P // MOE_BLOCK
    blk_pos = jnp.arange(n_blk, dtype=jnp.int32) * MOE_BLOCK
    blk_expert = jnp.minimum(
        jnp.sum((pad_end[None, :] <= blk_pos[:, None]).astype(jnp.int32), axis=1), N_EXPERTS - 1)
    n_used = (pad_end[-1:] // MOE_BLOCK).astype(jnp.int32)

    tb = _pick(T, (256, 128))
    xs = _dispatch(dest_flat, h2, jnp.zeros((P, D), F32), tb=tb)
    ys = _experts(blk_expert, n_used, xs, w1, w3, w2)
    return _combine(dest_flat, x1, route, final_norm.reshape(1, D), ys, tb=tb)


def kernel(x, positions, attn_norm, w_in, b_forget, b_gate, attn_sinks, w_proj_fox, w_proj_swa,
           w_out, ffn_norm, w_group, b_group, w_expert, b_expert, w1, w3, w2, final_norm):
    B, S, D = x.shape
    depth = attn_norm.shape[0]
    x2 = x.reshape(B * S, D)
    pos2 = positions.reshape(B * S, 1).astype(jnp.int32)
    assert depth == 1, "a single layer is followed directly by the final norm"
    out = _layer(x2, pos2, B, S, attn_norm[0], w_in[0], b_forget[0], b_gate[0], attn_sinks[0],
                 w_proj_fox[0], w_proj_swa[0], w_out[0], ffn_norm[0], w_group[0], b_group[0],
                 w_expert[0], b_expert[0], w1[0], w3[0], w2[0], final_norm)
    return out.reshape(B, S, D)
```

```python
import functools

import numpy as np
import jax
import jax.numpy as jnp
from jax import lax
from jax.experimental import pallas as pl
from jax.experimental.pallas import tpu as pltpu

HEAD_DIM = 64
FOX_HEADS = 8
SWA_HEADS = 8
SWA_KV_HEADS = 2
WINDOW = 128
ROPE_THETA = 10000.0
N_GROUPS = 8
EXPERTS_PER_GROUP = 8
N_EXPERTS = N_GROUPS * EXPERTS_PER_GROUP
ROW_BLOCK = 256
DMA_UNROLL = 8
EPS = 1e-6
LANES = 128
NEG_BIG = -1e30
LOG2E = 1.4426950408889634
BIAS_TERMS = 3
BIAS_LANES = 2 * BIAS_TERMS
FOX_TQ = 1024
FOX_TK = 256
FOX_UNROLL = 4

FOX_W = FOX_HEADS * HEAD_DIM
SWA_W = SWA_HEADS * HEAD_DIM
SWA_KV_W = SWA_KV_HEADS * HEAD_DIM
SWA_HEAD_ORDER = (0, 4, 1, 5, 2, 6, 3, 7)

F32 = jnp.float32
BF16 = jnp.bfloat16
VMEM_LIMIT = 56 * 1024 * 1024


def _cparams(sem):
    return pltpu.CompilerParams(dimension_semantics=sem, vmem_limit_bytes=VMEM_LIMIT)


def _split3(x):
    hi = x.astype(BF16)
    r1 = x - hi.astype(F32)
    mid = r1.astype(BF16)
    lo = (r1 - mid.astype(F32)).astype(BF16)
    return hi, mid, lo


C_QF, C_KF, C_VF = 0, FOX_W, 2 * FOX_W
C_QS = 3 * FOX_W
C_QSR = C_QS + SWA_W
C_KS = C_QSR + SWA_W
C_KSR = C_KS + SWA_KV_W
C_VS = C_KSR + SWA_KV_W
C_GF = C_VS + SWA_KV_W


def _in_proj_kernel(x_ref, pos_ref, g_ref, w_ref, invf_ref, bf_ref, bg_ref, place_ref, ones_ref,
                    qf_ref, kf_ref, vf_ref, qs_ref, ks_ref, vs_ref, gf_ref, gs_ref, qx_ref, kx_ref,
                    carry_ref, *, d_model):
    j = pl.program_id(1)
    tm = x_ref.shape[0]
    c_gs = C_GF + d_model
    c_fl = c_gs + d_model

    @pl.when(j == 0)
    def _():
        carry_ref[...] = jnp.zeros_like(carry_ref)

    x = x_ref[...]
    ms = jnp.mean(x * x, axis=-1, keepdims=True)
    h = ((x * lax.rsqrt(ms + EPS)) * g_ref[...]).astype(BF16)

    def mm(lo, hi):
        return jnp.dot(h, w_ref[:, lo:hi], preferred_element_type=F32)

    qf_ref[...] = mm(C_QF, C_KF).astype(BF16)
    kf_ref[...] = mm(C_KF, C_VF).astype(BF16)
    vf_ref[...] = mm(C_VF, C_QS).astype(BF16)

    ang = pos_ref[...].astype(F32) * invf_ref[...]
    cos = jnp.cos(ang)
    sin = jnp.sin(ang)
    n_rep = SWA_W // LANES
    cos_q = jnp.concatenate([cos] * n_rep, axis=1)
    sin_q = jnp.concatenate([sin] * n_rep, axis=1)
    qs_ref[...] = (mm(C_QS, C_QSR) * cos_q + mm(C_QSR, C_KS) * sin_q).astype(BF16)
    ks_ref[...] = (mm(C_KS, C_KSR) * cos + mm(C_KSR, C_VS) * sin).astype(BF16)
    vs_ref[...] = mm(C_VS, C_GF).astype(BF16)

    gf_ref[...] = jax.nn.sigmoid(mm(C_GF, c_gs) + bg_ref[0:1, :]).astype(BF16)
    gs_ref[...] = jax.nn.sigmoid(mm(c_gs, c_fl) + bg_ref[1:2, :]).astype(BF16)

    z = mm(c_fl, c_fl + LANES) + bf_ref[...]
    lane = lax.broadcasted_iota(jnp.int32, (tm, LANES), 1)
    lf = jnp.minimum(z, 0.0) - jnp.log1p(jnp.exp(-jnp.abs(z)))
    lf = jnp.where(lane < FOX_HEADS, lf, 0.0)
    row = lax.broadcasted_iota(jnp.int32, (tm, tm), 0)
    col = lax.broadcasted_iota(jnp.int32, (tm, tm), 1)
    tri = jnp.where(row >= col, 1.0, 0.0).astype(BF16)
    hi, mid, lo = _split3(lf)
    c = (jnp.dot(tri, hi, preferred_element_type=F32)
         + jnp.dot(tri, mid, preferred_element_type=F32)
         + jnp.dot(tri, lo, preferred_element_type=F32)) + carry_ref[0:1, :]
    carry_ref[0:1, :] = c[tm - 1:tm, :]
    chi, cmid, clo = _split3(c * LOG2E)
    ext = jnp.dot(jnp.concatenate([chi, cmid, clo], axis=1), place_ref[...],
                  preferred_element_type=F32) + ones_ref[...]
    qx_ref[...] = ext[:, :LANES].astype(BF16)
    kx_ref[...] = ext[:, LANES:].astype(BF16)


def _bias_placement():
    place = np.zeros((BIAS_TERMS * LANES, 2 * LANES), np.float32)
    ones = np.zeros((1, 2 * LANES), np.float32)
    for h in range(FOX_HEADS):
        for t in range(BIAS_TERMS):
            place[t * LANES + h, BIAS_LANES * h + t] = 1.0
            place[t * LANES + h, LANES + BIAS_LANES * h + BIAS_TERMS + t] = -1.0
            ones[0, BIAS_LANES * h + BIAS_TERMS + t] = 1.0
            ones[0, LANES + BIAS_LANES * h + t] = 1.0
    return jnp.asarray(place, BF16), jnp.asarray(ones, F32)


def _in_proj(x2, pos2, g_attn, w_all, invf, bf_pad, b_gate, *, batch, seq, tm):
    T, D = x2.shape
    nj = seq // tm
    n_all = w_all.shape[1]
    place, ones = _bias_placement()
    row_blk = lambda w: pl.BlockSpec((tm, w), lambda b, j: (b * nj + j, 0))
    const = lambda shape: pl.BlockSpec(shape, lambda b, j: (0, 0))
    out_shapes = (
        jax.ShapeDtypeStruct((T, FOX_W), BF16), jax.ShapeDtypeStruct((T, FOX_W), BF16),
        jax.ShapeDtypeStruct((T, FOX_W), BF16), jax.ShapeDtypeStruct((T, SWA_W), BF16),
        jax.ShapeDtypeStruct((T, SWA_KV_W), BF16), jax.ShapeDtypeStruct((T, SWA_KV_W), BF16),
        jax.ShapeDtypeStruct((T, D), BF16), jax.ShapeDtypeStruct((T, D), BF16),
        jax.ShapeDtypeStruct((T, LANES), BF16), jax.ShapeDtypeStruct((T, LANES), BF16),
    )
    return pl.pallas_call(
        functools.partial(_in_proj_kernel, d_model=D),
        grid=(batch, nj),
        in_specs=[row_blk(D), row_blk(1), const((1, D)), const((D, n_all)), const((1, LANES)),
                  const((1, LANES)), const((2, D)), const(place.shape), const(ones.shape)],
        out_specs=(row_blk(FOX_W), row_blk(FOX_W), row_blk(FOX_W), row_blk(SWA_W),
                   row_blk(SWA_KV_W), row_blk(SWA_KV_W), row_blk(D), row_blk(D), row_blk(LANES),
                   row_blk(LANES)),
        out_shape=out_shapes,
        scratch_shapes=[pltpu.VMEM((8, LANES), F32)],
        compiler_params=_cparams(("arbitrary", "arbitrary")),
        name="in_proj",
    )(x2, pos2, g_attn, w_all, invf, bf_pad, b_gate, place, ones)


def _fox_kernel(q_ref, qx_ref, k_ref, kx_ref, vt_ref, o_ref, acc_ref, *, tk, unroll):
    p = pl.program_id(1)
    i = pl.program_id(2)
    tq = q_ref.shape[1]
    n_diag = tq // tk
    qt = q_ref[...]
    qxt = qx_ref[...]
    row = lax.broadcasted_iota(jnp.int32, (LANES, tq), 0)
    zero = jnp.zeros_like(qt)
    q_ext = []
    for hh in range(2):
        first = BIAS_LANES * (2 * p + hh)
        head_rows = (row < HEAD_DIM) if hh == 0 else (row >= HEAD_DIM)
        bias_rows = (row >= first) & (row < first + BIAS_LANES)
        q_ext.append(jnp.concatenate([jnp.where(head_rows, qt, zero),
                                      jnp.where(bias_rows, qxt, zero)], axis=0))

    acc_ref[...] = jnp.zeros_like(acc_ref)

    def scores(ks, c0, masked):
        k_ext = jnp.concatenate([k_ref[pl.ds(ks, tk), :], kx_ref[pl.ds(ks, tk), :]], axis=1)
        out = []
        for hh in range(2):
            s = jnp.dot(k_ext, q_ext[hh][:, c0:], preferred_element_type=F32)
            if masked:
                key = lax.broadcasted_iota(jnp.int32, s.shape, 0)
                qry = lax.broadcasted_iota(jnp.int32, s.shape, 1)
                s = jnp.where(key <= qry, s, -jnp.inf)
            out.append(s)
        return out

    def accumulate(ks, c0, s_pair, stats):
        vt = vt_ref[:, pl.ds(ks, tk)]
        out = []
        for hh in range(2):
            m_all, l_all = stats[hh]
            m_old, l_old = m_all[:, c0:], l_all[:, c0:]
            s = s_pair[hh]
            m_new = jnp.maximum(m_old, jnp.max(s, axis=0, keepdims=True))
            alpha = jnp.exp2(m_old - m_new)
            pm = jnp.exp2(s - m_new)
            l_new = alpha * l_old + jnp.sum(pm, axis=0, keepdims=True)
            acc_ref[hh, :, c0:] = alpha * acc_ref[hh, :, c0:] + jnp.dot(
                vt, pm.astype(BF16), preferred_element_type=F32)
            if c0:
                m_new = jnp.concatenate([m_all[:, :c0], m_new], axis=1)
                l_new = jnp.concatenate([l_all[:, :c0], l_new], axis=1)
            out.append((m_new, l_new))
        return tuple(out)

    def run_tiles(first_tile, count, stats, diagonal):
        start = lambda u: pl.multiple_of((first_tile + u) * tk, tk)
        col0 = lambda u: u * tk if diagonal else 0
        s_next = scores(start(0), col0(0), diagonal)
        for u in range(count):
            s_cur = s_next
            if u + 1 < count:
                s_next = scores(start(u + 1), col0(u + 1), diagonal)
            stats = accumulate(start(u), col0(u), s_cur, stats)
        return stats

    init = tuple((jnp.full((1, tq), NEG_BIG, F32), jnp.zeros((1, tq), F32)) for _ in range(2))
    stats = lax.fori_loop(0, i * (n_diag // unroll),
                          lambda jb, st: run_tiles(jb * unroll, unroll, st, False), init)
    stats = run_tiles(i * n_diag, n_diag, stats, True)

    o_t = jnp.where(row < HEAD_DIM, acc_ref[0] * (1.0 / stats[0][1]),
                    acc_ref[1] * (1.0 / stats[1][1]))
    o_ref[...] = o_t.T.astype(BF16)


def _fox_attn(qt, qxt, kf, kx, vt, *, batch, seq, tq, tk, unroll):
    T = kf.shape[0]
    nq = seq // tq
    n_pair = FOX_HEADS // 2
    return pl.pallas_call(
        functools.partial(_fox_kernel, tk=tk, unroll=unroll),
        grid=(batch, n_pair, nq),
        in_specs=[
            pl.BlockSpec((None, None, LANES, tq), lambda b, p, i: (b, p, 0, i)),
            pl.BlockSpec((None, LANES, tq), lambda b, p, i: (b, 0, i)),
            pl.BlockSpec((seq, LANES), lambda b, p, i: (b, p)),
            pl.BlockSpec((seq, LANES), lambda b, p, i: (b, 0)),
            pl.BlockSpec((None, None, LANES, seq), lambda b, p, i: (b, p, 0, 0)),
        ],
        out_specs=pl.BlockSpec((tq, LANES), lambda b, p, i: (b * nq + i, p)),
        out_shape=jax.ShapeDtypeStruct((T, FOX_W), BF16),
        scratch_shapes=[pltpu.VMEM((2, LANES, tq), F32)],
        compiler_params=_cparams(("arbitrary", "arbitrary", "arbitrary")),
        name="fox_attn",
    )(qt, qxt, kf, kx, vt)


def _swa_kernel(sink_ref, q_ref, kc_ref, kp_ref, vc_ref, vp_ref, o_ref):
    i = pl.program_id(1)
    tq = q_ref.shape[0]
    n_win = tq // WINDOW
    n_col = SWA_W // LANES
    lane = lax.broadcasted_iota(jnp.int32, (WINDOW, LANES), 1)
    low = lane < HEAD_DIM
    r = lax.broadcasted_iota(jnp.int32, (WINDOW, 2 * WINDOW), 0)
    c = lax.broadcasted_iota(jnp.int32, (WINDOW, 2 * WINDOW), 1)
    dist = WINDOW + r - c
    band = (dist >= 0) & (dist < WINDOW)
    for w in range(n_win):
        if w == 0:
            k_win = jnp.concatenate([kp_ref[...], kc_ref[0:WINDOW, :]], axis=0)
            v_win = jnp.concatenate([vp_ref[...], vc_ref[0:WINDOW, :]], axis=0)
            first = i == 0
            valid = band & ((c >= WINDOW) | jnp.logical_not(first))
        else:
            k_win = kc_ref[(w - 1) * WINDOW:(w + 1) * WINDOW, :]
            v_win = vc_ref[(w - 1) * WINDOW:(w + 1) * WINDOW, :]
            valid = band
        for col_i in range(n_col):
            qc = q_ref[w * WINDOW:(w + 1) * WINDOW, col_i * LANES:(col_i + 1) * LANES]
            outs = []
            for half in range(2):
                qm = jnp.where(low if half == 0 else jnp.logical_not(low), qc, jnp.zeros_like(qc))
                s = lax.dot_general(qm, k_win, (((1,), (1,)), ((), ())),
                                    preferred_element_type=F32)
                s = jnp.where(valid, s, -jnp.inf)
                sink = sink_ref[2 * col_i + half]
                m = jnp.maximum(jnp.max(s, axis=1, keepdims=True), sink)
                pm = jnp.exp(s - m)
                denom = jnp.sum(pm, axis=1, keepdims=True) + jnp.exp(sink - m)
                pm = pm / denom
                outs.append(jnp.dot(pm.astype(BF16), v_win, preferred_element_type=F32))
            o_ref[w * WINDOW:(w + 1) * WINDOW, col_i * LANES:(col_i + 1) * LANES] = (
                jnp.where(low, outs[0], outs[1]).astype(BF16))


def _swa_attn(sinks_perm, qs, ks, vs, *, batch, seq, tq):
    T = qs.shape[0]
    nq = seq // tq
    n_win = tq // WINDOW
    nb = seq // WINDOW
    cur = lambda w: pl.BlockSpec((tq, w), lambda b, i: (b * nq + i, 0))
    prev = pl.BlockSpec((WINDOW, SWA_KV_W),
                        lambda b, i: (jnp.maximum(b * nb + i * n_win - 1, 0), 0))
    return pl.pallas_call(
        _swa_kernel,
        grid=(batch, nq),
        in_specs=[pl.BlockSpec(memory_space=pltpu.SMEM), cur(SWA_W), cur(SWA_KV_W), prev,
                  cur(SWA_KV_W), prev],
        out_specs=cur(SWA_W),
        out_shape=jax.ShapeDtypeStruct((T, SWA_W), BF16),
        compiler_params=_cparams(("arbitrary", "arbitrary")),
        name="swa_attn",
    )(sinks_perm, qs, ks, ks, vs, vs)


ROUTE_E0, ROUTE_E1, ROUTE_G0, ROUTE_G1, ROUTE_R0, ROUTE_R1 = range(6)


def _mix_route_kernel(of_ref, os_ref, gf_ref, gs_ref, x_ref, wpf_ref, wps_ref, wout_ref, gn_ref,
                      wrh_ref, wrl_ref, br_ref, x1_ref, h2_ref, route_ref, cnt_ref, carry_ref):
    step = pl.program_id(0)
    tm = x_ref.shape[0]

    @pl.when(step == 0)
    def _():
        carry_ref[...] = jnp.zeros_like(carry_ref)

    a = jnp.dot(of_ref[...], wpf_ref[...], preferred_element_type=F32)
    b = jnp.dot(os_ref[...], wps_ref[...], preferred_element_type=F32)
    merged = gf_ref[...].astype(F32) * a + gs_ref[...].astype(F32) * b
    y = jnp.dot(merged.astype(BF16), wout_ref[...], preferred_element_type=F32)
    x1 = x_ref[...] + y
    x1_ref[...] = x1
    ms = jnp.mean(x1 * x1, axis=-1, keepdims=True)
    h2 = (x1 * lax.rsqrt(ms + EPS)) * gn_ref[...]
    h2_ref[...] = h2

    h_hi = h2.astype(BF16)
    h_lo = (h2 - h_hi.astype(F32)).astype(BF16)
    wrh = wrh_ref[...]
    logits = (jnp.dot(h_hi, wrh, preferred_element_type=F32)
              + jnp.dot(h_lo, wrh, preferred_element_type=F32)
              + jnp.dot(h_hi, wrl_ref[...], preferred_element_type=F32)) + br_ref[...]

    lane = lax.broadcasted_iota(jnp.int32, (tm, LANES), 1)
    is_g = lane < N_GROUPS
    gl = jnp.where(is_g, logits, -jnp.inf)
    gmax = jnp.max(gl, axis=1, keepdims=True)
    gexp = jnp.where(is_g, jnp.exp(gl - gmax), 0.0)
    gprob = gexp / jnp.sum(gexp, axis=1, keepdims=True)
    g_w = jnp.max(gprob, axis=1, keepdims=True)
    g_idx = jnp.min(jnp.where(is_g & (gprob == g_w), lane, LANES), axis=1, keepdims=True)

    e_lane = lane - N_GROUPS
    in_grp = (e_lane >= 0) & (e_lane < N_EXPERTS) & ((e_lane >> 3) == g_idx)
    el = jnp.where(in_grp, logits, -jnp.inf)
    v0 = jnp.max(el, axis=1, keepdims=True)
    i0 = jnp.min(jnp.where(in_grp & (el == v0), lane, LANES), axis=1, keepdims=True)
    el1 = jnp.where(lane == i0, -jnp.inf, el)
    v1 = jnp.max(el1, axis=1, keepdims=True)
    i1 = jnp.min(jnp.where(in_grp & (lane != i0) & (el1 == v1), lane, LANES), axis=1, keepdims=True)
    t = jnp.exp(v1 - v0)
    den = 1.0 + t
    gate0 = g_w * (1.0 / den)
    gate1 = g_w * (t / den)
    e0 = i0 - N_GROUPS
    e1 = i1 - N_GROUPS

    oh0 = jnp.where(lane == e0, 1.0, 0.0)
    oh1 = jnp.where(lane == e1, 1.0, 0.0)
    row = lax.broadcasted_iota(jnp.int32, (tm, tm), 0)
    col = lax.broadcasted_iota(jnp.int32, (tm, tm), 1)
    tri = jnp.where(row > col, 1.0, 0.0).astype(BF16)
    cs0 = jnp.sum(oh0, axis=0, keepdims=True)
    cs1 = jnp.sum(oh1, axis=0, keepdims=True)
    carry = carry_ref[0:1, :]
    pre0 = jnp.dot(tri, oh0.astype(BF16), preferred_element_type=F32) + carry
    pre1 = jnp.dot(tri, oh1.astype(BF16), preferred_element_type=F32) + (carry + cs0)
    rank0 = jnp.sum(oh0 * pre0, axis=1, keepdims=True)
    rank1 = jnp.sum(oh1 * pre1, axis=1, keepdims=True)
    new_carry = carry + cs0 + cs1
    carry_ref[0:1, :] = new_carry
    cnt_ref[...] = jnp.broadcast_to(new_carry, cnt_ref.shape)

    route = jnp.where(lane == ROUTE_E0, e0.astype(F32), 0.0)
    route = jnp.where(lane == ROUTE_E1, e1.astype(F32), route)
    route = jnp.where(lane == ROUTE_G0, gate0, route)
    route = jnp.where(lane == ROUTE_G1, gate1, route)
    route = jnp.where(lane == ROUTE_R0, rank0, route)
    route = jnp.where(lane == ROUTE_R1, rank1, route)
    route_ref[...] = route


def _mix_route(o_f, o_s, g_f, g_s, x2, wpf, wps, wout, g_ffn, wr_hi, wr_lo, b_r, *, tm):
    T, D = x2.shape
    row_blk = lambda w: pl.BlockSpec((tm, w), lambda i: (i, 0))
    const = lambda shape: pl.BlockSpec(shape, lambda i: (0, 0))
    return pl.pallas_call(
        _mix_route_kernel,
        grid=(T // tm,),
        in_specs=[row_blk(FOX_W), row_blk(SWA_W), row_blk(D), row_blk(D), row_blk(D),
                  const((FOX_W, D)), const((SWA_W, D)), const((D, D)), const((1, D)),
                  const((D, LANES)), const((D, LANES)), const((1, LANES))],
        out_specs=(row_blk(D), row_blk(D), row_blk(LANES), const((8, LANES))),
        out_shape=(jax.ShapeDtypeStruct((T, D), F32), jax.ShapeDtypeStruct((T, D), F32),
                   jax.ShapeDtypeStruct((T, LANES), F32), jax.ShapeDtypeStruct((8, LANES), F32)),
        scratch_shapes=[pltpu.VMEM((8, LANES), F32)],
        compiler_params=_cparams(("arbitrary",)),
        name="mix_route",
    )(o_f, o_s, g_f, g_s, x2, wpf, wps, wout, g_ffn, wr_hi, wr_lo, b_r)


def _dispatch_kernel(bounds_ref, dest_ref, h_ref, xs_ref, zbuf, sem, zsem):
    step = pl.program_id(0)
    tb = h_ref.shape[0]

    @pl.when(step == 0)
    def _():
        zbuf[...] = jnp.zeros_like(zbuf)

        def tail_copy(e):
            tail = pl.multiple_of(bounds_ref[e + 1] - ROW_BLOCK, ROW_BLOCK)
            return pltpu.make_async_copy(zbuf, xs_ref.at[pl.ds(tail, ROW_BLOCK), :], zsem)

        def for_nonempty(action):
            def body(e, carry):
                @pl.when(bounds_ref[e + 1] > bounds_ref[e])
                def _():
                    action(tail_copy(e))
                return carry
            lax.fori_loop(0, N_EXPERTS, body, 0)

        n_rows = xs_ref.shape[0]
        first_unused = bounds_ref[N_EXPERTS] // ROW_BLOCK

        def unused_copy(blk):
            return pltpu.make_async_copy(
                zbuf, xs_ref.at[pl.ds(pl.multiple_of(blk * ROW_BLOCK, ROW_BLOCK), ROW_BLOCK), :], zsem)

        def for_unused(action):
            def body(blk, carry):
                action(unused_copy(blk))
                return carry
            lax.fori_loop(first_unused, n_rows // ROW_BLOCK, body, 0)

        for_nonempty(lambda cp: cp.start())
        for_unused(lambda cp: cp.start())
        for_nonempty(lambda cp: cp.wait())
        for_unused(lambda cp: cp.wait())

    def start(t, carry):
        for k in range(2):
            d = dest_ref[2 * t + k]
            pltpu.make_async_copy(h_ref.at[pl.ds(t, 1), :], xs_ref.at[pl.ds(d, 1), :], sem).start()
        return carry

    lax.fori_loop(0, tb, start, 0, unroll=DMA_UNROLL)
    for _ in range(2):
        pltpu.make_async_copy(h_ref, xs_ref.at[pl.ds(0, tb), :], sem).wait()


def _dispatch(bounds, dest_flat, h2, n_rows, *, tb):
    T, D = h2.shape
    grid_spec = pltpu.PrefetchScalarGridSpec(
        num_scalar_prefetch=1,
        grid=(T // tb,),
        in_specs=[pl.BlockSpec((2 * tb,), lambda i, bd: (i,), memory_space=pltpu.SMEM),
                  pl.BlockSpec((tb, D), lambda i, bd: (i, 0))],
        out_specs=pl.BlockSpec(memory_space=pl.ANY),
        scratch_shapes=[pltpu.VMEM((ROW_BLOCK, D), F32), pltpu.SemaphoreType.DMA(()),
                        pltpu.SemaphoreType.DMA(())],
    )
    return pl.pallas_call(
        _dispatch_kernel,
        grid_spec=grid_spec,
        out_shape=jax.ShapeDtypeStruct((n_rows, D), F32),
        compiler_params=_cparams(("arbitrary",)),
        name="dispatch",
    )(bounds, dest_flat, h2)


def _experts_kernel(be_ref, nu_ref, xs_ref, w1_ref, w3_ref, w2_ref, ys_ref, w1b, w3b, w2b):
    i = pl.program_id(0)
    e = be_ref[i]
    e_prev = be_ref[jnp.maximum(i - 1, 0)]
    used = i < nu_ref[0]

    @pl.when(used & ((i == 0) | (e != e_prev)))
    def _():
        w1b[...] = w1_ref[...].astype(BF16)
        w3b[...] = w3_ref[...].astype(BF16)
        w2b[...] = w2_ref[...].astype(BF16)

    @pl.when(used)
    def _():
        xb = xs_ref[...].astype(BF16)
        h1 = jnp.dot(xb, w1b[...], preferred_element_type=F32)
        h3 = jnp.dot(xb, w3b[...], preferred_element_type=F32)
        act = (h1 * jax.nn.sigmoid(h1)) * h3
        ys_ref[...] = jnp.dot(act.astype(BF16), w2b[...], preferred_element_type=F32)

    @pl.when(jnp.logical_not(used))
    def _():
        ys_ref[...] = jnp.zeros_like(ys_ref)


def _experts(blk_expert, n_used, xs, w1, w3, w2):
    P, D = xs.shape
    de = w1.shape[-1]
    n_blk = P // ROW_BLOCK
    grid_spec = pltpu.PrefetchScalarGridSpec(
        num_scalar_prefetch=2,
        grid=(n_blk,),
        in_specs=[pl.BlockSpec((ROW_BLOCK, D),
                               lambda i, be, nu: (jnp.minimum(i, jnp.maximum(nu[0] - 1, 0)), 0)),
                  pl.BlockSpec((None, D, de), lambda i, be, nu: (be[i], 0, 0)),
                  pl.BlockSpec((None, D, de), lambda i, be, nu: (be[i], 0, 0)),
                  pl.BlockSpec((None, de, D), lambda i, be, nu: (be[i], 0, 0))],
        out_specs=pl.BlockSpec((ROW_BLOCK, D), lambda i, be, nu: (i, 0)),
        scratch_shapes=[pltpu.VMEM((D, de), BF16), pltpu.VMEM((D, de), BF16),
                        pltpu.VMEM((de, D), BF16)],
    )
    return pl.pallas_call(
        _experts_kernel,
        grid_spec=grid_spec,
        out_shape=jax.ShapeDtypeStruct((P, D), F32),
        compiler_params=_cparams(("arbitrary",)),
        name="experts",
    )(blk_expert, n_used, xs, w1, w3, w2)


def _combine_kernel(dest_ref, x1_ref, route_ref, gfin_ref, ys_ref, o_ref, ybuf, sem):
    tb = x1_ref.shape[0]

    def start(t, carry):
        for k in range(2):
            d = dest_ref[2 * t + k]
            pltpu.make_async_copy(ys_ref.at[pl.ds(d, 1), :], ybuf.at[k, pl.ds(t, 1), :],
                                  sem).start()
        return carry

    lax.fori_loop(0, tb, start, 0, unroll=DMA_UNROLL)
    for k in range(2):
        pltpu.make_async_copy(ys_ref.at[pl.ds(0, tb), :], ybuf.at[k], sem).wait()

    route = route_ref[...]
    lane = lax.broadcasted_iota(jnp.int32, route.shape, 1)
    g0 = jnp.sum(jnp.where(lane == ROUTE_G0, route, 0.0), axis=1, keepdims=True)
    g1 = jnp.sum(jnp.where(lane == ROUTE_G1, route, 0.0), axis=1, keepdims=True)
    moe = g0 * ybuf[0] + g1 * ybuf[1]
    x2 = x1_ref[...] + moe
    ms = jnp.mean(x2 * x2, axis=-1, keepdims=True)
    o_ref[...] = (x2 * lax.rsqrt(ms + EPS)) * gfin_ref[...]


def _combine(dest_flat, x1, route, g_final, ys, *, tb):
    T, D = x1.shape
    return pl.pallas_call(
        _combine_kernel,
        grid=(T // tb,),
        in_specs=[pl.BlockSpec((2 * tb,), lambda i: (i,), memory_space=pltpu.SMEM),
                  pl.BlockSpec((tb, D), lambda i: (i, 0)),
                  pl.BlockSpec((tb, LANES), lambda i: (i, 0)),
                  pl.BlockSpec((1, D), lambda i: (0, 0)),
                  pl.BlockSpec(memory_space=pl.ANY)],
        out_specs=pl.BlockSpec((tb, D), lambda i: (i, 0)),
        out_shape=jax.ShapeDtypeStruct((T, D), F32),
        scratch_shapes=[pltpu.VMEM((2, tb, D), F32), pltpu.SemaphoreType.DMA(())],
        compiler_params=_cparams(("arbitrary",)),
        name="combine",
    )(dest_flat, x1, route, g_final, ys)


def _rot_half_cols(w, n_heads):
    d = w.shape[0]
    half = HEAD_DIM // 2
    w4 = w.reshape(d, n_heads, 2, half)
    return jnp.concatenate([-w4[:, :, 1:2, :], w4[:, :, 0:1, :]], axis=2).reshape(d, n_heads * HEAD_DIM)


def _pick(n, prefs):
    for p in prefs:
        if n % p == 0:
            return p
    return n


def _layer(x2, pos2, batch, seq, attn_norm, w_in, b_forget, b_gate, sinks, w_proj_fox, w_proj_swa,
           w_out, ffn_norm, w_group, b_group, w_expert, b_expert, w1, w3, w2, final_norm):
    T, D = x2.shape
    scale = HEAD_DIM ** -0.5

    sp = np.cumsum([FOX_W, FOX_W, FOX_W, FOX_HEADS, SWA_W, SWA_KV_W, SWA_KV_W, D, D])[:-1]
    wq_f, wk_f, wv_f, w_fl, wq_s, wk_s, wv_s, wg_f, wg_s = jnp.split(w_in, sp, axis=1)
    head_order = np.asarray(SWA_HEAD_ORDER)
    wq_s = wq_s.reshape(D, SWA_HEADS, HEAD_DIM)[:, head_order, :].reshape(D, SWA_W)
    w_fl_pad = jnp.pad(w_fl, ((0, 0), (0, LANES - FOX_HEADS)))
    w_all = jnp.concatenate(
        [wq_f * (scale * LOG2E), wk_f, wv_f, wq_s * scale, _rot_half_cols(wq_s, SWA_HEADS) * scale,
         wk_s, _rot_half_cols(wk_s, SWA_KV_HEADS), wv_s, wg_f, wg_s, w_fl_pad], axis=1).astype(BF16)
    half = HEAD_DIM // 2
    inv_freq = ROPE_THETA ** (-jnp.arange(half, dtype=F32) * 2.0 / HEAD_DIM)
    invf = jnp.tile(inv_freq, LANES // half).reshape(1, LANES)
    bf_pad = jnp.pad(b_forget.astype(F32), (0, LANES - FOX_HEADS)).reshape(1, LANES)
    sinks_perm = sinks.astype(F32)[head_order]
    wps = w_proj_swa.reshape(SWA_HEADS, HEAD_DIM, D)[head_order].reshape(SWA_W, D).astype(BF16)
    wpf = w_proj_fox.astype(BF16)
    wout = w_out.astype(BF16)
    w_r = jnp.pad(jnp.concatenate([w_group, w_expert], axis=1).astype(F32),
                  ((0, 0), (0, LANES - N_GROUPS - N_EXPERTS)))
    wr_hi = w_r.astype(BF16)
    wr_lo = (w_r - wr_hi.astype(F32)).astype(BF16)
    b_r = jnp.pad(jnp.concatenate([b_group, b_expert]).astype(F32),
                  (0, LANES - N_GROUPS - N_EXPERTS)).reshape(1, LANES)

    tm1 = _pick(seq, (512, 256, 128))
    qf, kf, vf, qs, ks, vs, g_f, g_s, qx, kx = _in_proj(
        x2, pos2, attn_norm.reshape(1, D), w_all, invf, bf_pad, b_gate.astype(F32),
        batch=batch, seq=seq, tm=tm1)
    vt = vf.reshape(batch, seq, FOX_HEADS // 2, LANES).transpose(0, 2, 3, 1)
    qt = qf.reshape(batch, seq, FOX_HEADS // 2, LANES).transpose(0, 2, 3, 1)
    qxt = qx.reshape(batch, seq, LANES).transpose(0, 2, 1)
    tq = _pick(seq, (FOX_TQ, 128))
    tk = min(tq, FOX_TK)
    o_f = _fox_attn(qt, qxt, kf, kx, vt, batch=batch, seq=seq, tq=tq, tk=tk,
                    unroll=min(FOX_UNROLL, tq // tk))
    o_s = _swa_attn(sinks_perm, qs, ks, vs, batch=batch, seq=seq, tq=_pick(seq, (512, 256, 128)))

    tm3 = _pick(T, (512, 256, 128))
    x1, h2, route, counts = _mix_route(o_f, o_s, g_f, g_s, x2, wpf, wps, wout,
                                       ffn_norm.reshape(1, D), wr_hi, wr_lo, b_r, tm=tm3)

    cnt = counts[0, :N_EXPERTS].astype(jnp.int32)
    padded = (cnt + ROW_BLOCK - 1) // ROW_BLOCK * ROW_BLOCK
    pad_end = jnp.cumsum(padded)
    pad_start = pad_end - padded
    bounds = jnp.concatenate([jnp.zeros((1,), jnp.int32), pad_end]).astype(jnp.int32)
    e_ids = route[:, ROUTE_E0:ROUTE_E1 + 1].astype(jnp.int32)
    ranks = route[:, ROUTE_R0:ROUTE_R1 + 1].astype(jnp.int32)
    dest_flat = (pad_start[e_ids] + ranks).reshape(-1)
    P = 2 * T + N_EXPERTS * ROW_BLOCK
    n_blk = P // ROW_BLOCK
    blk_pos = jnp.arange(n_blk, dtype=jnp.int32) * ROW_BLOCK
    blk_expert = jnp.minimum(
        jnp.sum((pad_end[None, :] <= blk_pos[:, None]).astype(jnp.int32), axis=1), N_EXPERTS - 1)
    n_used = (pad_end[-1:] // ROW_BLOCK).astype(jnp.int32)

    tb = _pick(T, (256, 128))
    xs = _dispatch(bounds, dest_flat, h2, P, tb=tb)
    ys = _experts(blk_expert, n_used, xs, w1, w3, w2)
    return _combine(dest_flat, x1, route, final_norm.reshape(1, D), ys, tb=tb)


def kernel(x, positions, attn_norm, w_in, b_forget, b_gate, attn_sinks, w_proj_fox, w_proj_swa,
           w_out, ffn_norm, w_group, b_group, w_expert, b_expert, w1, w3, w2, final_norm):
    B, S, D = x.shape
    depth = attn_norm.shape[0]
    x2 = x.reshape(B * S, D)
    pos2 = positions.reshape(B * S, 1).astype(jnp.int32)
    assert depth == 1, "a single layer is followed directly by the final norm"
    out = _layer(x2, pos2, B, S, attn_norm[0], w_in[0], b_forget[0], b_gate[0], attn_sinks[0],
                 w_proj_fox[0], w_proj_swa[0], w_out[0], ffn_norm[0], w_group[0], b_group[0],
                 w_expert[0], b_expert[0], w1[0], w3[0], w2[0], final_norm)
    return out.reshape(B, S, D)
```

```python
import functools

import numpy as np
import jax
import jax.numpy as jnp
from jax import lax
from jax.experimental import pallas as pl
from jax.experimental.pallas import tpu as pltpu

HEAD_DIM = 64
FOX_HEADS = 8
SWA_HEADS = 8
SWA_KV_HEADS = 2
WINDOW = 128
ROPE_THETA = 10000.0
N_GROUPS = 8
EXPERTS_PER_GROUP = 8
N_EXPERTS = N_GROUPS * EXPERTS_PER_GROUP
ROW_BLOCK = 256
EPS = 1e-6
LANES = 128
NEG_BIG = -1e30
LOG2E = 1.4426950408889634
BIAS_TERMS = 3
BIAS_LANES = 2 * BIAS_TERMS
FOX_TQ = 1024
FOX_TK = 256
FOX_UNROLL = 4

FOX_W = FOX_HEADS * HEAD_DIM
SWA_W = SWA_HEADS * HEAD_DIM
SWA_KV_W = SWA_KV_HEADS * HEAD_DIM
SWA_HEAD_ORDER = (0, 4, 1, 5, 2, 6, 3, 7)

F32 = jnp.float32
BF16 = jnp.bfloat16
VMEM_LIMIT = 56 * 1024 * 1024


def _cparams(sem):
    return pltpu.CompilerParams(dimension_semantics=sem, vmem_limit_bytes=VMEM_LIMIT)


def _split3(x):
    hi = x.astype(BF16)
    r1 = x - hi.astype(F32)
    mid = r1.astype(BF16)
    lo = (r1 - mid.astype(F32)).astype(BF16)
    return hi, mid, lo


C_QF, C_KF, C_VF = 0, FOX_W, 2 * FOX_W
C_QS = 3 * FOX_W
C_QSR = C_QS + SWA_W
C_KS = C_QSR + SWA_W
C_KSR = C_KS + SWA_KV_W
C_VS = C_KSR + SWA_KV_W
C_GF = C_VS + SWA_KV_W


def _in_proj_kernel(x_ref, pos_ref, g_ref, w_ref, invf_ref, bf_ref, bg_ref, place_ref, ones_ref,
                    qf_ref, kf_ref, vf_ref, qs_ref, ks_ref, vs_ref, gf_ref, gs_ref, qx_ref, kx_ref,
                    carry_ref, *, d_model):
    j = pl.program_id(1)
    tm = x_ref.shape[0]
    c_gs = C_GF + d_model
    c_fl = c_gs + d_model

    @pl.when(j == 0)
    def _():
        carry_ref[...] = jnp.zeros_like(carry_ref)

    x = x_ref[...]
    ms = jnp.mean(x * x, axis=-1, keepdims=True)
    h = ((x * lax.rsqrt(ms + EPS)) * g_ref[...]).astype(BF16)

    def mm(lo, hi):
        return jnp.dot(h, w_ref[:, lo:hi], preferred_element_type=F32)

    def store_pairs_t(ref, y):
        for pr in range(ref.shape[0]):
            ref[pr] = y[:, pr * LANES:(pr + 1) * LANES].T.astype(BF16)

    store_pairs_t(qf_ref, mm(C_QF, C_KF))
    kf_ref[...] = mm(C_KF, C_VF).astype(BF16)
    store_pairs_t(vf_ref, mm(C_VF, C_QS))

    ang = pos_ref[...].astype(F32) * invf_ref[...]
    cos = jnp.cos(ang)
    sin = jnp.sin(ang)
    n_rep = SWA_W // LANES
    cos_q = jnp.concatenate([cos] * n_rep, axis=1)
    sin_q = jnp.concatenate([sin] * n_rep, axis=1)
    qs_ref[...] = (mm(C_QS, C_QSR) * cos_q + mm(C_QSR, C_KS) * sin_q).astype(BF16)
    ks_ref[...] = (mm(C_KS, C_KSR) * cos + mm(C_KSR, C_VS) * sin).astype(BF16)
    vs_ref[...] = mm(C_VS, C_GF).astype(BF16)

    gf_ref[...] = jax.nn.sigmoid(mm(C_GF, c_gs) + bg_ref[0:1, :]).astype(BF16)
    gs_ref[...] = jax.nn.sigmoid(mm(c_gs, c_fl) + bg_ref[1:2, :]).astype(BF16)

    z = mm(c_fl, c_fl + LANES) + bf_ref[...]
    lane = lax.broadcasted_iota(jnp.int32, (tm, LANES), 1)
    lf = jnp.minimum(z, 0.0) - jnp.log1p(jnp.exp(-jnp.abs(z)))
    lf = jnp.where(lane < FOX_HEADS, lf, 0.0)
    row = lax.broadcasted_iota(jnp.int32, (tm, tm), 0)
    col = lax.broadcasted_iota(jnp.int32, (tm, tm), 1)
    tri = jnp.where(row >= col, 1.0, 0.0).astype(BF16)
    hi, mid, lo = _split3(lf)
    c = (jnp.dot(tri, hi, preferred_element_type=F32)
         + jnp.dot(tri, mid, preferred_element_type=F32)
         + jnp.dot(tri, lo, preferred_element_type=F32)) + carry_ref[0:1, :]
    carry_ref[0:1, :] = c[tm - 1:tm, :]
    chi, cmid, clo = _split3(c * LOG2E)
    ext = jnp.dot(jnp.concatenate([chi, cmid, clo], axis=1), place_ref[...],
                  preferred_element_type=F32) + ones_ref[...]
    qx_ref[...] = ext[:, :LANES].T.astype(BF16)
    kx_ref[...] = ext[:, LANES:].astype(BF16)


def _bias_placement():
    place = np.zeros((BIAS_TERMS * LANES, 2 * LANES), np.float32)
    ones = np.zeros((1, 2 * LANES), np.float32)
    for h in range(FOX_HEADS):
        for t in range(BIAS_TERMS):
            place[t * LANES + h, BIAS_LANES * h + t] = 1.0
            place[t * LANES + h, LANES + BIAS_LANES * h + BIAS_TERMS + t] = -1.0
            ones[0, BIAS_LANES * h + BIAS_TERMS + t] = 1.0
            ones[0, LANES + BIAS_LANES * h + t] = 1.0
    return jnp.asarray(place, BF16), jnp.asarray(ones, F32)


def _in_proj(x2, pos2, g_attn, w_all, invf, bf_pad, b_gate, *, batch, seq, tm):
    T, D = x2.shape
    nj = seq // tm
    n_all = w_all.shape[1]
    place, ones = _bias_placement()
    row_blk = lambda w: pl.BlockSpec((tm, w), lambda b, j: (b * nj + j, 0))
    const = lambda shape: pl.BlockSpec(shape, lambda b, j: (0, 0))
    n_pair = FOX_HEADS // 2
    pair_t = pl.BlockSpec((None, n_pair, LANES, tm), lambda b, j: (b, 0, 0, j))
    out_shapes = (
        jax.ShapeDtypeStruct((batch, n_pair, LANES, seq), BF16), jax.ShapeDtypeStruct((T, FOX_W), BF16),
        jax.ShapeDtypeStruct((batch, n_pair, LANES, seq), BF16), jax.ShapeDtypeStruct((T, SWA_W), BF16),
        jax.ShapeDtypeStruct((T, SWA_KV_W), BF16), jax.ShapeDtypeStruct((T, SWA_KV_W), BF16),
        jax.ShapeDtypeStruct((T, D), BF16), jax.ShapeDtypeStruct((T, D), BF16),
        jax.ShapeDtypeStruct((batch, LANES, seq), BF16), jax.ShapeDtypeStruct((T, LANES), BF16),
    )
    return pl.pallas_call(
        functools.partial(_in_proj_kernel, d_model=D),
        grid=(batch, nj),
        in_specs=[row_blk(D), row_blk(1), const((1, D)), const((D, n_all)), const((1, LANES)),
                  const((1, LANES)), const((2, D)), const(place.shape), const(ones.shape)],
        out_specs=(pair_t, row_blk(FOX_W), pair_t, row_blk(SWA_W),
                   row_blk(SWA_KV_W), row_blk(SWA_KV_W), row_blk(D), row_blk(D),
                   pl.BlockSpec((None, LANES, tm), lambda b, j: (b, 0, j)), row_blk(LANES)),
        out_shape=out_shapes,
        scratch_shapes=[pltpu.VMEM((8, LANES), F32)],
        compiler_params=_cparams(("arbitrary", "arbitrary")),
        name="in_proj",
    )(x2, pos2, g_attn, w_all, invf, bf_pad, b_gate, place, ones)


def _fox_kernel(q_ref, qx_ref, k_ref, kx_ref, vt_ref, o_ref, acc_ref, *, tk, unroll):
    p = pl.program_id(1)
    i = pl.program_id(2)
    tq = q_ref.shape[1]
    n_diag = tq // tk
    qt = q_ref[...]
    qxt = qx_ref[...]
    row = lax.broadcasted_iota(jnp.int32, (LANES, tq), 0)
    zero = jnp.zeros_like(qt)
    q_ext = []
    for hh in range(2):
        first = BIAS_LANES * (2 * p + hh)
        head_rows = (row < HEAD_DIM) if hh == 0 else (row >= HEAD_DIM)
        bias_rows = (row >= first) & (row < first + BIAS_LANES)
        q_ext.append(jnp.concatenate([jnp.where(head_rows, qt, zero),
                                      jnp.where(bias_rows, qxt, zero)], axis=0))

    acc_ref[...] = jnp.zeros_like(acc_ref)

    def scores(ks, c0, masked):
        k_ext = jnp.concatenate([k_ref[pl.ds(ks, tk), :], kx_ref[pl.ds(ks, tk), :]], axis=1)
        out = []
        for hh in range(2):
            s = jnp.dot(k_ext, q_ext[hh][:, c0:], preferred_element_type=F32)
            if masked:
                key = lax.broadcasted_iota(jnp.int32, s.shape, 0)
                qry = lax.broadcasted_iota(jnp.int32, s.shape, 1)
                s = jnp.where(key <= qry, s, -jnp.inf)
            out.append(s)
        return out

    def accumulate(ks, c0, s_pair, stats):
        vt = vt_ref[:, pl.ds(ks, tk)]
        out = []
        for hh in range(2):
            m_all, l_all = stats[hh]
            m_old, l_old = m_all[:, c0:], l_all[:, c0:]
            s = s_pair[hh]
            m_new = jnp.maximum(m_old, jnp.max(s, axis=0, keepdims=True))
            alpha = jnp.exp2(m_old - m_new)
            pm = jnp.exp2(s - m_new)
            l_new = alpha * l_old + jnp.sum(pm, axis=0, keepdims=True)
            acc_ref[hh, :, c0:] = alpha * acc_ref[hh, :, c0:] + jnp.dot(
                vt, pm.astype(BF16), preferred_element_type=F32)
            if c0:
                m_new = jnp.concatenate([m_all[:, :c0], m_new], axis=1)
                l_new = jnp.concatenate([l_all[:, :c0], l_new], axis=1)
            out.append((m_new, l_new))
        return tuple(out)

    def run_tiles(first_tile, count, stats, diagonal):
        start = lambda u: pl.multiple_of((first_tile + u) * tk, tk)
        col0 = lambda u: u * tk if diagonal else 0
        s_next = scores(start(0), col0(0), diagonal)
        for u in range(count):
            s_cur = s_next
            if u + 1 < count:
                s_next = scores(start(u + 1), col0(u + 1), diagonal)
            stats = accumulate(start(u), col0(u), s_cur, stats)
        return stats

    init = tuple((jnp.full((1, tq), NEG_BIG, F32), jnp.zeros((1, tq), F32)) for _ in range(2))
    stats = lax.fori_loop(0, i * (n_diag // unroll),
                          lambda jb, st: run_tiles(jb * unroll, unroll, st, False), init)
    stats = run_tiles(i * n_diag, n_diag, stats, True)

    o_t = jnp.where(row < HEAD_DIM, acc_ref[0] * (1.0 / stats[0][1]),
                    acc_ref[1] * (1.0 / stats[1][1]))
    o_ref[...] = o_t.T.astype(BF16)


def _fox_attn(qt, qxt, kf, kx, vt, *, batch, seq, tq, tk, unroll):
    T = kf.shape[0]
    nq = seq // tq
    n_pair = FOX_HEADS // 2
    return pl.pallas_call(
        functools.partial(_fox_kernel, tk=tk, unroll=unroll),
        grid=(batch, n_pair, nq),
        in_specs=[
            pl.BlockSpec((None, None, LANES, tq), lambda b, p, i: (b, p, 0, i)),
            pl.BlockSpec((None, LANES, tq), lambda b, p, i: (b, 0, i)),
            pl.BlockSpec((seq, LANES), lambda b, p, i: (b, p)),
            pl.BlockSpec((seq, LANES), lambda b, p, i: (b, 0)),
            pl.BlockSpec((None, None, LANES, seq), lambda b, p, i: (b, p, 0, 0)),
        ],
        out_specs=pl.BlockSpec((tq, LANES), lambda b, p, i: (b * nq + i, p)),
        out_shape=jax.ShapeDtypeStruct((T, FOX_W), BF16),
        scratch_shapes=[pltpu.VMEM((2, LANES, tq), F32)],
        compiler_params=_cparams(("arbitrary", "arbitrary", "arbitrary")),
        name="fox_attn",
    )(qt, qxt, kf, kx, vt)


def _swa_kernel(sink_ref, q_ref, kc_ref, kp_ref, vc_ref, vp_ref, o_ref):
    i = pl.program_id(1)
    tq = q_ref.shape[0]
    n_win = tq // WINDOW
    n_col = SWA_W // LANES
    lane = lax.broadcasted_iota(jnp.int32, (WINDOW, LANES), 1)
    low = lane < HEAD_DIM
    r = lax.broadcasted_iota(jnp.int32, (WINDOW, 2 * WINDOW), 0)
    c = lax.broadcasted_iota(jnp.int32, (WINDOW, 2 * WINDOW), 1)
    dist = WINDOW + r - c
    band = (dist >= 0) & (dist < WINDOW)
    for w in range(n_win):
        if w == 0:
            k_win = jnp.concatenate([kp_ref[...], kc_ref[0:WINDOW, :]], axis=0)
            v_win = jnp.concatenate([vp_ref[...], vc_ref[0:WINDOW, :]], axis=0)
            first = i == 0
            valid = band & ((c >= WINDOW) | jnp.logical_not(first))
        else:
            k_win = kc_ref[(w - 1) * WINDOW:(w + 1) * WINDOW, :]
            v_win = vc_ref[(w - 1) * WINDOW:(w + 1) * WINDOW, :]
            valid = band
        for col_i in range(n_col):
            qc = q_ref[w * WINDOW:(w + 1) * WINDOW, col_i * LANES:(col_i + 1) * LANES]
            outs = []
            for half in range(2):
                qm = jnp.where(low if half == 0 else jnp.logical_not(low), qc, jnp.zeros_like(qc))
                s = lax.dot_general(qm, k_win, (((1,), (1,)), ((), ())),
                                    preferred_element_type=F32)
                s = jnp.where(valid, s, -jnp.inf)
                sink = sink_ref[2 * col_i + half]
                m = jnp.maximum(jnp.max(s, axis=1, keepdims=True), sink)
                pm = jnp.exp(s - m)
                denom = jnp.sum(pm, axis=1, keepdims=True) + jnp.exp(sink - m)
                pm = pm / denom
                outs.append(jnp.dot(pm.astype(BF16), v_win, preferred_element_type=F32))
            o_ref[w * WINDOW:(w + 1) * WINDOW, col_i * LANES:(col_i + 1) * LANES] = (
                jnp.where(low, outs[0], outs[1]).astype(BF16))


def _swa_attn(sinks_perm, qs, ks, vs, *, batch, seq, tq):
    T = qs.shape[0]
    nq = seq // tq
    n_win = tq // WINDOW
    nb = seq // WINDOW
    cur = lambda w: pl.BlockSpec((tq, w), lambda b, i: (b * nq + i, 0))
    prev = pl.BlockSpec((WINDOW, SWA_KV_W),
                        lambda b, i: (jnp.maximum(b * nb + i * n_win - 1, 0), 0))
    return pl.pallas_call(
        _swa_kernel,
        grid=(batch, nq),
        in_specs=[pl.BlockSpec(memory_space=pltpu.SMEM), cur(SWA_W), cur(SWA_KV_W), prev,
                  cur(SWA_KV_W), prev],
        out_specs=cur(SWA_W),
        out_shape=jax.ShapeDtypeStruct((T, SWA_W), BF16),
        compiler_params=_cparams(("arbitrary", "arbitrary")),
        name="swa_attn",
    )(sinks_perm, qs, ks, ks, vs, vs)


ROUTE_E0, ROUTE_E1, ROUTE_G0, ROUTE_G1, ROUTE_R0, ROUTE_R1 = range(6)


def _mix_route_kernel(of_ref, os_ref, gf_ref, gs_ref, x_ref, wpf_ref, wps_ref, wout_ref, gn_ref,
                      wr_ref, br_ref, x1_ref, h2_ref, route_ref, cnt_ref, carry_ref):
    step = pl.program_id(0)
    tm = x_ref.shape[0]

    @pl.when(step == 0)
    def _():
        carry_ref[...] = jnp.zeros_like(carry_ref)

    a = jnp.dot(of_ref[...], wpf_ref[...], preferred_element_type=F32)
    b = jnp.dot(os_ref[...], wps_ref[...], preferred_element_type=F32)
    merged = gf_ref[...].astype(F32) * a + gs_ref[...].astype(F32) * b
    y = jnp.dot(merged.astype(BF16), wout_ref[...], preferred_element_type=F32)
    x1 = x_ref[...] + y
    x1_ref[...] = x1
    ms = jnp.mean(x1 * x1, axis=-1, keepdims=True)
    h2 = (x1 * lax.rsqrt(ms + EPS)) * gn_ref[...]
    h2_ref[...] = h2

    h_hi = h2.astype(BF16)
    h_lo = (h2 - h_hi.astype(F32)).astype(BF16)
    wr = wr_ref[...]
    r_hi = jnp.dot(h_hi, wr, preferred_element_type=F32)
    logits = (r_hi[:, :LANES] + r_hi[:, LANES:]
              + jnp.dot(h_lo, wr[:, :LANES], preferred_element_type=F32)) + br_ref[...]

    lane = lax.broadcasted_iota(jnp.int32, (tm, LANES), 1)
    is_g = lane < N_GROUPS
    gl = jnp.where(is_g, logits, -jnp.inf)
    gmax = jnp.max(gl, axis=1, keepdims=True)
    gexp = jnp.where(is_g, jnp.exp(gl - gmax), 0.0)
    gprob = gexp / jnp.sum(gexp, axis=1, keepdims=True)
    g_w = jnp.max(gprob, axis=1, keepdims=True)
    g_idx = jnp.min(jnp.where(is_g & (gprob == g_w), lane, LANES), axis=1, keepdims=True)

    e_lane = lane - N_GROUPS
    in_grp = (e_lane >= 0) & (e_lane < N_EXPERTS) & ((e_lane >> 3) == g_idx)
    el = jnp.where(in_grp, logits, -jnp.inf)
    v0 = jnp.max(el, axis=1, keepdims=True)
    i0 = jnp.min(jnp.where(in_grp & (el == v0), lane, LANES), axis=1, keepdims=True)
    el1 = jnp.where(lane == i0, -jnp.inf, el)
    v1 = jnp.max(el1, axis=1, keepdims=True)
    i1 = jnp.min(jnp.where(in_grp & (lane != i0) & (el1 == v1), lane, LANES), axis=1, keepdims=True)
    t = jnp.exp(v1 - v0)
    den = 1.0 + t
    gate0 = g_w * (1.0 / den)
    gate1 = g_w * (t / den)
    e0 = i0 - N_GROUPS
    e1 = i1 - N_GROUPS

    oh0 = jnp.where(lane == e0, 1.0, 0.0)
    oh1 = jnp.where(lane == e1, 1.0, 0.0)
    row = lax.broadcasted_iota(jnp.int32, (tm, tm), 0)
    col = lax.broadcasted_iota(jnp.int32, (tm, tm), 1)
    tri = jnp.where(row > col, 1.0, 0.0).astype(BF16)
    cs0 = jnp.sum(oh0, axis=0, keepdims=True)
    cs1 = jnp.sum(oh1, axis=0, keepdims=True)
    carry = carry_ref[0:1, :]
    pre = jnp.dot(tri, jnp.concatenate([oh0, oh1], axis=1).astype(BF16),
                  preferred_element_type=F32)
    pre0 = pre[:, :LANES] + carry
    pre1 = pre[:, LANES:] + (carry + cs0)
    rank0 = jnp.sum(oh0 * pre0, axis=1, keepdims=True)
    rank1 = jnp.sum(oh1 * pre1, axis=1, keepdims=True)
    new_carry = carry + cs0 + cs1
    carry_ref[0:1, :] = new_carry
    cnt_ref[...] = jnp.broadcast_to(new_carry, cnt_ref.shape)

    route = jnp.where(lane == ROUTE_E0, e0.astype(F32), 0.0)
    route = jnp.where(lane == ROUTE_E1, e1.astype(F32), route)
    route = jnp.where(lane == ROUTE_G0, gate0, route)
    route = jnp.where(lane == ROUTE_G1, gate1, route)
    route = jnp.where(lane == ROUTE_R0, rank0, route)
    route = jnp.where(lane == ROUTE_R1, rank1, route)
    route_ref[...] = route


def _mix_route(o_f, o_s, g_f, g_s, x2, wpf, wps, wout, g_ffn, wr_cat, b_r, *, tm):
    T, D = x2.shape
    row_blk = lambda w: pl.BlockSpec((tm, w), lambda i: (i, 0))
    const = lambda shape: pl.BlockSpec(shape, lambda i: (0, 0))
    return pl.pallas_call(
        _mix_route_kernel,
        grid=(T // tm,),
        in_specs=[row_blk(FOX_W), row_blk(SWA_W), row_blk(D), row_blk(D), row_blk(D),
                  const((FOX_W, D)), const((SWA_W, D)), const((D, D)), const((1, D)),
                  const((D, 2 * LANES)), const((1, LANES))],
        out_specs=(row_blk(D), row_blk(D), row_blk(LANES), const((8, LANES))),
        out_shape=(jax.ShapeDtypeStruct((T, D), F32), jax.ShapeDtypeStruct((T, D), F32),
                   jax.ShapeDtypeStruct((T, LANES), F32), jax.ShapeDtypeStruct((8, LANES), F32)),
        scratch_shapes=[pltpu.VMEM((8, LANES), F32)],
        compiler_params=_cparams(("arbitrary",)),
        name="mix_route",
    )(o_f, o_s, g_f, g_s, x2, wpf, wps, wout, g_ffn, wr_cat, b_r)


def _experts_kernel(be_ref, nu_ref, src_cur, src_next, dst_prev, dst_cur, h_ref, w1_ref, w3_ref,
                    w2_ref, planes_ref, xbuf, ybuf, w1b, w3b, w2b, gsem, ssem):
    i = pl.program_id(0)
    n_used = nu_ref[0]
    e = be_ref[i]
    e_prev = be_ref[jnp.maximum(i - 1, 0)]
    used = i < n_used
    slot = i % 2
    other = 1 - slot
    rows = xbuf.shape[1]
    junk0 = planes_ref.shape[0] - rows

    def gather_row(idx_ref, r, buf_slot):
        return pltpu.make_async_copy(h_ref.at[pl.ds(idx_ref[r], 1), :],
                                     xbuf.at[buf_slot, pl.ds(r, 1), :], gsem.at[buf_slot])

    def scatter_row(idx_ref, r, buf_slot):
        return pltpu.make_async_copy(ybuf.at[buf_slot, pl.ds(r, 1), :],
                                     planes_ref.at[pl.ds(idx_ref[r], 1), :], ssem.at[buf_slot])

    def wait_gather(buf_slot):
        pltpu.make_async_copy(h_ref.at[pl.ds(0, rows), :], xbuf.at[buf_slot],
                              gsem.at[buf_slot]).wait()

    def wait_scatter(buf_slot):
        pltpu.make_async_copy(ybuf.at[buf_slot], planes_ref.at[pl.ds(0, rows), :],
                              ssem.at[buf_slot]).wait()

    @pl.when(i == 0)
    def _():
        ybuf[...] = jnp.zeros_like(ybuf)
        pltpu.make_async_copy(ybuf.at[0], planes_ref.at[pl.ds(junk0, rows), :], ssem.at[0]).start()
        for r in range(rows):
            gather_row(src_cur, r, 0).start()

    @pl.when(used & ((i == 0) | (e != e_prev)))
    def _():
        w1b[...] = w1_ref[...].astype(BF16)
        w3b[...] = w3_ref[...].astype(BF16)
        w2b[...] = w2_ref[...].astype(BF16)

    @pl.when(used)
    def _():
        wait_gather(slot)
        wait_scatter(slot)
        xb = xbuf[slot].astype(BF16)
        de = w1b.shape[1]
        n_part = 2
        part = de // n_part
        n_chunk = 2 * n_part
        per_chunk = rows // n_chunk

        def issue(chunk):
            for r in range(chunk * per_chunk, (chunk + 1) * per_chunk):
                gather_row(src_next, r, other).start()
                scatter_row(dst_prev, r, other).start()

        acts = []
        for q in range(n_part):
            issue(q)
            cols = slice(q * part, (q + 1) * part)
            h1 = jnp.dot(xb, w1b[:, cols], preferred_element_type=F32)
            h3 = jnp.dot(xb, w3b[:, cols], preferred_element_type=F32)
            acts.append(((h1 * jax.nn.sigmoid(h1)) * h3).astype(BF16))
        y = None
        for q in range(n_part):
            issue(n_part + q)
            yq = jnp.dot(acts[q], w2b[q * part:(q + 1) * part, :], preferred_element_type=F32)
            y = yq if y is None else y + yq
        ybuf[slot] = y

    @pl.when(i == n_used - 1)
    def _():
        for r in range(rows):
            scatter_row(dst_cur, r, slot).start()
        wait_gather(other)
        wait_scatter(other)
        wait_scatter(slot)


def _experts(blk_expert, n_used, row_src, row_dst_ext, h2, w1, w3, w2, n_plane_rows):
    T, D = h2.shape
    de = w1.shape[-1]
    n_blk = row_src.shape[0] // ROW_BLOCK
    smem_blk = lambda fn: pl.BlockSpec((ROW_BLOCK,), fn, memory_space=pltpu.SMEM)
    grid_spec = pltpu.PrefetchScalarGridSpec(
        num_scalar_prefetch=2,
        grid=(n_blk,),
        in_specs=[smem_blk(lambda i, be, nu: (i,)),
                  smem_blk(lambda i, be, nu: (jnp.minimum(i + 1, n_blk - 1),)),
                  smem_blk(lambda i, be, nu: (i,)),
                  smem_blk(lambda i, be, nu: (i + 1,)),
                  pl.BlockSpec(memory_space=pl.ANY),
                  pl.BlockSpec((None, D, de), lambda i, be, nu: (be[i], 0, 0)),
                  pl.BlockSpec((None, D, de), lambda i, be, nu: (be[i], 0, 0)),
                  pl.BlockSpec((None, de, D), lambda i, be, nu: (be[i], 0, 0))],
        out_specs=pl.BlockSpec(memory_space=pl.ANY),
        scratch_shapes=[pltpu.VMEM((2, ROW_BLOCK, D), F32), pltpu.VMEM((2, ROW_BLOCK, D), F32),
                        pltpu.VMEM((D, de), BF16), pltpu.VMEM((D, de), BF16),
                        pltpu.VMEM((de, D), BF16),
                        pltpu.SemaphoreType.DMA((2,)), pltpu.SemaphoreType.DMA((2,))],
    )
    return pl.pallas_call(
        _experts_kernel,
        grid_spec=grid_spec,
        out_shape=jax.ShapeDtypeStruct((n_plane_rows, D), F32),
        compiler_params=_cparams(("arbitrary",)),
        name="experts",
    )(blk_expert, n_used, row_src, row_src, row_dst_ext, row_dst_ext, h2, w1, w3, w2)


def _finish_kernel(x1_ref, route_ref, gfin_ref, p0_ref, p1_ref, o_ref):
    route = route_ref[...]
    lane = lax.broadcasted_iota(jnp.int32, route.shape, 1)
    g0 = jnp.sum(jnp.where(lane == ROUTE_G0, route, 0.0), axis=1, keepdims=True)
    g1 = jnp.sum(jnp.where(lane == ROUTE_G1, route, 0.0), axis=1, keepdims=True)
    x2 = x1_ref[...] + (g0 * p0_ref[...] + g1 * p1_ref[...])
    ms = jnp.mean(x2 * x2, axis=-1, keepdims=True)
    o_ref[...] = (x2 * lax.rsqrt(ms + EPS)) * gfin_ref[...]


def _finish(x1, route, g_final, planes, *, tb):
    T, D = x1.shape
    nb = T // tb
    return pl.pallas_call(
        _finish_kernel,
        grid=(nb,),
        in_specs=[pl.BlockSpec((tb, D), lambda i: (i, 0)),
                  pl.BlockSpec((tb, LANES), lambda i: (i, 0)),
                  pl.BlockSpec((1, D), lambda i: (0, 0)),
                  pl.BlockSpec((tb, D), lambda i: (i, 0)),
                  pl.BlockSpec((tb, D), lambda i: (nb + i, 0))],
        out_specs=pl.BlockSpec((tb, D), lambda i: (i, 0)),
        out_shape=jax.ShapeDtypeStruct((T, D), F32),
        compiler_params=_cparams(("arbitrary",)),
        name="finish",
    )(x1, route, g_final, planes, planes)


def _rot_half_cols(w, n_heads):
    d = w.shape[0]
    half = HEAD_DIM // 2
    w4 = w.reshape(d, n_heads, 2, half)
    return jnp.concatenate([-w4[:, :, 1:2, :], w4[:, :, 0:1, :]], axis=2).reshape(d, n_heads * HEAD_DIM)


def _pick(n, prefs):
    for p in prefs:
        if n % p == 0:
            return p
    return n


def _layer(x2, pos2, batch, seq, attn_norm, w_in, b_forget, b_gate, sinks, w_proj_fox, w_proj_swa,
           w_out, ffn_norm, w_group, b_group, w_expert, b_expert, w1, w3, w2, final_norm):
    T, D = x2.shape
    scale = HEAD_DIM ** -0.5

    sp = np.cumsum([FOX_W, FOX_W, FOX_W, FOX_HEADS, SWA_W, SWA_KV_W, SWA_KV_W, D, D])[:-1]
    wq_f, wk_f, wv_f, w_fl, wq_s, wk_s, wv_s, wg_f, wg_s = jnp.split(w_in, sp, axis=1)
    head_order = np.asarray(SWA_HEAD_ORDER)
    wq_s = wq_s.reshape(D, SWA_HEADS, HEAD_DIM)[:, head_order, :].reshape(D, SWA_W)
    w_fl_pad = jnp.pad(w_fl, ((0, 0), (0, LANES - FOX_HEADS)))
    w_all = jnp.concatenate(
        [wq_f * (scale * LOG2E), wk_f, wv_f, wq_s * scale, _rot_half_cols(wq_s, SWA_HEADS) * scale,
         wk_s, _rot_half_cols(wk_s, SWA_KV_HEADS), wv_s, wg_f, wg_s, w_fl_pad], axis=1).astype(BF16)
    half = HEAD_DIM // 2
    inv_freq = ROPE_THETA ** (-jnp.arange(half, dtype=F32) * 2.0 / HEAD_DIM)
    invf = jnp.tile(inv_freq, LANES // half).reshape(1, LANES)
    bf_pad = jnp.pad(b_forget.astype(F32), (0, LANES - FOX_HEADS)).reshape(1, LANES)
    sinks_perm = sinks.astype(F32)[head_order]
    wps = w_proj_swa.reshape(SWA_HEADS, HEAD_DIM, D)[head_order].reshape(SWA_W, D).astype(BF16)
    wpf = w_proj_fox.astype(BF16)
    wout = w_out.astype(BF16)
    w_r = jnp.pad(jnp.concatenate([w_group, w_expert], axis=1).astype(F32),
                  ((0, 0), (0, LANES - N_GROUPS - N_EXPERTS)))
    wr_hi = w_r.astype(BF16)
    wr_lo = (w_r - wr_hi.astype(F32)).astype(BF16)
    wr_cat = jnp.concatenate([wr_hi, wr_lo], axis=1)
    b_r = jnp.pad(jnp.concatenate([b_group, b_expert]).astype(F32),
                  (0, LANES - N_GROUPS - N_EXPERTS)).reshape(1, LANES)

    tm1 = _pick(seq, (512, 256, 128))
    qt, kf, vt, qs, ks, vs, g_f, g_s, qxt, kx = _in_proj(
        x2, pos2, attn_norm.reshape(1, D), w_all, invf, bf_pad, b_gate.astype(F32),
        batch=batch, seq=seq, tm=tm1)
    tq = _pick(seq, (FOX_TQ, 128))
    tk = min(tq, FOX_TK)
    o_f = _fox_attn(qt, qxt, kf, kx, vt, batch=batch, seq=seq, tq=tq, tk=tk,
                    unroll=min(FOX_UNROLL, tq // tk))
    o_s = _swa_attn(sinks_perm, qs, ks, vs, batch=batch, seq=seq, tq=_pick(seq, (512, 256, 128)))

    tm3 = _pick(T, (512, 256, 128))
    x1, h2, route, counts = _mix_route(o_f, o_s, g_f, g_s, x2, wpf, wps, wout,
                                       ffn_norm.reshape(1, D), wr_cat, b_r, tm=tm3)

    cnt = counts[0, :N_EXPERTS].astype(jnp.int32)
    padded = (cnt + ROW_BLOCK - 1) // ROW_BLOCK * ROW_BLOCK
    pad_end = jnp.cumsum(padded)
    pad_start = pad_end - padded
    e_ids = route[:, ROUTE_E0:ROUTE_E1 + 1].astype(jnp.int32)
    ranks = route[:, ROUTE_R0:ROUTE_R1 + 1].astype(jnp.int32)
    dest_flat = (pad_start[e_ids] + ranks).reshape(-1)
    P = 2 * T + N_EXPERTS * ROW_BLOCK
    n_blk = P // ROW_BLOCK
    blk_pos = jnp.arange(n_blk, dtype=jnp.int32) * ROW_BLOCK
    blk_expert = jnp.minimum(
        jnp.sum((pad_end[None, :] <= blk_pos[:, None]).astype(jnp.int32), axis=1), N_EXPERTS - 1)
    n_used = (pad_end[-1:] // ROW_BLOCK).astype(jnp.int32)
    assign = jnp.full((P,), -1, jnp.int32).at[dest_flat].set(
        jnp.arange(2 * T, dtype=jnp.int32), unique_indices=True)
    junk = 2 * T + jnp.arange(P, dtype=jnp.int32) % ROW_BLOCK
    row_src = jnp.where(assign >= 0, assign >> 1, 0)
    row_dst = jnp.where(assign >= 0, (assign & 1) * T + (assign >> 1), junk)
    row_dst_ext = jnp.concatenate([junk[:ROW_BLOCK], row_dst])

    planes = _experts(blk_expert, n_used, row_src, row_dst_ext, h2, w1, w3, w2, 2 * T + ROW_BLOCK)
    return _finish(x1, route, final_norm.reshape(1, D), planes, tb=_pick(T, (512, 256, 128)))


def kernel(x, positions, attn_norm, w_in, b_forget, b_gate, attn_sinks, w_proj_fox, w_proj_swa,
           w_out, ffn_norm, w_group, b_group, w_expert, b_expert, w1, w3, w2, final_norm):
    B, S, D = x.shape
    depth = attn_norm.shape[0]
    x2 = x.reshape(B * S, D)
    pos2 = positions.reshape(B * S, 1).astype(jnp.int32)
    assert depth == 1, "a single layer is followed directly by the final norm"
    out = _layer(x2, pos2, B, S, attn_norm[0], w_in[0], b_forget[0], b_gate[0], attn_sinks[0],
                 w_proj_fox[0], w_proj_swa[0], w_out[0], ffn_norm[0], w_group[0], b_group[0],
                 w_expert[0], b_expert[0], w1[0], w3[0], w2[0], final_norm)
    return out.reshape(B, S, D)
```

```python
import functools

import numpy as np
import jax
import jax.numpy as jnp
from jax import lax
from jax.experimental import pallas as pl
from jax.experimental.pallas import tpu as pltpu

HEAD_DIM = 64
FOX_HEADS = 8
SWA_HEADS = 8
SWA_KV_HEADS = 2
WINDOW = 128
ROPE_THETA = 10000.0
N_GROUPS = 8
EXPERTS_PER_GROUP = 8
N_EXPERTS = N_GROUPS * EXPERTS_PER_GROUP
ROW_BLOCK = 256
DMA_UNROLL = 8
EPS = 1e-6
LANES = 128
NEG_BIG = -1e30
LOG2E = 1.4426950408889634
BIAS_TERMS = 3
BIAS_LANES = 2 * BIAS_TERMS
FOX_TQ = 1024
FOX_TK = 256
FOX_UNROLL = 4

FOX_W = FOX_HEADS * HEAD_DIM
SWA_W = SWA_HEADS * HEAD_DIM
SWA_KV_W = SWA_KV_HEADS * HEAD_DIM
SWA_HEAD_ORDER = (0, 4, 1, 5, 2, 6, 3, 7)

F32 = jnp.float32
BF16 = jnp.bfloat16
VMEM_LIMIT = 56 * 1024 * 1024


def _cparams(sem):
    return pltpu.CompilerParams(dimension_semantics=sem, vmem_limit_bytes=VMEM_LIMIT)


def _split3(x):
    hi = x.astype(BF16)
    r1 = x - hi.astype(F32)
    mid = r1.astype(BF16)
    lo = (r1 - mid.astype(F32)).astype(BF16)
    return hi, mid, lo


W_NAMES = ("q_fox", "k_fox", "v_fox", "q_swa", "q_swa_rot", "k_swa", "k_swa_rot", "v_swa",
           "gate_fox", "gate_swa", "forget")


def _in_proj_kernel(x_ref, pos_ref, g_ref, invf_ref, bf_ref, bg_ref, place_ref, ones_ref, *refs):
    w = dict(zip(W_NAMES, refs[:len(W_NAMES)]))
    (qf_ref, kf_ref, vf_ref, qs_ref, ks_ref, vs_ref, gf_ref, gs_ref, qx_ref, kx_ref,
     carry_ref) = refs[len(W_NAMES):]
    j = pl.program_id(1)
    tm = x_ref.shape[0]

    @pl.when(j == 0)
    def _():
        carry_ref[...] = jnp.zeros_like(carry_ref)

    x = x_ref[...]
    ms = jnp.mean(x * x, axis=-1, keepdims=True)
    h = ((x * lax.rsqrt(ms + EPS)) * g_ref[...]).astype(BF16)

    def mm(name):
        return jnp.dot(h, w[name][...], preferred_element_type=F32)

    def store_pairs_t(ref, y):
        for pr in range(ref.shape[0]):
            ref[pr] = y[:, pr * LANES:(pr + 1) * LANES].T.astype(BF16)

    store_pairs_t(qf_ref, mm("q_fox"))
    kf_ref[...] = mm("k_fox").astype(BF16)
    store_pairs_t(vf_ref, mm("v_fox"))

    ang = pos_ref[...].astype(F32) * invf_ref[...]
    cos = jnp.cos(ang)
    sin = jnp.sin(ang)
    n_rep = SWA_W // LANES
    cos_q = jnp.concatenate([cos] * n_rep, axis=1)
    sin_q = jnp.concatenate([sin] * n_rep, axis=1)
    qs_ref[...] = (mm("q_swa") * cos_q + mm("q_swa_rot") * sin_q).astype(BF16)
    ks_ref[...] = (mm("k_swa") * cos + mm("k_swa_rot") * sin).astype(BF16)
    vs_ref[...] = mm("v_swa").astype(BF16)

    gf_ref[...] = jax.nn.sigmoid(mm("gate_fox") + bg_ref[0:1, :]).astype(BF16)
    gs_ref[...] = jax.nn.sigmoid(mm("gate_swa") + bg_ref[1:2, :]).astype(BF16)

    z = mm("forget") + bf_ref[...]
    lane = lax.broadcasted_iota(jnp.int32, (tm, LANES), 1)
    lf = jnp.minimum(z, 0.0) - jnp.log1p(jnp.exp(-jnp.abs(z)))
    lf = jnp.where(lane < FOX_HEADS, lf, 0.0)
    row = lax.broadcasted_iota(jnp.int32, (tm, tm), 0)
    col = lax.broadcasted_iota(jnp.int32, (tm, tm), 1)
    tri = jnp.where(row >= col, 1.0, 0.0).astype(BF16)
    hi, mid, lo = _split3(lf)
    c = (jnp.dot(tri, hi, preferred_element_type=F32)
         + jnp.dot(tri, mid, preferred_element_type=F32)
         + jnp.dot(tri, lo, preferred_element_type=F32)) + carry_ref[0:1, :]
    carry_ref[0:1, :] = c[tm - 1:tm, :]
    chi, cmid, clo = _split3(c * LOG2E)
    ext = jnp.dot(jnp.concatenate([chi, cmid, clo], axis=1), place_ref[...],
                  preferred_element_type=F32) + ones_ref[...]
    qx_ref[...] = ext[:, :LANES].T.astype(BF16)
    kx_ref[...] = ext[:, LANES:].astype(BF16)


def _bias_placement():
    place = np.zeros((BIAS_TERMS * LANES, 2 * LANES), np.float32)
    ones = np.zeros((1, 2 * LANES), np.float32)
    for h in range(FOX_HEADS):
        for t in range(BIAS_TERMS):
            place[t * LANES + h, BIAS_LANES * h + t] = 1.0
            place[t * LANES + h, LANES + BIAS_LANES * h + BIAS_TERMS + t] = -1.0
            ones[0, BIAS_LANES * h + BIAS_TERMS + t] = 1.0
            ones[0, LANES + BIAS_LANES * h + t] = 1.0
    return jnp.asarray(place, BF16), jnp.asarray(ones, F32)


def _in_proj(x2, pos2, g_attn, weights, invf, bf_pad, b_gate, *, batch, seq, tm):
    T, D = x2.shape
    nj = seq // tm
    w_list = [weights[name] for name in W_NAMES]
    place, ones = _bias_placement()
    row_blk = lambda w: pl.BlockSpec((tm, w), lambda b, j: (b * nj + j, 0))
    const = lambda shape: pl.BlockSpec(shape, lambda b, j: (0, 0))
    n_pair = FOX_HEADS // 2
    pair_t = pl.BlockSpec((None, n_pair, LANES, tm), lambda b, j: (b, 0, 0, j))
    out_shapes = (
        jax.ShapeDtypeStruct((batch, n_pair, LANES, seq), BF16), jax.ShapeDtypeStruct((T, FOX_W), BF16),
        jax.ShapeDtypeStruct((batch, n_pair, LANES, seq), BF16), jax.ShapeDtypeStruct((T, SWA_W), BF16),
        jax.ShapeDtypeStruct((T, SWA_KV_W), BF16), jax.ShapeDtypeStruct((T, SWA_KV_W), BF16),
        jax.ShapeDtypeStruct((T, D), BF16), jax.ShapeDtypeStruct((T, D), BF16),
        jax.ShapeDtypeStruct((batch, LANES, seq), BF16), jax.ShapeDtypeStruct((T, LANES), BF16),
    )
    return pl.pallas_call(
        _in_proj_kernel,
        grid=(batch, nj),
        in_specs=[row_blk(D), row_blk(1), const((1, D)), const((1, LANES)),
                  const((1, LANES)), const((2, D)), const(place.shape), const(ones.shape)]
                 + [const(wt.shape) for wt in w_list],
        out_specs=(pair_t, row_blk(FOX_W), pair_t, row_blk(SWA_W),
                   row_blk(SWA_KV_W), row_blk(SWA_KV_W), row_blk(D), row_blk(D),
                   pl.BlockSpec((None, LANES, tm), lambda b, j: (b, 0, j)), row_blk(LANES)),
        out_shape=out_shapes,
        scratch_shapes=[pltpu.VMEM((8, LANES), F32)],
        compiler_params=_cparams(("arbitrary", "arbitrary")),
        name="in_proj",
    )(x2, pos2, g_attn, invf, bf_pad, b_gate, place, ones, *w_list)


def _fox_kernel(q_ref, qx_ref, k_ref, kx_ref, vt_ref, o_ref, acc_ref, *, tk, unroll):
    p = pl.program_id(1)
    i = pl.program_id(2)
    tq = q_ref.shape[1]
    n_diag = tq // tk
    qt = q_ref[...]
    qxt = qx_ref[...]
    row = lax.broadcasted_iota(jnp.int32, (LANES, tq), 0)
    zero = jnp.zeros_like(qt)
    q_ext = []
    for hh in range(2):
        first = BIAS_LANES * (2 * p + hh)
        head_rows = (row < HEAD_DIM) if hh == 0 else (row >= HEAD_DIM)
        bias_rows = (row >= first) & (row < first + BIAS_LANES)
        q_ext.append(jnp.concatenate([jnp.where(head_rows, qt, zero),
                                      jnp.where(bias_rows, qxt, zero)], axis=0))

    acc_ref[...] = jnp.zeros_like(acc_ref)

    def scores(ks, c0, masked):
        k_ext = jnp.concatenate([k_ref[pl.ds(ks, tk), :], kx_ref[pl.ds(ks, tk), :]], axis=1)
        out = []
        for hh in range(2):
            s = jnp.dot(k_ext, q_ext[hh][:, c0:], preferred_element_type=F32)
            if masked:
                key = lax.broadcasted_iota(jnp.int32, s.shape, 0)
                qry = lax.broadcasted_iota(jnp.int32, s.shape, 1)
                s = jnp.where(key <= qry, s, -jnp.inf)
            out.append(s)
        return out

    def accumulate(ks, c0, s_pair, stats):
        vt = vt_ref[:, pl.ds(ks, tk)]
        out = []
        for hh in range(2):
            m_all, l_all = stats[hh]
            m_old, l_old = m_all[:, c0:], l_all[:, c0:]
            s = s_pair[hh]
            m_new = jnp.maximum(m_old, jnp.max(s, axis=0, keepdims=True))
            alpha = jnp.exp2(m_old - m_new)
            pm = jnp.exp2(s - m_new)
            l_new = alpha * l_old + jnp.sum(pm, axis=0, keepdims=True)
            acc_ref[hh, :, c0:] = alpha * acc_ref[hh, :, c0:] + jnp.dot(
                vt, pm.astype(BF16), preferred_element_type=F32)
            if c0:
                m_new = jnp.concatenate([m_all[:, :c0], m_new], axis=1)
                l_new = jnp.concatenate([l_all[:, :c0], l_new], axis=1)
            out.append((m_new, l_new))
        return tuple(out)

    def run_tiles(first_tile, count, stats, diagonal):
        start = lambda u: pl.multiple_of((first_tile + u) * tk, tk)
        col0 = lambda u: u * tk if diagonal else 0
        s_next = scores(start(0), col0(0), diagonal)
        for u in range(count):
            s_cur = s_next
            if u + 1 < count:
                s_next = scores(start(u + 1), col0(u + 1), diagonal)
            stats = accumulate(start(u), col0(u), s_cur, stats)
        return stats

    init = tuple((jnp.full((1, tq), NEG_BIG, F32), jnp.zeros((1, tq), F32)) for _ in range(2))
    stats = lax.fori_loop(0, i * (n_diag // unroll),
                          lambda jb, st: run_tiles(jb * unroll, unroll, st, False), init)
    stats = run_tiles(i * n_diag, n_diag, stats, True)

    o_t = jnp.where(row < HEAD_DIM, acc_ref[0] * (1.0 / stats[0][1]),
                    acc_ref[1] * (1.0 / stats[1][1]))
    o_ref[...] = o_t.T.astype(BF16)


def _fox_attn(qt, qxt, kf, kx, vt, *, batch, seq, tq, tk, unroll):
    T = kf.shape[0]
    nq = seq // tq
    n_pair = FOX_HEADS // 2
    return pl.pallas_call(
        functools.partial(_fox_kernel, tk=tk, unroll=unroll),
        grid=(batch, n_pair, nq),
        in_specs=[
            pl.BlockSpec((None, None, LANES, tq), lambda b, p, i: (b, p, 0, i)),
            pl.BlockSpec((None, LANES, tq), lambda b, p, i: (b, 0, i)),
            pl.BlockSpec((seq, LANES), lambda b, p, i: (b, p)),
            pl.BlockSpec((seq, LANES), lambda b, p, i: (b, 0)),
            pl.BlockSpec((None, None, LANES, seq), lambda b, p, i: (b, p, 0, 0)),
        ],
        out_specs=pl.BlockSpec((tq, LANES), lambda b, p, i: (b * nq + i, p)),
        out_shape=jax.ShapeDtypeStruct((T, FOX_W), BF16),
        scratch_shapes=[pltpu.VMEM((2, LANES, tq), F32)],
        compiler_params=_cparams(("arbitrary", "arbitrary", "arbitrary")),
        name="fox_attn",
    )(qt, qxt, kf, kx, vt)


def _swa_kernel(sink_ref, q_ref, kc_ref, kp_ref, vc_ref, vp_ref, o_ref):
    i = pl.program_id(1)
    tq = q_ref.shape[0]
    n_win = tq // WINDOW
    n_col = SWA_W // LANES
    lane = lax.broadcasted_iota(jnp.int32, (WINDOW, LANES), 1)
    low = lane < HEAD_DIM
    r = lax.broadcasted_iota(jnp.int32, (WINDOW, 2 * WINDOW), 0)
    c = lax.broadcasted_iota(jnp.int32, (WINDOW, 2 * WINDOW), 1)
    dist = WINDOW + r - c
    band = (dist >= 0) & (dist < WINDOW)
    for w in range(n_win):
        if w == 0:
            k_win = jnp.concatenate([kp_ref[...], kc_ref[0:WINDOW, :]], axis=0)
            v_win = jnp.concatenate([vp_ref[...], vc_ref[0:WINDOW, :]], axis=0)
            first = i == 0
            valid = band & ((c >= WINDOW) | jnp.logical_not(first))
        else:
            k_win = kc_ref[(w - 1) * WINDOW:(w + 1) * WINDOW, :]
            v_win = vc_ref[(w - 1) * WINDOW:(w + 1) * WINDOW, :]
            valid = band
        for col_i in range(n_col):
            qc = q_ref[w * WINDOW:(w + 1) * WINDOW, col_i * LANES:(col_i + 1) * LANES]
            outs = []
            for half in range(2):
                qm = jnp.where(low if half == 0 else jnp.logical_not(low), qc, jnp.zeros_like(qc))
                s = lax.dot_general(qm, k_win, (((1,), (1,)), ((), ())),
                                    preferred_element_type=F32)
                s = jnp.where(valid, s, -jnp.inf)
                sink = sink_ref[2 * col_i + half]
                m = jnp.maximum(jnp.max(s, axis=1, keepdims=True), sink)
                pm = jnp.exp(s - m)
                denom = jnp.sum(pm, axis=1, keepdims=True) + jnp.exp(sink - m)
                pm = pm / denom
                outs.append(jnp.dot(pm.astype(BF16), v_win, preferred_element_type=F32))
            o_ref[w * WINDOW:(w + 1) * WINDOW, col_i * LANES:(col_i + 1) * LANES] = (
                jnp.where(low, outs[0], outs[1]).astype(BF16))


def _swa_attn(sinks_perm, qs, ks, vs, *, batch, seq, tq):
    T = qs.shape[0]
    nq = seq // tq
    n_win = tq // WINDOW
    nb = seq // WINDOW
    cur = lambda w: pl.BlockSpec((tq, w), lambda b, i: (b * nq + i, 0))
    prev = pl.BlockSpec((WINDOW, SWA_KV_W),
                        lambda b, i: (jnp.maximum(b * nb + i * n_win - 1, 0), 0))
    return pl.pallas_call(
        _swa_kernel,
        grid=(batch, nq),
        in_specs=[pl.BlockSpec(memory_space=pltpu.SMEM), cur(SWA_W), cur(SWA_KV_W), prev,
                  cur(SWA_KV_W), prev],
        out_specs=cur(SWA_W),
        out_shape=jax.ShapeDtypeStruct((T, SWA_W), BF16),
        compiler_params=_cparams(("arbitrary", "arbitrary")),
        name="swa_attn",
    )(sinks_perm, qs, ks, ks, vs, vs)


ROUTE_E0, ROUTE_E1, ROUTE_G0, ROUTE_G1, ROUTE_R0, ROUTE_R1 = range(6)


def _mix_route_kernel(of_ref, os_ref, gf_ref, gs_ref, x_ref, wpf_ref, wps_ref, wout_ref, gn_ref,
                      wr_ref, br_ref, x1_ref, h2_ref, route_ref, cnt_ref, carry_ref):
    step = pl.program_id(0)
    tm = x_ref.shape[0]

    @pl.when(step == 0)
    def _():
        carry_ref[...] = jnp.zeros_like(carry_ref)

    a = jnp.dot(of_ref[...], wpf_ref[...], preferred_element_type=F32)
    b = jnp.dot(os_ref[...], wps_ref[...], preferred_element_type=F32)
    merged = gf_ref[...].astype(F32) * a + gs_ref[...].astype(F32) * b
    y = jnp.dot(merged.astype(BF16), wout_ref[...], preferred_element_type=F32)
    x1 = x_ref[...] + y
    x1_ref[...] = x1
    ms = jnp.mean(x1 * x1, axis=-1, keepdims=True)
    h2 = (x1 * lax.rsqrt(ms + EPS)) * gn_ref[...]
    h2_ref[...] = h2

    h_hi = h2.astype(BF16)
    h_lo = (h2 - h_hi.astype(F32)).astype(BF16)
    wr = wr_ref[...]
    r_hi = jnp.dot(h_hi, wr, preferred_element_type=F32)
    logits = (r_hi[:, :LANES] + r_hi[:, LANES:]
              + jnp.dot(h_lo, wr[:, :LANES], preferred_element_type=F32)) + br_ref[...]

    lane = lax.broadcasted_iota(jnp.int32, (tm, LANES), 1)
    is_g = lane < N_GROUPS
    gl = jnp.where(is_g, logits, -jnp.inf)
    gmax = jnp.max(gl, axis=1, keepdims=True)
    gexp = jnp.where(is_g, jnp.exp(gl - gmax), 0.0)
    gprob = gexp / jnp.sum(gexp, axis=1, keepdims=True)
    g_w = jnp.max(gprob, axis=1, keepdims=True)
    g_idx = jnp.min(jnp.where(is_g & (gprob == g_w), lane, LANES), axis=1, keepdims=True)

    e_lane = lane - N_GROUPS
    in_grp = (e_lane >= 0) & (e_lane < N_EXPERTS) & ((e_lane >> 3) == g_idx)
    el = jnp.where(in_grp, logits, -jnp.inf)
    v0 = jnp.max(el, axis=1, keepdims=True)
    i0 = jnp.min(jnp.where(in_grp & (el == v0), lane, LANES), axis=1, keepdims=True)
    el1 = jnp.where(lane == i0, -jnp.inf, el)
    v1 = jnp.max(el1, axis=1, keepdims=True)
    i1 = jnp.min(jnp.where(in_grp & (lane != i0) & (el1 == v1), lane, LANES), axis=1, keepdims=True)
    t = jnp.exp(v1 - v0)
    den = 1.0 + t
    gate0 = g_w * (1.0 / den)
    gate1 = g_w * (t / den)
    e0 = i0 - N_GROUPS
    e1 = i1 - N_GROUPS

    oh0 = jnp.where(lane == e0, 1.0, 0.0)
    oh1 = jnp.where(lane == e1, 1.0, 0.0)
    row = lax.broadcasted_iota(jnp.int32, (tm, tm), 0)
    col = lax.broadcasted_iota(jnp.int32, (tm, tm), 1)
    tri = jnp.where(row > col, 1.0, 0.0).astype(BF16)
    cs0 = jnp.sum(oh0, axis=0, keepdims=True)
    cs1 = jnp.sum(oh1, axis=0, keepdims=True)
    carry = carry_ref[0:1, :]
    pre = jnp.dot(tri, jnp.concatenate([oh0, oh1], axis=1).astype(BF16),
                  preferred_element_type=F32)
    pre0 = pre[:, :LANES] + carry
    pre1 = pre[:, LANES:] + (carry + cs0)
    rank0 = jnp.sum(oh0 * pre0, axis=1, keepdims=True)
    rank1 = jnp.sum(oh1 * pre1, axis=1, keepdims=True)
    new_carry = carry + cs0 + cs1
    carry_ref[0:1, :] = new_carry
    cnt_ref[...] = jnp.broadcast_to(new_carry, cnt_ref.shape)

    route = jnp.where(lane == ROUTE_E0, e0.astype(F32), 0.0)
    route = jnp.where(lane == ROUTE_E1, e1.astype(F32), route)
    route = jnp.where(lane == ROUTE_G0, gate0, route)
    route = jnp.where(lane == ROUTE_G1, gate1, route)
    route = jnp.where(lane == ROUTE_R0, rank0, route)
    route = jnp.where(lane == ROUTE_R1, rank1, route)
    route_ref[...] = route


def _mix_route(o_f, o_s, g_f, g_s, x2, wpf, wps, wout, g_ffn, wr_cat, b_r, *, tm):
    T, D = x2.shape
    row_blk = lambda w: pl.BlockSpec((tm, w), lambda i: (i, 0))
    const = lambda shape: pl.BlockSpec(shape, lambda i: (0, 0))
    return pl.pallas_call(
        _mix_route_kernel,
        grid=(T // tm,),
        in_specs=[row_blk(FOX_W), row_blk(SWA_W), row_blk(D), row_blk(D), row_blk(D),
                  const((FOX_W, D)), const((SWA_W, D)), const((D, D)), const((1, D)),
                  const((D, 2 * LANES)), const((1, LANES))],
        out_specs=(row_blk(D), row_blk(D), row_blk(LANES), const((8, LANES))),
        out_shape=(jax.ShapeDtypeStruct((T, D), F32), jax.ShapeDtypeStruct((T, D), F32),
                   jax.ShapeDtypeStruct((T, LANES), F32), jax.ShapeDtypeStruct((8, LANES), F32)),
        scratch_shapes=[pltpu.VMEM((8, LANES), F32)],
        compiler_params=_cparams(("arbitrary",)),
        name="mix_route",
    )(o_f, o_s, g_f, g_s, x2, wpf, wps, wout, g_ffn, wr_cat, b_r)


def _dispatch_kernel(bounds_ref, dest_ref, h_ref, xs_ref, zbuf, sem, zsem):
    step = pl.program_id(0)
    tb = h_ref.shape[0]

    @pl.when(step == 0)
    def _():
        zbuf[...] = jnp.zeros_like(zbuf)

        def tail_copy(e):
            tail = pl.multiple_of(bounds_ref[e + 1] - ROW_BLOCK, ROW_BLOCK)
            return pltpu.make_async_copy(zbuf, xs_ref.at[pl.ds(tail, ROW_BLOCK), :], zsem)

        def for_nonempty(action):
            def body(e, carry):
                @pl.when(bounds_ref[e + 1] > bounds_ref[e])
                def _():
                    action(tail_copy(e))
                return carry
            lax.fori_loop(0, N_EXPERTS, body, 0)

        n_rows = xs_ref.shape[0]
        first_unused = bounds_ref[N_EXPERTS] // ROW_BLOCK

        def unused_copy(blk):
            return pltpu.make_async_copy(
                zbuf, xs_ref.at[pl.ds(pl.multiple_of(blk * ROW_BLOCK, ROW_BLOCK), ROW_BLOCK), :], zsem)

        def for_unused(action):
            def body(blk, carry):
                action(unused_copy(blk))
                return carry
            lax.fori_loop(first_unused, n_rows // ROW_BLOCK, body, 0)

        for_nonempty(lambda cp: cp.start())
        for_unused(lambda cp: cp.start())
        for_nonempty(lambda cp: cp.wait())
        for_unused(lambda cp: cp.wait())

    def start(t, carry):
        for k in range(2):
            d = dest_ref[2 * t + k]
            pltpu.make_async_copy(h_ref.at[pl.ds(t, 1), :], xs_ref.at[pl.ds(d, 1), :],
                                  sem).start(priority=k)
        return carry

    lax.fori_loop(0, tb, start, 0, unroll=DMA_UNROLL)
    for _ in range(2):
        pltpu.make_async_copy(h_ref, xs_ref.at[pl.ds(0, tb), :], sem).wait()


def _dispatch(bounds, dest_flat, h2, n_rows, *, tb):
    T, D = h2.shape
    grid_spec = pltpu.PrefetchScalarGridSpec(
        num_scalar_prefetch=1,
        grid=(T // tb,),
        in_specs=[pl.BlockSpec((2 * tb,), lambda i, bd: (i,), memory_space=pltpu.SMEM),
                  pl.BlockSpec((tb, D), lambda i, bd: (i, 0))],
        out_specs=pl.BlockSpec(memory_space=pl.ANY),
        scratch_shapes=[pltpu.VMEM((ROW_BLOCK, D), F32), pltpu.SemaphoreType.DMA(()),
                        pltpu.SemaphoreType.DMA(())],
    )
    return pl.pallas_call(
        _dispatch_kernel,
        grid_spec=grid_spec,
        out_shape=jax.ShapeDtypeStruct((n_rows, D), F32),
        compiler_params=_cparams(("arbitrary",)),
        name="dispatch",
    )(bounds, dest_flat, h2)


def _experts_kernel(be_ref, nu_ref, xs_ref, w1_ref, w3_ref, w2_ref, ys_ref, w1b, w3b, w2b):
    i = pl.program_id(0)
    e = be_ref[i]
    e_prev = be_ref[jnp.maximum(i - 1, 0)]
    used = i < nu_ref[0]

    @pl.when(used & ((i == 0) | (e != e_prev)))
    def _():
        w1b[...] = w1_ref[...].astype(BF16)
        w3b[...] = w3_ref[...].astype(BF16)
        w2b[...] = w2_ref[...].astype(BF16)

    @pl.when(used)
    def _():
        xb = xs_ref[...].astype(BF16)
        de = w1b.shape[1]
        n_part = 2
        part = de // n_part
        acts = []
        for q in range(n_part):
            cols = slice(q * part, (q + 1) * part)
            h1 = jnp.dot(xb, w1b[:, cols], preferred_element_type=F32)
            h3 = jnp.dot(xb, w3b[:, cols], preferred_element_type=F32)
            acts.append(((h1 * jax.nn.sigmoid(h1)) * h3).astype(BF16))
        y = jnp.dot(acts[0], w2b[0:part, :], preferred_element_type=F32)
        for q in range(1, n_part):
            y = y + jnp.dot(acts[q], w2b[q * part:(q + 1) * part, :], preferred_element_type=F32)
        ys_ref[...] = y

    @pl.when(jnp.logical_not(used))
    def _():
        ys_ref[...] = jnp.zeros_like(ys_ref)


def _experts(blk_expert, n_used, xs, w1, w3, w2):
    P, D = xs.shape
    de = w1.shape[-1]
    n_blk = P // ROW_BLOCK
    grid_spec = pltpu.PrefetchScalarGridSpec(
        num_scalar_prefetch=2,
        grid=(n_blk,),
        in_specs=[pl.BlockSpec((ROW_BLOCK, D),
                               lambda i, be, nu: (jnp.minimum(i, jnp.maximum(nu[0] - 1, 0)), 0)),
                  pl.BlockSpec((None, D, de), lambda i, be, nu: (be[i], 0, 0)),
                  pl.BlockSpec((None, D, de), lambda i, be, nu: (be[i], 0, 0)),
                  pl.BlockSpec((None, de, D), lambda i, be, nu: (be[i], 0, 0))],
        out_specs=pl.BlockSpec((ROW_BLOCK, D), lambda i, be, nu: (i, 0)),
        scratch_shapes=[pltpu.VMEM((D, de), BF16), pltpu.VMEM((D, de), BF16),
                        pltpu.VMEM((de, D), BF16)],
    )
    return pl.pallas_call(
        _experts_kernel,
        grid_spec=grid_spec,
        out_shape=jax.ShapeDtypeStruct((P, D), F32),
        compiler_params=_cparams(("arbitrary",)),
        name="experts",
    )(blk_expert, n_used, xs, w1, w3, w2)


def _combine_kernel(dest_ref, x1_ref, route_ref, gfin_ref, ys_ref, o_ref, ybuf, sem):
    tb = x1_ref.shape[0]

    def start(t, carry):
        for k in range(2):
            d = dest_ref[2 * t + k]
            pltpu.make_async_copy(ys_ref.at[pl.ds(d, 1), :], ybuf.at[k, pl.ds(t, 1), :],
                                  sem).start(priority=k)
        return carry

    lax.fori_loop(0, tb, start, 0, unroll=DMA_UNROLL)
    for k in range(2):
        pltpu.make_async_copy(ys_ref.at[pl.ds(0, tb), :], ybuf.at[k], sem).wait()

    route = route_ref[...]
    lane = lax.broadcasted_iota(jnp.int32, route.shape, 1)
    g0 = jnp.sum(jnp.where(lane == ROUTE_G0, route, 0.0), axis=1, keepdims=True)
    g1 = jnp.sum(jnp.where(lane == ROUTE_G1, route, 0.0), axis=1, keepdims=True)
    moe = g0 * ybuf[0] + g1 * ybuf[1]
    x2 = x1_ref[...] + moe
    ms = jnp.mean(x2 * x2, axis=-1, keepdims=True)
    o_ref[...] = (x2 * lax.rsqrt(ms + EPS)) * gfin_ref[...]


def _combine(dest_flat, x1, route, g_final, ys, *, tb):
    T, D = x1.shape
    return pl.pallas_call(
        _combine_kernel,
        grid=(T // tb,),
        in_specs=[pl.BlockSpec((2 * tb,), lambda i: (i,), memory_space=pltpu.SMEM),
                  pl.BlockSpec((tb, D), lambda i: (i, 0)),
                  pl.BlockSpec((tb, LANES), lambda i: (i, 0)),
                  pl.BlockSpec((1, D), lambda i: (0, 0)),
                  pl.BlockSpec(memory_space=pl.ANY)],
        out_specs=pl.BlockSpec((tb, D), lambda i: (i, 0)),
        out_shape=jax.ShapeDtypeStruct((T, D), F32),
        scratch_shapes=[pltpu.VMEM((2, tb, D), F32), pltpu.SemaphoreType.DMA(())],
        compiler_params=_cparams(("arbitrary",)),
        name="combine",
    )(dest_flat, x1, route, g_final, ys)


def _rot_half_cols(w, n_heads):
    d = w.shape[0]
    half = HEAD_DIM // 2
    w4 = w.reshape(d, n_heads, 2, half)
    return jnp.concatenate([-w4[:, :, 1:2, :], w4[:, :, 0:1, :]], axis=2).reshape(d, n_heads * HEAD_DIM)


def _pick(n, prefs):
    for p in prefs:
        if n % p == 0:
            return p
    return n


def _layer(x2, pos2, batch, seq, attn_norm, w_in, b_forget, b_gate, sinks, w_proj_fox, w_proj_swa,
           w_out, ffn_norm, w_group, b_group, w_expert, b_expert, w1, w3, w2, final_norm):
    T, D = x2.shape
    scale = HEAD_DIM ** -0.5

    sp = np.cumsum([FOX_W, FOX_W, FOX_W, FOX_HEADS, SWA_W, SWA_KV_W, SWA_KV_W, D, D])[:-1]
    wq_f, wk_f, wv_f, w_fl, wq_s, wk_s, wv_s, wg_f, wg_s = jnp.split(w_in, sp, axis=1)
    head_order = np.asarray(SWA_HEAD_ORDER)
    wq_s = wq_s.reshape(D, SWA_HEADS, HEAD_DIM)[:, head_order, :].reshape(D, SWA_W)
    pieces = (wq_f * (scale * LOG2E), wk_f, wv_f, wq_s * scale,
              _rot_half_cols(wq_s, SWA_HEADS) * scale, wk_s, _rot_half_cols(wk_s, SWA_KV_HEADS), wv_s,
              wg_f, wg_s, jnp.pad(w_fl, ((0, 0), (0, LANES - FOX_HEADS))))
    weights = {name: piece.astype(BF16) for name, piece in zip(W_NAMES, pieces)}
    half = HEAD_DIM // 2
    inv_freq = ROPE_THETA ** (-jnp.arange(half, dtype=F32) * 2.0 / HEAD_DIM)
    invf = jnp.tile(inv_freq, LANES // half).reshape(1, LANES)
    bf_pad = jnp.pad(b_forget.astype(F32), (0, LANES - FOX_HEADS)).reshape(1, LANES)
    sinks_perm = sinks.astype(F32)[head_order]
    wps = w_proj_swa.reshape(SWA_HEADS, HEAD_DIM, D)[head_order].reshape(SWA_W, D).astype(BF16)
    wpf = w_proj_fox.astype(BF16)
    wout = w_out.astype(BF16)
    w_r = jnp.pad(jnp.concatenate([w_group, w_expert], axis=1).astype(F32),
                  ((0, 0), (0, LANES - N_GROUPS - N_EXPERTS)))
    wr_hi = w_r.astype(BF16)
    wr_lo = (w_r - wr_hi.astype(F32)).astype(BF16)
    wr_cat = jnp.concatenate([wr_hi, wr_lo], axis=1)
    b_r = jnp.pad(jnp.concatenate([b_group, b_expert]).astype(F32),
                  (0, LANES - N_GROUPS - N_EXPERTS)).reshape(1, LANES)

    tm1 = _pick(seq, (512, 256, 128))
    qt, kf, vt, qs, ks, vs, g_f, g_s, qxt, kx = _in_proj(
        x2, pos2, attn_norm.reshape(1, D), weights, invf, bf_pad, b_gate.astype(F32),
        batch=batch, seq=seq, tm=tm1)
    tq = _pick(seq, (FOX_TQ, 128))
    tk = min(tq, FOX_TK)
    o_f = _fox_attn(qt, qxt, kf, kx, vt, batch=batch, seq=seq, tq=tq, tk=tk,
                    unroll=min(FOX_UNROLL, tq // tk))
    o_s = _swa_attn(sinks_perm, qs, ks, vs, batch=batch, seq=seq, tq=_pick(seq, (512, 256, 128)))

    tm3 = _pick(T, (512, 256, 128))
    x1, h2, route, counts = _mix_route(o_f, o_s, g_f, g_s, x2, wpf, wps, wout,
                                       ffn_norm.reshape(1, D), wr_cat, b_r, tm=tm3)

    cnt = counts[0, :N_EXPERTS].astype(jnp.int32)
    padded = (cnt + ROW_BLOCK - 1) // ROW_BLOCK * ROW_BLOCK
    pad_end = jnp.cumsum(padded)
    pad_start = pad_end - padded
    bounds = jnp.concatenate([jnp.zeros((1,), jnp.int32), pad_end]).astype(jnp.int32)
    e_ids = route[:, ROUTE_E0:ROUTE_E1 + 1].astype(jnp.int32)
    ranks = route[:, ROUTE_R0:ROUTE_R1 + 1].astype(jnp.int32)
    dest_flat = (pad_start[e_ids] + ranks).reshape(-1)
    P = 2 * T + N_EXPERTS * ROW_BLOCK
    n_blk = P // ROW_BLOCK
    blk_pos = jnp.arange(n_blk, dtype=jnp.int32) * ROW_BLOCK
    blk_expert = jnp.minimum(
        jnp.sum((pad_end[None, :] <= blk_pos[:, None]).astype(jnp.int32), axis=1), N_EXPERTS - 1)
    n_used = (pad_end[-1:] // ROW_BLOCK).astype(jnp.int32)

    tb = _pick(T, (256, 128))
    xs = _dispatch(bounds, dest_flat, h2, P, tb=tb)
    ys = _experts(blk_expert, n_used, xs, w1, w3, w2)
    return _combine(dest_flat, x1, route, final_norm.reshape(1, D), ys, tb=tb)


def kernel(x, positions, attn_norm, w_in, b_forget, b_gate, attn_sinks, w_proj_fox, w_proj_swa,
           w_out, ffn_norm, w_group, b_group, w_expert, b_expert, w1, w3, w2, final_norm):
    B, S, D = x.shape
    depth = attn_norm.shape[0]
    x2 = x.reshape(B * S, D)
    pos2 = positions.reshape(B * S, 1).astype(jnp.int32)
    assert depth == 1, "a single layer is followed directly by the final norm"
    out = _layer(x2, pos2, B, S, attn_norm[0], w_in[0], b_forget[0], b_gate[0], attn_sinks[0],
                 w_proj_fox[0], w_proj_swa[0], w_out[0], ffn_norm[0], w_group[0], b_group[0],
                 w_expert[0], b_expert[0], w1[0], w3[0], w2[0], final_norm)
    return out.reshape(B, S, D)
```

```python
import functools

import numpy as np
import jax
import jax.numpy as jnp
from jax import lax
from jax.experimental import pallas as pl
from jax.experimental.pallas import tpu as pltpu

HEAD_DIM = 64
FOX_HEADS = 8
SWA_HEADS = 8
SWA_KV_HEADS = 2
WINDOW = 128
ROPE_THETA = 10000.0
N_GROUPS = 8
EXPERTS_PER_GROUP = 8
N_EXPERTS = N_GROUPS * EXPERTS_PER_GROUP
ROW_BLOCK = 256
DMA_UNROLL = 8
EPS = 1e-6
LANES = 128
NEG_BIG = -1e30
LOG2E = 1.4426950408889634
BIAS_TERMS = 3
BIAS_LANES = 2 * BIAS_TERMS
FOX_TQ = 1024
FOX_TK = 256
FOX_UNROLL = 4

FOX_W = FOX_HEADS * HEAD_DIM
SWA_W = SWA_HEADS * HEAD_DIM
SWA_KV_W = SWA_KV_HEADS * HEAD_DIM
SWA_HEAD_ORDER = (0, 4, 1, 5, 2, 6, 3, 7)

F32 = jnp.float32
BF16 = jnp.bfloat16
VMEM_LIMIT = 56 * 1024 * 1024


def _cparams(sem):
    return pltpu.CompilerParams(dimension_semantics=sem, vmem_limit_bytes=VMEM_LIMIT)


def _split3(x):
    hi = x.astype(BF16)
    r1 = x - hi.astype(F32)
    mid = r1.astype(BF16)
    lo = (r1 - mid.astype(F32)).astype(BF16)
    return hi, mid, lo


W_NAMES = ("q_fox", "k_fox", "v_fox", "q_swa", "k_swa", "v_swa", "gate_fox", "gate_swa", "forget")


def _in_proj_kernel(x_ref, pos_ref, g_ref, invf_ref, bf_ref, bg_ref, place_ref, ones_ref, *refs):
    w = dict(zip(W_NAMES, refs[:len(W_NAMES)]))
    (qf_ref, kf_ref, vf_ref, qs_ref, ks_ref, vs_ref, gf_ref, gs_ref, qx_ref, kx_ref,
     carry_ref) = refs[len(W_NAMES):]
    j = pl.program_id(1)
    tm = x_ref.shape[0]

    @pl.when(j == 0)
    def _():
        carry_ref[...] = jnp.zeros_like(carry_ref)

    x = x_ref[...]
    ms = jnp.mean(x * x, axis=-1, keepdims=True)
    h = ((x * lax.rsqrt(ms + EPS)) * g_ref[...]).astype(BF16)

    def mm(name):
        return jnp.dot(h, w[name][...], preferred_element_type=F32)

    def store_pairs_t(ref, y):
        for pr in range(ref.shape[0]):
            ref[pr] = y[:, pr * LANES:(pr + 1) * LANES].T.astype(BF16)

    store_pairs_t(qf_ref, mm("q_fox"))
    kf_ref[...] = mm("k_fox").astype(BF16)
    store_pairs_t(vf_ref, mm("v_fox"))

    ang = pos_ref[...].astype(F32) * invf_ref[...]
    cos = jnp.cos(ang)
    sin = jnp.sin(ang)
    n_rep = SWA_W // LANES
    cos_q = jnp.concatenate([cos] * n_rep, axis=1)
    sin_q = jnp.concatenate([sin] * n_rep, axis=1)
    def rot_half(y):
        width = y.shape[1]
        lane_y = lax.broadcasted_iota(jnp.int32, y.shape, 1)
        first = (lane_y & (HEAD_DIM - 1)) < HEAD_DIM // 2
        return jnp.where(first, -pltpu.roll(y, width - HEAD_DIM // 2, axis=1),
                         pltpu.roll(y, HEAD_DIM // 2, axis=1))

    q_s = mm("q_swa")
    k_s = mm("k_swa")
    qs_ref[...] = (q_s * cos_q + rot_half(q_s) * sin_q).astype(BF16)
    ks_ref[...] = (k_s * cos + rot_half(k_s) * sin).astype(BF16)
    vs_ref[...] = mm("v_swa").astype(BF16)

    gf_ref[...] = jax.nn.sigmoid(mm("gate_fox") + bg_ref[0:1, :]).astype(BF16)
    gs_ref[...] = jax.nn.sigmoid(mm("gate_swa") + bg_ref[1:2, :]).astype(BF16)

    z = mm("forget") + bf_ref[...]
    lane = lax.broadcasted_iota(jnp.int32, (tm, LANES), 1)
    lf = jnp.minimum(z, 0.0) - jnp.log1p(jnp.exp(-jnp.abs(z)))
    lf = jnp.where(lane < FOX_HEADS, lf, 0.0)
    row = lax.broadcasted_iota(jnp.int32, (tm, tm), 0)
    col = lax.broadcasted_iota(jnp.int32, (tm, tm), 1)
    tri = jnp.where(row >= col, 1.0, 0.0).astype(BF16)
    hi, mid, lo = _split3(lf)
    c = (jnp.dot(tri, hi, preferred_element_type=F32)
         + jnp.dot(tri, mid, preferred_element_type=F32)
         + jnp.dot(tri, lo, preferred_element_type=F32)) + carry_ref[0:1, :]
    carry_ref[0:1, :] = c[tm - 1:tm, :]
    chi, cmid, clo = _split3(c * LOG2E)
    ext = jnp.dot(jnp.concatenate([chi, cmid, clo], axis=1), place_ref[...],
                  preferred_element_type=F32) + ones_ref[...]
    qx_ref[...] = ext[:, :LANES].T.astype(BF16)
    kx_ref[...] = ext[:, LANES:].astype(BF16)


def _bias_placement():
    place = np.zeros((BIAS_TERMS * LANES, 2 * LANES), np.float32)
    ones = np.zeros((1, 2 * LANES), np.float32)
    for h in range(FOX_HEADS):
        for t in range(BIAS_TERMS):
            place[t * LANES + h, BIAS_LANES * h + t] = 1.0
            place[t * LANES + h, LANES + BIAS_LANES * h + BIAS_TERMS + t] = -1.0
            ones[0, BIAS_LANES * h + BIAS_TERMS + t] = 1.0
            ones[0, LANES + BIAS_LANES * h + t] = 1.0
    return jnp.asarray(place, BF16), jnp.asarray(ones, F32)


def _in_proj(x2, pos2, g_attn, weights, invf, bf_pad, b_gate, *, batch, seq, tm):
    T, D = x2.shape
    nj = seq // tm
    w_list = [weights[name] for name in W_NAMES]
    place, ones = _bias_placement()
    row_blk = lambda w: pl.BlockSpec((tm, w), lambda b, j: (b * nj + j, 0))
    const = lambda shape: pl.BlockSpec(shape, lambda b, j: (0, 0))
    n_pair = FOX_HEADS // 2
    pair_t = pl.BlockSpec((None, n_pair, LANES, tm), lambda b, j: (b, 0, 0, j))
    out_shapes = (
        jax.ShapeDtypeStruct((batch, n_pair, LANES, seq), BF16), jax.ShapeDtypeStruct((T, FOX_W), BF16),
        jax.ShapeDtypeStruct((batch, n_pair, LANES, seq), BF16), jax.ShapeDtypeStruct((T, SWA_W), BF16),
        jax.ShapeDtypeStruct((T, SWA_KV_W), BF16), jax.ShapeDtypeStruct((T, SWA_KV_W), BF16),
        jax.ShapeDtypeStruct((T, D), BF16), jax.ShapeDtypeStruct((T, D), BF16),
        jax.ShapeDtypeStruct((batch, LANES, seq), BF16), jax.ShapeDtypeStruct((T, LANES), BF16),
    )
    return pl.pallas_call(
        _in_proj_kernel,
        grid=(batch, nj),
        in_specs=[row_blk(D), row_blk(1), const((1, D)), const((1, LANES)),
                  const((1, LANES)), const((2, D)), const(place.shape), const(ones.shape)]
                 + [const(wt.shape) for wt in w_list],
        out_specs=(pair_t, row_blk(FOX_W), pair_t, row_blk(SWA_W),
                   row_blk(SWA_KV_W), row_blk(SWA_KV_W), row_blk(D), row_blk(D),
                   pl.BlockSpec((None, LANES, tm), lambda b, j: (b, 0, j)), row_blk(LANES)),
        out_shape=out_shapes,
        scratch_shapes=[pltpu.VMEM((8, LANES), F32)],
        compiler_params=_cparams(("arbitrary", "arbitrary")),
        name="in_proj",
    )(x2, pos2, g_attn, invf, bf_pad, b_gate, place, ones, *w_list)


def _fox_kernel(q_ref, qx_ref, k_ref, kx_ref, vt_ref, o_ref, acc_ref, *, tk, unroll):
    p = pl.program_id(1)
    i = pl.program_id(2)
    tq = q_ref.shape[1]
    n_diag = tq // tk
    qt = q_ref[...]
    qxt = qx_ref[...]
    row = lax.broadcasted_iota(jnp.int32, (LANES, tq), 0)
    zero = jnp.zeros_like(qt)
    q_ext = []
    for hh in range(2):
        first = BIAS_LANES * (2 * p + hh)
        head_rows = (row < HEAD_DIM) if hh == 0 else (row >= HEAD_DIM)
        bias_rows = (row >= first) & (row < first + BIAS_LANES)
        q_ext.append(jnp.concatenate([jnp.where(head_rows, qt, zero),
                                      jnp.where(bias_rows, qxt, zero)], axis=0))

    acc_ref[...] = jnp.zeros_like(acc_ref)

    def scores(ks, c0, masked):
        k_ext = jnp.concatenate([k_ref[pl.ds(ks, tk), :], kx_ref[pl.ds(ks, tk), :]], axis=1)
        out = []
        for hh in range(2):
            s = jnp.dot(k_ext, q_ext[hh][:, c0:], preferred_element_type=F32)
            if masked:
                key = lax.broadcasted_iota(jnp.int32, s.shape, 0)
                qry = lax.broadcasted_iota(jnp.int32, s.shape, 1)
                s = jnp.where(key <= qry, s, -jnp.inf)
            out.append(s)
        return out

    def accumulate(ks, c0, s_pair, stats):
        vt = vt_ref[:, pl.ds(ks, tk)]
        out = []
        for hh in range(2):
            m_all, l_all = stats[hh]
            m_old, l_old = m_all[:, c0:], l_all[:, c0:]
            s = s_pair[hh]
            m_new = jnp.maximum(m_old, jnp.max(s, axis=0, keepdims=True))
            alpha = jnp.exp2(m_old - m_new)
            pm = jnp.exp2(s - m_new)
            l_new = alpha * l_old + jnp.sum(pm, axis=0, keepdims=True)
            acc_ref[hh, :, c0:] = alpha * acc_ref[hh, :, c0:] + jnp.dot(
                vt, pm.astype(BF16), preferred_element_type=F32)
            if c0:
                m_new = jnp.concatenate([m_all[:, :c0], m_new], axis=1)
                l_new = jnp.concatenate([l_all[:, :c0], l_new], axis=1)
            out.append((m_new, l_new))
        return tuple(out)

    def run_tiles(first_tile, count, stats, diagonal):
        start = lambda u: pl.multiple_of((first_tile + u) * tk, tk)
        col0 = lambda u: u * tk if diagonal else 0
        s_next = scores(start(0), col0(0), diagonal)
        for u in range(count):
            s_cur = s_next
            if u + 1 < count:
                s_next = scores(start(u + 1), col0(u + 1), diagonal)
            stats = accumulate(start(u), col0(u), s_cur, stats)
        return stats

    init = tuple((jnp.full((1, tq), NEG_BIG, F32), jnp.zeros((1, tq), F32)) for _ in range(2))
    stats = lax.fori_loop(0, i * (n_diag // unroll),
                          lambda jb, st: run_tiles(jb * unroll, unroll, st, False), init)
    stats = run_tiles(i * n_diag, n_diag, stats, True)

    o_t = jnp.where(row < HEAD_DIM, acc_ref[0] * (1.0 / stats[0][1]),
                    acc_ref[1] * (1.0 / stats[1][1]))
    o_ref[...] = o_t.T.astype(BF16)


def _fox_attn(qt, qxt, kf, kx, vt, *, batch, seq, tq, tk, unroll):
    T = kf.shape[0]
    nq = seq // tq
    n_pair = FOX_HEADS // 2
    return pl.pallas_call(
        functools.partial(_fox_kernel, tk=tk, unroll=unroll),
        grid=(batch, n_pair, nq),
        in_specs=[
            pl.BlockSpec((None, None, LANES, tq), lambda b, p, i: (b, p, 0, i)),
            pl.BlockSpec((None, LANES, tq), lambda b, p, i: (b, 0, i)),
            pl.BlockSpec((seq, LANES), lambda b, p, i: (b, p)),
            pl.BlockSpec((seq, LANES), lambda b, p, i: (b, 0)),
            pl.BlockSpec((None, None, LANES, seq), lambda b, p, i: (b, p, 0, 0)),
        ],
        out_specs=pl.BlockSpec((tq, LANES), lambda b, p, i: (b * nq + i, p)),
        out_shape=jax.ShapeDtypeStruct((T, FOX_W), BF16),
        scratch_shapes=[pltpu.VMEM((2, LANES, tq), F32)],
        compiler_params=_cparams(("arbitrary", "arbitrary", "arbitrary")),
        name="fox_attn",
    )(qt, qxt, kf, kx, vt)


def _swa_kernel(sink_ref, q_ref, kc_ref, kp_ref, vc_ref, vp_ref, o_ref):
    i = pl.program_id(1)
    tq = q_ref.shape[0]
    n_win = tq // WINDOW
    n_col = SWA_W // LANES
    lane = lax.broadcasted_iota(jnp.int32, (WINDOW, LANES), 1)
    low = lane < HEAD_DIM
    r = lax.broadcasted_iota(jnp.int32, (WINDOW, 2 * WINDOW), 0)
    c = lax.broadcasted_iota(jnp.int32, (WINDOW, 2 * WINDOW), 1)
    dist = WINDOW + r - c
    band = (dist >= 0) & (dist < WINDOW)
    for w in range(n_win):
        if w == 0:
            k_win = jnp.concatenate([kp_ref[...], kc_ref[0:WINDOW, :]], axis=0)
            v_win = jnp.concatenate([vp_ref[...], vc_ref[0:WINDOW, :]], axis=0)
            first = i == 0
            valid = band & ((c >= WINDOW) | jnp.logical_not(first))
        else:
            k_win = kc_ref[(w - 1) * WINDOW:(w + 1) * WINDOW, :]
            v_win = vc_ref[(w - 1) * WINDOW:(w + 1) * WINDOW, :]
            valid = band
        for col_i in range(n_col):
            qc = q_ref[w * WINDOW:(w + 1) * WINDOW, col_i * LANES:(col_i + 1) * LANES]
            outs = []
            for half in range(2):
                qm = jnp.where(low if half == 0 else jnp.logical_not(low), qc, jnp.zeros_like(qc))
                s = lax.dot_general(qm, k_win, (((1,), (1,)), ((), ())),
                                    preferred_element_type=F32)
                s = jnp.where(valid, s, -jnp.inf)
                sink = sink_ref[2 * col_i + half]
                m = jnp.maximum(jnp.max(s, axis=1, keepdims=True), sink)
                pm = jnp.exp(s - m)
                denom = jnp.sum(pm, axis=1, keepdims=True) + jnp.exp(sink - m)
                pm = pm / denom
                outs.append(jnp.dot(pm.astype(BF16), v_win, preferred_element_type=F32))
            o_ref[w * WINDOW:(w + 1) * WINDOW, col_i * LANES:(col_i + 1) * LANES] = (
                jnp.where(low, outs[0], outs[1]).astype(BF16))


def _swa_attn(sinks_perm, qs, ks, vs, *, batch, seq, tq):
    T = qs.shape[0]
    nq = seq // tq
    n_win = tq // WINDOW
    nb = seq // WINDOW
    cur = lambda w: pl.BlockSpec((tq, w), lambda b, i: (b * nq + i, 0))
    prev = pl.BlockSpec((WINDOW, SWA_KV_W),
                        lambda b, i: (jnp.maximum(b * nb + i * n_win - 1, 0), 0))
    return pl.pallas_call(
        _swa_kernel,
        grid=(batch, nq),
        in_specs=[pl.BlockSpec(memory_space=pltpu.SMEM), cur(SWA_W), cur(SWA_KV_W), prev,
                  cur(SWA_KV_W), prev],
        out_specs=cur(SWA_W),
        out_shape=jax.ShapeDtypeStruct((T, SWA_W), BF16),
        compiler_params=_cparams(("arbitrary", "arbitrary")),
        name="swa_attn",
    )(sinks_perm, qs, ks, ks, vs, vs)


ROUTE_E0, ROUTE_E1, ROUTE_G0, ROUTE_G1, ROUTE_R0, ROUTE_R1 = range(6)
ROUTE_ROWS = 8


def _mix_route_kernel(of_ref, os_ref, gf_ref, gs_ref, x_ref, wpf_ref, wps_ref, wout_ref, gn_ref,
                      wr_ref, br_ref, x1_ref, h2_ref, route_ref, route_t_ref, cnt_ref, carry_ref):
    step = pl.program_id(0)
    tm = x_ref.shape[0]

    @pl.when(step == 0)
    def _():
        carry_ref[...] = jnp.zeros_like(carry_ref)

    a = jnp.dot(of_ref[...], wpf_ref[...], preferred_element_type=F32)
    b = jnp.dot(os_ref[...], wps_ref[...], preferred_element_type=F32)
    merged = gf_ref[...].astype(F32) * a + gs_ref[...].astype(F32) * b
    y = jnp.dot(merged.astype(BF16), wout_ref[...], preferred_element_type=F32)
    x1 = x_ref[...] + y
    x1_ref[...] = x1
    ms = jnp.mean(x1 * x1, axis=-1, keepdims=True)
    h2 = (x1 * lax.rsqrt(ms + EPS)) * gn_ref[...]
    h2_ref[...] = h2

    h_hi = h2.astype(BF16)
    h_lo = (h2 - h_hi.astype(F32)).astype(BF16)
    wr = wr_ref[...]
    r_hi = jnp.dot(h_hi, wr, preferred_element_type=F32)
    logits = (r_hi[:, :LANES] + r_hi[:, LANES:]
              + jnp.dot(h_lo, wr[:, :LANES], preferred_element_type=F32)) + br_ref[...]

    lane = lax.broadcasted_iota(jnp.int32, (tm, LANES), 1)
    is_g = lane < N_GROUPS
    gl = jnp.where(is_g, logits, -jnp.inf)
    gmax = jnp.max(gl, axis=1, keepdims=True)
    gexp = jnp.where(is_g, jnp.exp(gl - gmax), 0.0)
    gprob = gexp / jnp.sum(gexp, axis=1, keepdims=True)
    g_w = jnp.max(gprob, axis=1, keepdims=True)
    g_idx = jnp.min(jnp.where(is_g & (gprob == g_w), lane, LANES), axis=1, keepdims=True)

    e_lane = lane - N_GROUPS
    in_grp = (e_lane >= 0) & (e_lane < N_EXPERTS) & ((e_lane >> 3) == g_idx)
    el = jnp.where(in_grp, logits, -jnp.inf)
    v0 = jnp.max(el, axis=1, keepdims=True)
    i0 = jnp.min(jnp.where(in_grp & (el == v0), lane, LANES), axis=1, keepdims=True)
    el1 = jnp.where(lane == i0, -jnp.inf, el)
    v1 = jnp.max(el1, axis=1, keepdims=True)
    i1 = jnp.min(jnp.where(in_grp & (lane != i0) & (el1 == v1), lane, LANES), axis=1, keepdims=True)
    t = jnp.exp(v1 - v0)
    den = 1.0 + t
    gate0 = g_w * (1.0 / den)
    gate1 = g_w * (t / den)
    e0 = i0 - N_GROUPS
    e1 = i1 - N_GROUPS

    oh0 = jnp.where(lane == e0, 1.0, 0.0)
    oh1 = jnp.where(lane == e1, 1.0, 0.0)
    row = lax.broadcasted_iota(jnp.int32, (tm, tm), 0)
    col = lax.broadcasted_iota(jnp.int32, (tm, tm), 1)
    tri = jnp.where(row > col, 1.0, 0.0).astype(BF16)
    cs0 = jnp.sum(oh0, axis=0, keepdims=True)
    cs1 = jnp.sum(oh1, axis=0, keepdims=True)
    carry = carry_ref[0:1, :]
    pre = jnp.dot(tri, jnp.concatenate([oh0, oh1], axis=1).astype(BF16),
                  preferred_element_type=F32)
    pre0 = pre[:, :LANES] + carry
    pre1 = pre[:, LANES:] + (carry + cs0)
    rank0 = jnp.sum(oh0 * pre0, axis=1, keepdims=True)
    rank1 = jnp.sum(oh1 * pre1, axis=1, keepdims=True)
    new_carry = carry + cs0 + cs1
    carry_ref[0:1, :] = new_carry
    cnt_ref[...] = jnp.broadcast_to(new_carry, cnt_ref.shape)

    route = jnp.where(lane == ROUTE_E0, e0.astype(F32), 0.0)
    route = jnp.where(lane == ROUTE_E1, e1.astype(F32), route)
    route = jnp.where(lane == ROUTE_G0, gate0, route)
    route = jnp.where(lane == ROUTE_G1, gate1, route)
    route = jnp.where(lane == ROUTE_R0, rank0, route)
    route = jnp.where(lane == ROUTE_R1, rank1, route)
    route_ref[...] = route
    route_t_ref[...] = route.T[0:route_t_ref.shape[0], :]


def _mix_route(o_f, o_s, g_f, g_s, x2, wpf, wps, wout, g_ffn, wr_cat, b_r, *, tm):
    T, D = x2.shape
    row_blk = lambda w: pl.BlockSpec((tm, w), lambda i: (i, 0))
    const = lambda shape: pl.BlockSpec(shape, lambda i: (0, 0))
    return pl.pallas_call(
        _mix_route_kernel,
        grid=(T // tm,),
        in_specs=[row_blk(FOX_W), row_blk(SWA_W), row_blk(D), row_blk(D), row_blk(D),
                  const((FOX_W, D)), const((SWA_W, D)), const((D, D)), const((1, D)),
                  const((D, 2 * LANES)), const((1, LANES))],
        out_specs=(row_blk(D), row_blk(D), row_blk(LANES),
                   pl.BlockSpec((ROUTE_ROWS, tm), lambda i: (0, i)), const((8, LANES))),
        out_shape=(jax.ShapeDtypeStruct((T, D), F32), jax.ShapeDtypeStruct((T, D), F32),
                   jax.ShapeDtypeStruct((T, LANES), F32), jax.ShapeDtypeStruct((ROUTE_ROWS, T), F32),
                   jax.ShapeDtypeStruct((8, LANES), F32)),
        scratch_shapes=[pltpu.VMEM((8, LANES), F32)],
        compiler_params=_cparams(("arbitrary",)),
        name="mix_route",
    )(o_f, o_s, g_f, g_s, x2, wpf, wps, wout, g_ffn, wr_cat, b_r)


def _dispatch_kernel(bounds_ref, dest0_ref, dest1_ref, h_ref, xs_ref, zbuf, sem, zsem):
    step = pl.program_id(0)
    tb = h_ref.shape[0]

    @pl.when(step == 0)
    def _():
        zbuf[...] = jnp.zeros_like(zbuf)

        def tail_copy(e):
            tail = pl.multiple_of(bounds_ref[e + 1] - ROW_BLOCK, ROW_BLOCK)
            return pltpu.make_async_copy(zbuf, xs_ref.at[pl.ds(tail, ROW_BLOCK), :], zsem)

        def for_nonempty(action):
            def body(e, carry):
                @pl.when(bounds_ref[e + 1] > bounds_ref[e])
                def _():
                    action(tail_copy(e))
                return carry
            lax.fori_loop(0, N_EXPERTS, body, 0)

        n_rows = xs_ref.shape[0]
        first_unused = bounds_ref[N_EXPERTS] // ROW_BLOCK

        def unused_copy(blk):
            return pltpu.make_async_copy(
                zbuf, xs_ref.at[pl.ds(pl.multiple_of(blk * ROW_BLOCK, ROW_BLOCK), ROW_BLOCK), :], zsem)

        def for_unused(action):
            def body(blk, carry):
                action(unused_copy(blk))
                return carry
            lax.fori_loop(first_unused, n_rows // ROW_BLOCK, body, 0)

        for_nonempty(lambda cp: cp.start())
        for_unused(lambda cp: cp.start())
        for_nonempty(lambda cp: cp.wait())
        for_unused(lambda cp: cp.wait())

    def start(t, carry):
        for k, dest_ref in enumerate((dest0_ref, dest1_ref)):
            d = dest_ref[t]
            pltpu.make_async_copy(h_ref.at[pl.ds(t, 1), :], xs_ref.at[pl.ds(d, 1), :],
                                  sem).start(priority=k)
        return carry

    lax.fori_loop(0, tb, start, 0, unroll=DMA_UNROLL)
    for _ in range(2):
        pltpu.make_async_copy(h_ref, xs_ref.at[pl.ds(0, tb), :], sem).wait()


def _dispatch(bounds, dest0, dest1, h2, n_rows, *, tb):
    T, D = h2.shape
    smem_blk = pl.BlockSpec((tb,), lambda i, bd: (i,), memory_space=pltpu.SMEM)
    grid_spec = pltpu.PrefetchScalarGridSpec(
        num_scalar_prefetch=1,
        grid=(T // tb,),
        in_specs=[smem_blk, smem_blk, pl.BlockSpec((tb, D), lambda i, bd: (i, 0))],
        out_specs=pl.BlockSpec(memory_space=pl.ANY),
        scratch_shapes=[pltpu.VMEM((ROW_BLOCK, D), F32), pltpu.SemaphoreType.DMA(()),
                        pltpu.SemaphoreType.DMA(())],
    )
    return pl.pallas_call(
        _dispatch_kernel,
        grid_spec=grid_spec,
        out_shape=jax.ShapeDtypeStruct((n_rows, D), F32),
        compiler_params=_cparams(("arbitrary",)),
        name="dispatch",
    )(bounds, dest0, dest1, h2)


def _experts_kernel(be_ref, nu_ref, xs_ref, w1_ref, w3_ref, w2_ref, ys_ref, w1b, w3b, w2b):
    i = pl.program_id(0)
    e = be_ref[i]
    e_prev = be_ref[jnp.maximum(i - 1, 0)]
    used = i < nu_ref[0]

    @pl.when(used & ((i == 0) | (e != e_prev)))
    def _():
        w1b[...] = w1_ref[...].astype(BF16)
        w3b[...] = w3_ref[...].astype(BF16)
        w2b[...] = w2_ref[...].astype(BF16)

    @pl.when(used)
    def _():
        xb = xs_ref[...].astype(BF16)
        de = w1b.shape[1]
        n_part = 2
        part = de // n_part
        acts = []
        for q in range(n_part):
            cols = slice(q * part, (q + 1) * part)
            h1 = jnp.dot(xb, w1b[:, cols], preferred_element_type=F32)
            h3 = jnp.dot(xb, w3b[:, cols], preferred_element_type=F32)
            acts.append(((h1 * jax.nn.sigmoid(h1)) * h3).astype(BF16))
        y = jnp.dot(acts[0], w2b[0:part, :], preferred_element_type=F32)
        for q in range(1, n_part):
            y = y + jnp.dot(acts[q], w2b[q * part:(q + 1) * part, :], preferred_element_type=F32)
        ys_ref[...] = y

    @pl.when(jnp.logical_not(used))
    def _():
        ys_ref[...] = jnp.zeros_like(ys_ref)


def _experts(blk_expert, n_used, xs, w1, w3, w2):
    P, D = xs.shape
    de = w1.shape[-1]
    n_blk = P // ROW_BLOCK
    grid_spec = pltpu.PrefetchScalarGridSpec(
        num_scalar_prefetch=2,
        grid=(n_blk,),
        in_specs=[pl.BlockSpec((ROW_BLOCK, D),
                               lambda i, be, nu: (jnp.minimum(i, jnp.maximum(nu[0] - 1, 0)), 0)),
                  pl.BlockSpec((None, D, de), lambda i, be, nu: (be[i], 0, 0)),
                  pl.BlockSpec((None, D, de), lambda i, be, nu: (be[i], 0, 0)),
                  pl.BlockSpec((None, de, D), lambda i, be, nu: (be[i], 0, 0))],
        out_specs=pl.BlockSpec((ROW_BLOCK, D), lambda i, be, nu: (i, 0)),
        scratch_shapes=[pltpu.VMEM((D, de), BF16), pltpu.VMEM((D, de), BF16),
                        pltpu.VMEM((de, D), BF16)],
    )
    return pl.pallas_call(
        _experts_kernel,
        grid_spec=grid_spec,
        out_shape=jax.ShapeDtypeStruct((P, D), F32),
        compiler_params=_cparams(("arbitrary",)),
        name="experts",
    )(blk_expert, n_used, xs, w1, w3, w2)


def _combine_kernel(d0_cur, d1_cur, d0_next, d1_next, x1_ref, route_ref, gfin_ref, ys_ref, o_ref,
                    ybuf, sem):
    i = pl.program_id(0)
    tb = x1_ref.shape[0]
    slot = i % 2

    def issue(dest_refs, buf_slot):
        def start(t, carry):
            for k, dest_ref in enumerate(dest_refs):
                d = dest_ref[t]
                pltpu.make_async_copy(ys_ref.at[pl.ds(d, 1), :], ybuf.at[buf_slot, k, pl.ds(t, 1), :],
                                      sem.at[buf_slot]).start(priority=k)
            return carry
        lax.fori_loop(0, tb, start, 0, unroll=DMA_UNROLL)

    @pl.when(i == 0)
    def _():
        issue((d0_cur, d1_cur), 0)

    @pl.when(i + 1 < pl.num_programs(0))
    def _():
        issue((d0_next, d1_next), 1 - slot)

    for k in range(2):
        pltpu.make_async_copy(ys_ref.at[pl.ds(0, tb), :], ybuf.at[slot, k], sem.at[slot]).wait()

    route = route_ref[...]
    lane = lax.broadcasted_iota(jnp.int32, route.shape, 1)
    g0 = jnp.sum(jnp.where(lane == ROUTE_G0, route, 0.0), axis=1, keepdims=True)
    g1 = jnp.sum(jnp.where(lane == ROUTE_G1, route, 0.0), axis=1, keepdims=True)
    moe = g0 * ybuf[slot, 0] + g1 * ybuf[slot, 1]
    x2 = x1_ref[...] + moe
    ms = jnp.mean(x2 * x2, axis=-1, keepdims=True)
    o_ref[...] = (x2 * lax.rsqrt(ms + EPS)) * gfin_ref[...]


def _combine(dest0, dest1, x1, route, g_final, ys, *, tb):
    T, D = x1.shape
    nb = T // tb
    cur = pl.BlockSpec((tb,), lambda i: (i,), memory_space=pltpu.SMEM)
    nxt = pl.BlockSpec((tb,), lambda i: (jnp.minimum(i + 1, nb - 1),), memory_space=pltpu.SMEM)
    return pl.pallas_call(
        _combine_kernel,
        grid=(nb,),
        in_specs=[cur, cur, nxt, nxt,
                  pl.BlockSpec((tb, D), lambda i: (i, 0)),
                  pl.BlockSpec((tb, LANES), lambda i: (i, 0)),
                  pl.BlockSpec((1, D), lambda i: (0, 0)),
                  pl.BlockSpec(memory_space=pl.ANY)],
        out_specs=pl.BlockSpec((tb, D), lambda i: (i, 0)),
        out_shape=jax.ShapeDtypeStruct((T, D), F32),
        scratch_shapes=[pltpu.VMEM((2, 2, tb, D), F32), pltpu.SemaphoreType.DMA((2,))],
        compiler_params=_cparams(("arbitrary",)),
        name="combine",
    )(dest0, dest1, dest0, dest1, x1, route, g_final, ys)


def _pick(n, prefs):
    for p in prefs:
        if n % p == 0:
            return p
    return n


def _layer(x2, pos2, batch, seq, attn_norm, w_in, b_forget, b_gate, sinks, w_proj_fox, w_proj_swa,
           w_out, ffn_norm, w_group, b_group, w_expert, b_expert, w1, w3, w2, final_norm):
    T, D = x2.shape
    scale = HEAD_DIM ** -0.5

    sp = np.cumsum([FOX_W, FOX_W, FOX_W, FOX_HEADS, SWA_W, SWA_KV_W, SWA_KV_W, D, D])[:-1]
    wq_f, wk_f, wv_f, w_fl, wq_s, wk_s, wv_s, wg_f, wg_s = jnp.split(w_in, sp, axis=1)
    head_order = np.asarray(SWA_HEAD_ORDER)
    wq_s = wq_s.reshape(D, SWA_HEADS, HEAD_DIM)[:, head_order, :].reshape(D, SWA_W)
    pieces = (wq_f * (scale * LOG2E), wk_f, wv_f, wq_s * scale, wk_s, wv_s,
              wg_f, wg_s, jnp.pad(w_fl, ((0, 0), (0, LANES - FOX_HEADS))))
    weights = {name: piece.astype(BF16) for name, piece in zip(W_NAMES, pieces)}
    half = HEAD_DIM // 2
    inv_freq = ROPE_THETA ** (-jnp.arange(half, dtype=F32) * 2.0 / HEAD_DIM)
    invf = jnp.tile(inv_freq, LANES // half).reshape(1, LANES)
    bf_pad = jnp.pad(b_forget.astype(F32), (0, LANES - FOX_HEADS)).reshape(1, LANES)
    sinks_perm = sinks.astype(F32)[head_order]
    wps = w_proj_swa.reshape(SWA_HEADS, HEAD_DIM, D)[head_order].reshape(SWA_W, D).astype(BF16)
    wpf = w_proj_fox.astype(BF16)
    wout = w_out.astype(BF16)
    w_r = jnp.pad(jnp.concatenate([w_group, w_expert], axis=1).astype(F32),
                  ((0, 0), (0, LANES - N_GROUPS - N_EXPERTS)))
    wr_hi = w_r.astype(BF16)
    wr_lo = (w_r - wr_hi.astype(F32)).astype(BF16)
    wr_cat = jnp.concatenate([wr_hi, wr_lo], axis=1)
    b_r = jnp.pad(jnp.concatenate([b_group, b_expert]).astype(F32),
                  (0, LANES - N_GROUPS - N_EXPERTS)).reshape(1, LANES)

    tm1 = _pick(seq, (512, 256, 128))
    qt, kf, vt, qs, ks, vs, g_f, g_s, qxt, kx = _in_proj(
        x2, pos2, attn_norm.reshape(1, D), weights, invf, bf_pad, b_gate.astype(F32),
        batch=batch, seq=seq, tm=tm1)
    tq = _pick(seq, (FOX_TQ, 128))
    tk = min(tq, FOX_TK)
    o_f = _fox_attn(qt, qxt, kf, kx, vt, batch=batch, seq=seq, tq=tq, tk=tk,
                    unroll=min(FOX_UNROLL, tq // tk))
    o_s = _swa_attn(sinks_perm, qs, ks, vs, batch=batch, seq=seq, tq=_pick(seq, (512, 256, 128)))

    tm3 = _pick(T, (512, 256, 128))
    x1, h2, route, route_t, counts = _mix_route(o_f, o_s, g_f, g_s, x2, wpf, wps, wout,
                                       ffn_norm.reshape(1, D), wr_cat, b_r, tm=tm3)

    cnt = counts[0, :N_EXPERTS].astype(jnp.int32)
    padded = (cnt + ROW_BLOCK - 1) // ROW_BLOCK * ROW_BLOCK
    pad_end = jnp.cumsum(padded)
    pad_start = pad_end - padded
    bounds = jnp.concatenate([jnp.zeros((1,), jnp.int32), pad_end]).astype(jnp.int32)
    fields = route_t.astype(jnp.int32)
    dest0 = pad_start[fields[ROUTE_E0]] + fields[ROUTE_R0]
    dest1 = pad_start[fields[ROUTE_E1]] + fields[ROUTE_R1]
    P = 2 * T + N_EXPERTS * ROW_BLOCK
    n_blk = P // ROW_BLOCK
    blk_pos = jnp.arange(n_blk, dtype=jnp.int32) * ROW_BLOCK
    blk_expert = jnp.minimum(
        jnp.sum((pad_end[None, :] <= blk_pos[:, None]).astype(jnp.int32), axis=1), N_EXPERTS - 1)
    n_used = (pad_end[-1:] // ROW_BLOCK).astype(jnp.int32)

    tb = _pick(T, (256, 128))
    xs = _dispatch(bounds, dest0, dest1, h2, P, tb=tb)
    ys = _experts(blk_expert, n_used, xs, w1, w3, w2)
    return _combine(dest0, dest1, x1, route, final_norm.reshape(1, D), ys, tb=tb)


def kernel(x, positions, attn_norm, w_in, b_forget, b_gate, attn_sinks, w_proj_fox, w_proj_swa,
           w_out, ffn_norm, w_group, b_group, w_expert, b_expert, w1, w3, w2, final_norm):
    B, S, D = x.shape
    depth = attn_norm.shape[0]
    x2 = x.reshape(B * S, D)
    pos2 = positions.reshape(B * S, 1).astype(jnp.int32)
    assert depth == 1, "a single layer is followed directly by the final norm"
    out = _layer(x2, pos2, B, S, attn_norm[0], w_in[0], b_forget[0], b_gate[0], attn_sinks[0],
                 w_proj_fox[0], w_proj_swa[0], w_out[0], ffn_norm[0], w_group[0], b_group[0],
                 w_expert[0], b_expert[0], w1[0], w3[0], w2[0], final_norm)
    return out.reshape(B, S, D)
```

```python
import functools

import numpy as np
import jax
import jax.numpy as jnp
from jax import lax
from jax.experimental import pallas as pl
from jax.experimental.pallas import tpu as pltpu

HEAD_DIM = 64
FOX_HEADS = 8
SWA_HEADS = 8
SWA_KV_HEADS = 2
WINDOW = 128
ROPE_THETA = 10000.0
N_GROUPS = 8
EXPERTS_PER_GROUP = 8
N_EXPERTS = N_GROUPS * EXPERTS_PER_GROUP
ROW_BLOCK = 256
DMA_UNROLL = 8
EPS = 1e-6
LANES = 128
NEG_BIG = -1e30
LOG2E = 1.4426950408889634
BIAS_TERMS = 3
BIAS_LANES = 2 * BIAS_TERMS
FOX_TQ = 1024
FOX_TK = 256
FOX_UNROLL = 4

FOX_W = FOX_HEADS * HEAD_DIM
SWA_W = SWA_HEADS * HEAD_DIM
SWA_KV_W = SWA_KV_HEADS * HEAD_DIM
SWA_HEAD_ORDER = (0, 4, 1, 5, 2, 6, 3, 7)

F32 = jnp.float32
BF16 = jnp.bfloat16
VMEM_LIMIT = 56 * 1024 * 1024


def _cparams(sem):
    return pltpu.CompilerParams(dimension_semantics=sem, vmem_limit_bytes=VMEM_LIMIT)


def _split3(x):
    hi = x.astype(BF16)
    r1 = x - hi.astype(F32)
    mid = r1.astype(BF16)
    lo = (r1 - mid.astype(F32)).astype(BF16)
    return hi, mid, lo


W_NAMES = ("q_fox", "k_fox", "v_fox", "q_swa", "k_swa", "v_swa", "gate_fox", "gate_swa", "forget")


def _in_proj_kernel(x_ref, pos_ref, g_ref, invf_ref, bf_ref, bg_ref, place_ref, ones_ref, *refs):
    w = dict(zip(W_NAMES, refs[:len(W_NAMES)]))
    (qf_ref, kf_ref, vf_ref, qs_ref, ks_ref, vs_ref, gf_ref, gs_ref, qx_ref, kx_ref,
     carry_ref) = refs[len(W_NAMES):]
    j = pl.program_id(1)
    tm = x_ref.shape[0]

    @pl.when(j == 0)
    def _():
        carry_ref[...] = jnp.zeros_like(carry_ref)

    x = x_ref[...]
    ms = jnp.mean(x * x, axis=-1, keepdims=True)
    h = ((x * lax.rsqrt(ms + EPS)) * g_ref[...]).astype(BF16)

    def mm(name):
        return jnp.dot(h, w[name][...], preferred_element_type=F32)

    def store_pairs_t(ref, y):
        for pr in range(ref.shape[0]):
            ref[pr] = y[:, pr * LANES:(pr + 1) * LANES].T.astype(BF16)

    store_pairs_t(qf_ref, mm("q_fox"))
    kf_ref[...] = mm("k_fox").astype(BF16)
    store_pairs_t(vf_ref, mm("v_fox"))

    ang = pos_ref[...].astype(F32) * invf_ref[...]
    cos = jnp.cos(ang)
    sin = jnp.sin(ang)
    n_rep = SWA_W // LANES
    cos_q = jnp.concatenate([cos] * n_rep, axis=1)
    sin_q = jnp.concatenate([sin] * n_rep, axis=1)
    def rot_half(y):
        width = y.shape[1]
        lane_y = lax.broadcasted_iota(jnp.int32, y.shape, 1)
        first = (lane_y & (HEAD_DIM - 1)) < HEAD_DIM // 2
        return jnp.where(first, -pltpu.roll(y, width - HEAD_DIM // 2, axis=1),
                         pltpu.roll(y, HEAD_DIM // 2, axis=1))

    q_s = mm("q_swa")
    k_s = mm("k_swa")
    qs_ref[...] = (q_s * cos_q + rot_half(q_s) * sin_q).astype(BF16)
    ks_ref[...] = (k_s * cos + rot_half(k_s) * sin).astype(BF16)
    vs_ref[...] = mm("v_swa").astype(BF16)

    gf_ref[...] = jax.nn.sigmoid(mm("gate_fox") + bg_ref[0:1, :]).astype(BF16)
    gs_ref[...] = jax.nn.sigmoid(mm("gate_swa") + bg_ref[1:2, :]).astype(BF16)

    z = mm("forget") + bf_ref[...]
    lane = lax.broadcasted_iota(jnp.int32, (tm, LANES), 1)
    lf = jnp.minimum(z, 0.0) - jnp.log1p(jnp.exp(-jnp.abs(z)))
    lf = jnp.where(lane < FOX_HEADS, lf, 0.0)
    row = lax.broadcasted_iota(jnp.int32, (tm, tm), 0)
    col = lax.broadcasted_iota(jnp.int32, (tm, tm), 1)
    tri = jnp.where(row >= col, 1.0, 0.0).astype(BF16)
    hi, mid, lo = _split3(lf)
    c = (jnp.dot(tri, hi, preferred_element_type=F32)
         + jnp.dot(tri, mid, preferred_element_type=F32)
         + jnp.dot(tri, lo, preferred_element_type=F32)) + carry_ref[0:1, :]
    carry_ref[0:1, :] = c[tm - 1:tm, :]
    chi, cmid, clo = _split3(c * LOG2E)
    ext = jnp.dot(jnp.concatenate([chi, cmid, clo], axis=1), place_ref[...],
                  preferred_element_type=F32) + ones_ref[...]
    qx_ref[...] = ext[:, :LANES].T.astype(BF16)
    kx_ref[...] = ext[:, LANES:].astype(BF16)


def _bias_placement():
    place = np.zeros((BIAS_TERMS * LANES, 2 * LANES), np.float32)
    ones = np.zeros((1, 2 * LANES), np.float32)
    for h in range(FOX_HEADS):
        for t in range(BIAS_TERMS):
            place[t * LANES + h, BIAS_LANES * h + t] = 1.0
            place[t * LANES + h, LANES + BIAS_LANES * h + BIAS_TERMS + t] = -1.0
            ones[0, BIAS_LANES * h + BIAS_TERMS + t] = 1.0
            ones[0, LANES + BIAS_LANES * h + t] = 1.0
    return jnp.asarray(place, BF16), jnp.asarray(ones, F32)


def _in_proj(x2, pos2, g_attn, weights, invf, bf_pad, b_gate, *, batch, seq, tm):
    T, D = x2.shape
    nj = seq // tm
    w_list = [weights[name] for name in W_NAMES]
    place, ones = _bias_placement()
    row_blk = lambda w: pl.BlockSpec((tm, w), lambda b, j: (b * nj + j, 0))
    const = lambda shape: pl.BlockSpec(shape, lambda b, j: (0, 0))
    n_pair = FOX_HEADS // 2
    pair_t = pl.BlockSpec((None, n_pair, LANES, tm), lambda b, j: (b, 0, 0, j))
    out_shapes = (
        jax.ShapeDtypeStruct((batch, n_pair, LANES, seq), BF16), jax.ShapeDtypeStruct((T, FOX_W), BF16),
        jax.ShapeDtypeStruct((batch, n_pair, LANES, seq), BF16), jax.ShapeDtypeStruct((T, SWA_W), BF16),
        jax.ShapeDtypeStruct((T, SWA_KV_W), BF16), jax.ShapeDtypeStruct((T, SWA_KV_W), BF16),
        jax.ShapeDtypeStruct((T, D), BF16), jax.ShapeDtypeStruct((T, D), BF16),
        jax.ShapeDtypeStruct((batch, LANES, seq), BF16), jax.ShapeDtypeStruct((T, LANES), BF16),
    )
    return pl.pallas_call(
        _in_proj_kernel,
        grid=(batch, nj),
        in_specs=[row_blk(D), row_blk(1), const((1, D)), const((1, LANES)),
                  const((1, LANES)), const((2, D)), const(place.shape), const(ones.shape)]
                 + [const(wt.shape) for wt in w_list],
        out_specs=(pair_t, row_blk(FOX_W), pair_t, row_blk(SWA_W),
                   row_blk(SWA_KV_W), row_blk(SWA_KV_W), row_blk(D), row_blk(D),
                   pl.BlockSpec((None, LANES, tm), lambda b, j: (b, 0, j)), row_blk(LANES)),
        out_shape=out_shapes,
        scratch_shapes=[pltpu.VMEM((8, LANES), F32)],
        compiler_params=_cparams(("arbitrary", "arbitrary")),
        name="in_proj",
    )(x2, pos2, g_attn, invf, bf_pad, b_gate, place, ones, *w_list)


def _fox_kernel(q_ref, qx_ref, k_ref, kx_ref, vt_ref, o_ref, acc_ref, *, tk, unroll):
    p = pl.program_id(1)
    i = pl.program_id(2)
    tq = q_ref.shape[1]
    n_diag = tq // tk
    qt = q_ref[...]
    qxt = qx_ref[...]
    row = lax.broadcasted_iota(jnp.int32, (LANES, tq), 0)
    zero = jnp.zeros_like(qt)
    q_ext = []
    for hh in range(2):
        first = BIAS_LANES * (2 * p + hh)
        head_rows = (row < HEAD_DIM) if hh == 0 else (row >= HEAD_DIM)
        bias_rows = (row >= first) & (row < first + BIAS_LANES)
        q_ext.append(jnp.concatenate([jnp.where(head_rows, qt, zero),
                                      jnp.where(bias_rows, qxt, zero)], axis=0))

    acc_ref[...] = jnp.zeros_like(acc_ref)

    def scores(ks, c0, masked):
        k_ext = jnp.concatenate([k_ref[pl.ds(ks, tk), :], kx_ref[pl.ds(ks, tk), :]], axis=1)
        out = []
        for hh in range(2):
            s = jnp.dot(k_ext, q_ext[hh][:, c0:], preferred_element_type=F32)
            if masked:
                key = lax.broadcasted_iota(jnp.int32, s.shape, 0)
                qry = lax.broadcasted_iota(jnp.int32, s.shape, 1)
                s = jnp.where(key <= qry, s, -jnp.inf)
            out.append(s)
        return out

    def accumulate(ks, c0, s_pair, stats):
        vt = vt_ref[:, pl.ds(ks, tk)]
        out = []
        for hh in range(2):
            m_all, l_all = stats[hh]
            m_old, l_old = m_all[:, c0:], l_all[:, c0:]
            s = s_pair[hh]
            m_new = jnp.maximum(m_old, jnp.max(s, axis=0, keepdims=True))
            alpha = jnp.exp2(m_old - m_new)
            pm = jnp.exp2(s - m_new)
            l_new = alpha * l_old + jnp.sum(pm, axis=0, keepdims=True)
            acc_ref[hh, :, c0:] = alpha * acc_ref[hh, :, c0:] + jnp.dot(
                vt, pm.astype(BF16), preferred_element_type=F32)
            if c0:
                m_new = jnp.concatenate([m_all[:, :c0], m_new], axis=1)
                l_new = jnp.concatenate([l_all[:, :c0], l_new], axis=1)
            out.append((m_new, l_new))
        return tuple(out)

    def run_tiles(first_tile, count, stats, diagonal):
        start = lambda u: pl.multiple_of((first_tile + u) * tk, tk)
        col0 = lambda u: u * tk if diagonal else 0
        s_next = scores(start(0), col0(0), diagonal)
        for u in range(count):
            s_cur = s_next
            if u + 1 < count:
                s_next = scores(start(u + 1), col0(u + 1), diagonal)
            stats = accumulate(start(u), col0(u), s_cur, stats)
        return stats

    init = tuple((jnp.full((1, tq), NEG_BIG, F32), jnp.zeros((1, tq), F32)) for _ in range(2))
    stats = lax.fori_loop(0, i * (n_diag // unroll),
                          lambda jb, st: run_tiles(jb * unroll, unroll, st, False), init)
    stats = run_tiles(i * n_diag, n_diag, stats, True)

    o_t = jnp.where(row < HEAD_DIM, acc_ref[0] * (1.0 / stats[0][1]),
                    acc_ref[1] * (1.0 / stats[1][1]))
    o_ref[...] = o_t.T.astype(BF16)


def _fox_attn(qt, qxt, kf, kx, vt, *, batch, seq, tq, tk, unroll):
    T = kf.shape[0]
    nq = seq // tq
    n_pair = FOX_HEADS // 2
    return pl.pallas_call(
        functools.partial(_fox_kernel, tk=tk, unroll=unroll),
        grid=(batch, n_pair, nq),
        in_specs=[
            pl.BlockSpec((None, None, LANES, tq), lambda b, p, i: (b, p, 0, i)),
            pl.BlockSpec((None, LANES, tq), lambda b, p, i: (b, 0, i)),
            pl.BlockSpec((seq, LANES), lambda b, p, i: (b, p)),
            pl.BlockSpec((seq, LANES), lambda b, p, i: (b, 0)),
            pl.BlockSpec((None, None, LANES, seq), lambda b, p, i: (b, p, 0, 0)),
        ],
        out_specs=pl.BlockSpec((tq, LANES), lambda b, p, i: (b * nq + i, p)),
        out_shape=jax.ShapeDtypeStruct((T, FOX_W), BF16),
        scratch_shapes=[pltpu.VMEM((2, LANES, tq), F32)],
        compiler_params=_cparams(("arbitrary", "arbitrary", "arbitrary")),
        name="fox_attn",
    )(qt, qxt, kf, kx, vt)


def _swa_kernel(sink_ref, q_ref, kc_ref, kp_ref, vc_ref, vp_ref, o_ref):
    i = pl.program_id(1)
    tq = q_ref.shape[0]
    n_win = tq // WINDOW
    n_col = SWA_W // LANES
    lane = lax.broadcasted_iota(jnp.int32, (WINDOW, LANES), 1)
    low = lane < HEAD_DIM
    r = lax.broadcasted_iota(jnp.int32, (WINDOW, 2 * WINDOW), 0)
    c = lax.broadcasted_iota(jnp.int32, (WINDOW, 2 * WINDOW), 1)
    dist = WINDOW + r - c
    band = (dist >= 0) & (dist < WINDOW)
    for w in range(n_win):
        if w == 0:
            k_win = jnp.concatenate([kp_ref[...], kc_ref[0:WINDOW, :]], axis=0)
            v_win = jnp.concatenate([vp_ref[...], vc_ref[0:WINDOW, :]], axis=0)
            first = i == 0
            valid = band & ((c >= WINDOW) | jnp.logical_not(first))
        else:
            k_win = kc_ref[(w - 1) * WINDOW:(w + 1) * WINDOW, :]
            v_win = vc_ref[(w - 1) * WINDOW:(w + 1) * WINDOW, :]
            valid = band
        for col_i in range(n_col):
            qc = q_ref[w * WINDOW:(w + 1) * WINDOW, col_i * LANES:(col_i + 1) * LANES]
            outs = []
            for half in range(2):
                qm = jnp.where(low if half == 0 else jnp.logical_not(low), qc, jnp.zeros_like(qc))
                s = lax.dot_general(qm, k_win, (((1,), (1,)), ((), ())),
                                    preferred_element_type=F32)
                s = jnp.where(valid, s, -jnp.inf)
                sink = sink_ref[2 * col_i + half]
                m = jnp.maximum(jnp.max(s, axis=1, keepdims=True), sink)
                pm = jnp.exp(s - m)
                denom = jnp.sum(pm, axis=1, keepdims=True) + jnp.exp(sink - m)
                pm = pm / denom
                outs.append(jnp.dot(pm.astype(BF16), v_win, preferred_element_type=F32))
            o_ref[w * WINDOW:(w + 1) * WINDOW, col_i * LANES:(col_i + 1) * LANES] = (
                jnp.where(low, outs[0], outs[1]).astype(BF16))


def _swa_attn(sinks_perm, qs, ks, vs, *, batch, seq, tq):
    T = qs.shape[0]
    nq = seq // tq
    n_win = tq // WINDOW
    nb = seq // WINDOW
    cur = lambda w: pl.BlockSpec((tq, w), lambda b, i: (b * nq + i, 0))
    prev = pl.BlockSpec((WINDOW, SWA_KV_W),
                        lambda b, i: (jnp.maximum(b * nb + i * n_win - 1, 0), 0))
    return pl.pallas_call(
        _swa_kernel,
        grid=(batch, nq),
        in_specs=[pl.BlockSpec(memory_space=pltpu.SMEM), cur(SWA_W), cur(SWA_KV_W), prev,
                  cur(SWA_KV_W), prev],
        out_specs=cur(SWA_W),
        out_shape=jax.ShapeDtypeStruct((T, SWA_W), BF16),
        compiler_params=_cparams(("arbitrary", "arbitrary")),
        name="swa_attn",
    )(sinks_perm, qs, ks, ks, vs, vs)


ROUTE_E0, ROUTE_E1, ROUTE_G0, ROUTE_G1, ROUTE_R0, ROUTE_R1 = range(6)
ROUTE_ROWS = 8


def _mix_route_kernel(of_ref, os_ref, gf_ref, gs_ref, x_ref, wpf_ref, wps_ref, wout_ref, gn_ref,
                      wr_ref, br_ref, x1_ref, h2_ref, route_ref, route_t_ref, cnt_ref, carry_ref):
    step = pl.program_id(0)
    tm = x_ref.shape[0]

    @pl.when(step == 0)
    def _():
        carry_ref[...] = jnp.zeros_like(carry_ref)

    a = jnp.dot(of_ref[...], wpf_ref[...], preferred_element_type=F32)
    b = jnp.dot(os_ref[...], wps_ref[...], preferred_element_type=F32)
    merged = gf_ref[...].astype(F32) * a + gs_ref[...].astype(F32) * b
    y = jnp.dot(merged.astype(BF16), wout_ref[...], preferred_element_type=F32)
    x1 = x_ref[...] + y
    x1_ref[...] = x1
    ms = jnp.mean(x1 * x1, axis=-1, keepdims=True)
    h2 = (x1 * lax.rsqrt(ms + EPS)) * gn_ref[...]
    h2_ref[...] = h2

    h_hi = h2.astype(BF16)
    h_lo = (h2 - h_hi.astype(F32)).astype(BF16)
    wr = wr_ref[...]
    r_hi = jnp.dot(h_hi, wr, preferred_element_type=F32)
    logits = (r_hi[:, :LANES] + r_hi[:, LANES:]
              + jnp.dot(h_lo, wr[:, :LANES], preferred_element_type=F32)) + br_ref[...]

    lane = lax.broadcasted_iota(jnp.int32, (tm, LANES), 1)
    is_g = lane < N_GROUPS
    gl = jnp.where(is_g, logits, -jnp.inf)
    gmax = jnp.max(gl, axis=1, keepdims=True)
    gexp = jnp.where(is_g, jnp.exp(gl - gmax), 0.0)
    gprob = gexp / jnp.sum(gexp, axis=1, keepdims=True)
    g_w = jnp.max(gprob, axis=1, keepdims=True)
    g_idx = jnp.min(jnp.where(is_g & (gprob == g_w), lane, LANES), axis=1, keepdims=True)

    e_lane = lane - N_GROUPS
    in_grp = (e_lane >= 0) & (e_lane < N_EXPERTS) & ((e_lane >> 3) == g_idx)
    el = jnp.where(in_grp, logits, -jnp.inf)
    v0 = jnp.max(el, axis=1, keepdims=True)
    i0 = jnp.min(jnp.where(in_grp & (el == v0), lane, LANES), axis=1, keepdims=True)
    el1 = jnp.where(lane == i0, -jnp.inf, el)
    v1 = jnp.max(el1, axis=1, keepdims=True)
    i1 = jnp.min(jnp.where(in_grp & (lane != i0) & (el1 == v1), lane, LANES), axis=1, keepdims=True)
    t = jnp.exp(v1 - v0)
    den = 1.0 + t
    gate0 = g_w * (1.0 / den)
    gate1 = g_w * (t / den)
    e0 = i0 - N_GROUPS
    e1 = i1 - N_GROUPS

    oh0 = jnp.where(lane == e0, 1.0, 0.0)
    oh1 = jnp.where(lane == e1, 1.0, 0.0)
    row = lax.broadcasted_iota(jnp.int32, (tm, tm), 0)
    col = lax.broadcasted_iota(jnp.int32, (tm, tm), 1)
    tri = jnp.where(row > col, 1.0, 0.0).astype(BF16)
    cs0 = jnp.sum(oh0, axis=0, keepdims=True)
    cs1 = jnp.sum(oh1, axis=0, keepdims=True)
    carry = carry_ref[0:1, :]
    pre = jnp.dot(tri, jnp.concatenate([oh0, oh1], axis=1).astype(BF16),
                  preferred_element_type=F32)
    pre0 = pre[:, :LANES] + carry
    pre1 = pre[:, LANES:] + (carry + cs0)
    rank0 = jnp.sum(oh0 * pre0, axis=1, keepdims=True)
    rank1 = jnp.sum(oh1 * pre1, axis=1, keepdims=True)
    new_carry = carry + cs0 + cs1
    carry_ref[0:1, :] = new_carry
    cnt_ref[...] = jnp.broadcast_to(new_carry, cnt_ref.shape)

    route = jnp.where(lane == ROUTE_E0, e0.astype(F32), 0.0)
    route = jnp.where(lane == ROUTE_E1, e1.astype(F32), route)
    route = jnp.where(lane == ROUTE_G0, gate0, route)
    route = jnp.where(lane == ROUTE_G1, gate1, route)
    route = jnp.where(lane == ROUTE_R0, rank0, route)
    route = jnp.where(lane == ROUTE_R1, rank1, route)
    route_ref[...] = route
    route_t_ref[...] = route.T[0:route_t_ref.shape[0], :]


def _mix_route(o_f, o_s, g_f, g_s, x2, wpf, wps, wout, g_ffn, wr_cat, b_r, *, tm):
    T, D = x2.shape
    row_blk = lambda w: pl.BlockSpec((tm, w), lambda i: (i, 0))
    const = lambda shape: pl.BlockSpec(shape, lambda i: (0, 0))
    return pl.pallas_call(
        _mix_route_kernel,
        grid=(T // tm,),
        in_specs=[row_blk(FOX_W), row_blk(SWA_W), row_blk(D), row_blk(D), row_blk(D),
                  const((FOX_W, D)), const((SWA_W, D)), const((D, D)), const((1, D)),
                  const((D, 2 * LANES)), const((1, LANES))],
        out_specs=(row_blk(D), row_blk(D), row_blk(LANES),
                   pl.BlockSpec((ROUTE_ROWS, tm), lambda i: (0, i)), const((8, LANES))),
        out_shape=(jax.ShapeDtypeStruct((T, D), F32), jax.ShapeDtypeStruct((T, D), F32),
                   jax.ShapeDtypeStruct((T, LANES), F32), jax.ShapeDtypeStruct((ROUTE_ROWS, T), F32),
                   jax.ShapeDtypeStruct((8, LANES), F32)),
        scratch_shapes=[pltpu.VMEM((8, LANES), F32)],
        compiler_params=_cparams(("arbitrary",)),
        name="mix_route",
    )(o_f, o_s, g_f, g_s, x2, wpf, wps, wout, g_ffn, wr_cat, b_r)


def _plan_kernel(cnt_ref, route_t_ref, dest_ref, tbl_ref, blk_ref):
    lane = lax.broadcasted_iota(jnp.int32, cnt_ref.shape, 1)
    cnt = jnp.where(lane < N_EXPERTS, cnt_ref[...], 0.0)
    padded = jnp.floor((cnt + (ROW_BLOCK - 1)) * (1.0 / ROW_BLOCK)) * ROW_BLOCK
    pad_end = padded
    shift = 1
    while shift < N_EXPERTS:
        pad_end = pad_end + jnp.where(lane >= shift, pltpu.roll(pad_end, shift, axis=1), 0.0)
        shift *= 2
    pad_start = pad_end - padded
    row = lax.broadcasted_iota(jnp.int32, cnt_ref.shape, 0)
    tbl_ref[...] = jnp.where(row == 0, pad_start, pad_end).astype(jnp.int32)

    fields = route_t_ref[...]
    start_of = jnp.zeros_like(fields)
    for e in range(N_EXPERTS):
        start_of = jnp.where(fields == float(e), pad_start[0:1, e:e + 1], start_of)
    ranks = pltpu.roll(fields, ROUTE_ROWS - ROUTE_R0, axis=0)
    dest_ref[...] = (start_of + ranks).astype(jnp.int32)

    ends = pad_end.T[0:N_EXPERTS, 0:1]
    n_col = blk_ref.shape[1]
    blk_pos = (lax.broadcasted_iota(jnp.int32, (N_EXPERTS, n_col), 1) * ROW_BLOCK).astype(F32)
    owner = jnp.sum(jnp.where(ends <= blk_pos, 1.0, 0.0), axis=0, keepdims=True)
    blk_ref[...] = jnp.broadcast_to(jnp.minimum(owner, N_EXPERTS - 1.0), blk_ref.shape).astype(jnp.int32)


def _plan(counts, route_t, n_blk):
    T = route_t.shape[1]
    n_col = (n_blk + LANES - 1) // LANES * LANES
    full = lambda shape: pl.BlockSpec(shape, lambda i: (0, 0))
    return pl.pallas_call(
        _plan_kernel,
        grid=(1,),
        in_specs=[full((8, LANES)), full((ROUTE_ROWS, T))],
        out_specs=(full((ROUTE_ROWS, T)), full((8, LANES)), full((8, n_col))),
        out_shape=(jax.ShapeDtypeStruct((ROUTE_ROWS, T), jnp.int32),
                   jax.ShapeDtypeStruct((8, LANES), jnp.int32),
                   jax.ShapeDtypeStruct((8, n_col), jnp.int32)),
        compiler_params=_cparams(("arbitrary",)),
        name="plan",
    )(counts, route_t)


def _dispatch_kernel(bounds_ref, dest0_ref, dest1_ref, h_ref, xs_ref, zbuf, sem, zsem):
    step = pl.program_id(0)
    tb = h_ref.shape[0]

    @pl.when(step == 0)
    def _():
        zbuf[...] = jnp.zeros_like(zbuf)

        def tail_copy(e):
            tail = pl.multiple_of(bounds_ref[e + 1] - ROW_BLOCK, ROW_BLOCK)
            return pltpu.make_async_copy(zbuf, xs_ref.at[pl.ds(tail, ROW_BLOCK), :], zsem)

        def for_nonempty(action):
            def body(e, carry):
                @pl.when(bounds_ref[e + 1] > bounds_ref[e])
                def _():
                    action(tail_copy(e))
                return carry
            lax.fori_loop(0, N_EXPERTS, body, 0)

        n_rows = xs_ref.shape[0]
        first_unused = bounds_ref[N_EXPERTS] // ROW_BLOCK

        def unused_copy(blk):
            return pltpu.make_async_copy(
                zbuf, xs_ref.at[pl.ds(pl.multiple_of(blk * ROW_BLOCK, ROW_BLOCK), ROW_BLOCK), :], zsem)

        def for_unused(action):
            def body(blk, carry):
                action(unused_copy(blk))
                return carry
            lax.fori_loop(first_unused, n_rows // ROW_BLOCK, body, 0)

        for_nonempty(lambda cp: cp.start())
        for_unused(lambda cp: cp.start())
        for_nonempty(lambda cp: cp.wait())
        for_unused(lambda cp: cp.wait())

    def start(t, carry):
        for k, dest_ref in enumerate((dest0_ref, dest1_ref)):
            d = dest_ref[t]
            pltpu.make_async_copy(h_ref.at[pl.ds(t, 1), :], xs_ref.at[pl.ds(d, 1), :],
                                  sem).start(priority=k)
        return carry

    lax.fori_loop(0, tb, start, 0, unroll=DMA_UNROLL)
    for _ in range(2):
        pltpu.make_async_copy(h_ref, xs_ref.at[pl.ds(0, tb), :], sem).wait()


def _dispatch(bounds, dest0, dest1, h2, n_rows, *, tb):
    T, D = h2.shape
    smem_blk = pl.BlockSpec((tb,), lambda i, bd: (i,), memory_space=pltpu.SMEM)
    grid_spec = pltpu.PrefetchScalarGridSpec(
        num_scalar_prefetch=1,
        grid=(T // tb,),
        in_specs=[smem_blk, smem_blk, pl.BlockSpec((tb, D), lambda i, bd: (i, 0))],
        out_specs=pl.BlockSpec(memory_space=pl.ANY),
        scratch_shapes=[pltpu.VMEM((ROW_BLOCK, D), F32), pltpu.SemaphoreType.DMA(()),
                        pltpu.SemaphoreType.DMA(())],
    )
    return pl.pallas_call(
        _dispatch_kernel,
        grid_spec=grid_spec,
        out_shape=jax.ShapeDtypeStruct((n_rows, D), F32),
        compiler_params=_cparams(("arbitrary",)),
        name="dispatch",
    )(bounds, dest0, dest1, h2)


def _experts_kernel(be_ref, nu_ref, xs_ref, w1_ref, w3_ref, w2_ref, ys_ref, w1b, w3b, w2b):
    i = pl.program_id(0)
    e = be_ref[i]
    e_prev = be_ref[jnp.maximum(i - 1, 0)]
    used = i < nu_ref[0]

    @pl.when(used & ((i == 0) | (e != e_prev)))
    def _():
        w1b[...] = w1_ref[...].astype(BF16)
        w3b[...] = w3_ref[...].astype(BF16)
        w2b[...] = w2_ref[...].astype(BF16)

    @pl.when(used)
    def _():
        xb = xs_ref[...].astype(BF16)
        de = w1b.shape[1]
        n_part = 2
        part = de // n_part
        acts = []
        for q in range(n_part):
            cols = slice(q * part, (q + 1) * part)
            h1 = jnp.dot(xb, w1b[:, cols], preferred_element_type=F32)
            h3 = jnp.dot(xb, w3b[:, cols], preferred_element_type=F32)
            acts.append(((h1 * jax.nn.sigmoid(h1)) * h3).astype(BF16))
        y = jnp.dot(acts[0], w2b[0:part, :], preferred_element_type=F32)
        for q in range(1, n_part):
            y = y + jnp.dot(acts[q], w2b[q * part:(q + 1) * part, :], preferred_element_type=F32)
        ys_ref[...] = y

    @pl.when(jnp.logical_not(used))
    def _():
        ys_ref[...] = jnp.zeros_like(ys_ref)


def _experts(blk_expert, n_used, xs, w1, w3, w2):
    P, D = xs.shape
    de = w1.shape[-1]
    n_blk = P // ROW_BLOCK
    grid_spec = pltpu.PrefetchScalarGridSpec(
        num_scalar_prefetch=2,
        grid=(n_blk,),
        in_specs=[pl.BlockSpec((ROW_BLOCK, D),
                               lambda i, be, nu: (jnp.minimum(i, jnp.maximum(nu[0] - 1, 0)), 0)),
                  pl.BlockSpec((None, D, de), lambda i, be, nu: (be[i], 0, 0)),
                  pl.BlockSpec((None, D, de), lambda i, be, nu: (be[i], 0, 0)),
                  pl.BlockSpec((None, de, D), lambda i, be, nu: (be[i], 0, 0))],
        out_specs=pl.BlockSpec((ROW_BLOCK, D), lambda i, be, nu: (i, 0)),
        scratch_shapes=[pltpu.VMEM((D, de), BF16), pltpu.VMEM((D, de), BF16),
                        pltpu.VMEM((de, D), BF16)],
    )
    return pl.pallas_call(
        _experts_kernel,
        grid_spec=grid_spec,
        out_shape=jax.ShapeDtypeStruct((P, D), F32),
        compiler_params=_cparams(("arbitrary",)),
        name="experts",
    )(blk_expert, n_used, xs, w1, w3, w2)


def _combine_kernel(d0_cur, d1_cur, d0_next, d1_next, x1_ref, route_ref, gfin_ref, ys_ref, o_ref,
                    ybuf, sem):
    i = pl.program_id(0)
    tb = x1_ref.shape[0]
    slot = i % 2

    def issue(dest_refs, buf_slot):
        def start(t, carry):
            for k, dest_ref in enumerate(dest_refs):
                d = dest_ref[t]
                pltpu.make_async_copy(ys_ref.at[pl.ds(d, 1), :], ybuf.at[buf_slot, k, pl.ds(t, 1), :],
                                      sem.at[buf_slot]).start(priority=k)
            return carry
        lax.fori_loop(0, tb, start, 0, unroll=DMA_UNROLL)

    @pl.when(i == 0)
    def _():
        issue((d0_cur, d1_cur), 0)

    @pl.when(i + 1 < pl.num_programs(0))
    def _():
        issue((d0_next, d1_next), 1 - slot)

    for k in range(2):
        pltpu.make_async_copy(ys_ref.at[pl.ds(0, tb), :], ybuf.at[slot, k], sem.at[slot]).wait()

    route = route_ref[...]
    lane = lax.broadcasted_iota(jnp.int32, route.shape, 1)
    g0 = jnp.sum(jnp.where(lane == ROUTE_G0, route, 0.0), axis=1, keepdims=True)
    g1 = jnp.sum(jnp.where(lane == ROUTE_G1, route, 0.0), axis=1, keepdims=True)
    moe = g0 * ybuf[slot, 0] + g1 * ybuf[slot, 1]
    x2 = x1_ref[...] + moe
    ms = jnp.mean(x2 * x2, axis=-1, keepdims=True)
    o_ref[...] = (x2 * lax.rsqrt(ms + EPS)) * gfin_ref[...]


def _combine(dest0, dest1, x1, route, g_final, ys, *, tb):
    T, D = x1.shape
    nb = T // tb
    cur = pl.BlockSpec((tb,), lambda i: (i,), memory_space=pltpu.SMEM)
    nxt = pl.BlockSpec((tb,), lambda i: (jnp.minimum(i + 1, nb - 1),), memory_space=pltpu.SMEM)
    return pl.pallas_call(
        _combine_kernel,
        grid=(nb,),
        in_specs=[cur, cur, nxt, nxt,
                  pl.BlockSpec((tb, D), lambda i: (i, 0)),
                  pl.BlockSpec((tb, LANES), lambda i: (i, 0)),
                  pl.BlockSpec((1, D), lambda i: (0, 0)),
                  pl.BlockSpec(memory_space=pl.ANY)],
        out_specs=pl.BlockSpec((tb, D), lambda i: (i, 0)),
        out_shape=jax.ShapeDtypeStruct((T, D), F32),
        scratch_shapes=[pltpu.VMEM((2, 2, tb, D), F32), pltpu.SemaphoreType.DMA((2,))],
        compiler_params=_cparams(("arbitrary",)),
        name="combine",
    )(dest0, dest1, dest0, dest1, x1, route, g_final, ys)


def _pick(n, prefs):
    for p in prefs:
        if n % p == 0:
            return p
    return n


def _layer(x2, pos2, batch, seq, attn_norm, w_in, b_forget, b_gate, sinks, w_proj_fox, w_proj_swa,
           w_out, ffn_norm, w_group, b_group, w_expert, b_expert, w1, w3, w2, final_norm):
    T, D = x2.shape
    scale = HEAD_DIM ** -0.5

    sp = np.cumsum([FOX_W, FOX_W, FOX_W, FOX_HEADS, SWA_W, SWA_KV_W, SWA_KV_W, D, D])[:-1]
    wq_f, wk_f, wv_f, w_fl, wq_s, wk_s, wv_s, wg_f, wg_s = jnp.split(w_in, sp, axis=1)
    head_order = np.asarray(SWA_HEAD_ORDER)
    wq_s = wq_s.reshape(D, SWA_HEADS, HEAD_DIM)[:, head_order, :].reshape(D, SWA_W)
    pieces = (wq_f * (scale * LOG2E), wk_f, wv_f, wq_s * scale, wk_s, wv_s,
              wg_f, wg_s, jnp.pad(w_fl, ((0, 0), (0, LANES - FOX_HEADS))))
    weights = {name: piece.astype(BF16) for name, piece in zip(W_NAMES, pieces)}
    half = HEAD_DIM // 2
    inv_freq = ROPE_THETA ** (-jnp.arange(half, dtype=F32) * 2.0 / HEAD_DIM)
    invf = jnp.tile(inv_freq, LANES // half).reshape(1, LANES)
    bf_pad = jnp.pad(b_forget.astype(F32), (0, LANES - FOX_HEADS)).reshape(1, LANES)
    sinks_perm = sinks.astype(F32)[head_order]
    wps = w_proj_swa.reshape(SWA_HEADS, HEAD_DIM, D)[head_order].reshape(SWA_W, D).astype(BF16)
    wpf = w_proj_fox.astype(BF16)
    wout = w_out.astype(BF16)
    w_r = jnp.pad(jnp.concatenate([w_group, w_expert], axis=1).astype(F32),
                  ((0, 0), (0, LANES - N_GROUPS - N_EXPERTS)))
    wr_hi = w_r.astype(BF16)
    wr_lo = (w_r - wr_hi.astype(F32)).astype(BF16)
    wr_cat = jnp.concatenate([wr_hi, wr_lo], axis=1)
    b_r = jnp.pad(jnp.concatenate([b_group, b_expert]).astype(F32),
                  (0, LANES - N_GROUPS - N_EXPERTS)).reshape(1, LANES)

    tm1 = _pick(seq, (512, 256, 128))
    qt, kf, vt, qs, ks, vs, g_f, g_s, qxt, kx = _in_proj(
        x2, pos2, attn_norm.reshape(1, D), weights, invf, bf_pad, b_gate.astype(F32),
        batch=batch, seq=seq, tm=tm1)
    tq = _pick(seq, (FOX_TQ, 128))
    tk = min(tq, FOX_TK)
    o_f = _fox_attn(qt, qxt, kf, kx, vt, batch=batch, seq=seq, tq=tq, tk=tk,
                    unroll=min(FOX_UNROLL, tq // tk))
    o_s = _swa_attn(sinks_perm, qs, ks, vs, batch=batch, seq=seq, tq=_pick(seq, (512, 256, 128)))

    tm3 = _pick(T, (512, 256, 128))
    x1, h2, route, route_t, counts = _mix_route(o_f, o_s, g_f, g_s, x2, wpf, wps, wout,
                                       ffn_norm.reshape(1, D), wr_cat, b_r, tm=tm3)

    P = 2 * T + N_EXPERTS * ROW_BLOCK
    n_blk = P // ROW_BLOCK
    dest, tbl, blk = _plan(counts, route_t, n_blk)
    dest0, dest1 = dest[ROUTE_E0], dest[ROUTE_E1]
    pad_end = tbl[1, :N_EXPERTS]
    bounds = jnp.concatenate([jnp.zeros((1,), jnp.int32), pad_end])
    blk_expert = blk[0, :n_blk]
    n_used = pad_end[-1:] // ROW_BLOCK

    xs = _dispatch(bounds, dest0, dest1, h2, P, tb=_pick(T, (1024, 512, 256, 128)))
    ys = _experts(blk_expert, n_used, xs, w1, w3, w2)
    return _combine(dest0, dest1, x1, route, final_norm.reshape(1, D), ys,
                    tb=_pick(T, (512, 256, 128)))


def kernel(x, positions, attn_norm, w_in, b_forget, b_gate, attn_sinks, w_proj_fox, w_proj_swa,
           w_out, ffn_norm, w_group, b_group, w_expert, b_expert, w1, w3, w2, final_norm):
    B, S, D = x.shape
    depth = attn_norm.shape[0]
    x2 = x.reshape(B * S, D)
    pos2 = positions.reshape(B * S, 1).astype(jnp.int32)
    assert depth == 1, "a single layer is followed directly by the final norm"
    out = _layer(x2, pos2, B, S, attn_norm[0], w_in[0], b_forget[0], b_gate[0], attn_sinks[0],
                 w_proj_fox[0], w_proj_swa[0], w_out[0], ffn_norm[0], w_group[0], b_group[0],
                 w_expert[0], b_expert[0], w1[0], w3[0], w2[0], final_norm)
    return out.reshape(B, S, D)
```

```python
import functools

import numpy as np
import jax
import jax.numpy as jnp
from jax import lax
from jax.experimental import pallas as pl
from jax.experimental.pallas import tpu as pltpu

HEAD_DIM = 64
FOX_HEADS = 8
SWA_HEADS = 8
SWA_KV_HEADS = 2
WINDOW = 128
ROPE_THETA = 10000.0
N_GROUPS = 8
EXPERTS_PER_GROUP = 8
N_EXPERTS = N_GROUPS * EXPERTS_PER_GROUP
ROW_BLOCK = 512
SUBLANES = 8
EPS = 1e-6
LANES = 128
NEG_BIG = -1e30
LOG2E = 1.4426950408889634
BIAS_TERMS = 3
BIAS_LANES = 2 * BIAS_TERMS
FOX_TQ = 1024
FOX_TK = 256
FOX_UNROLL = 4

FOX_W = FOX_HEADS * HEAD_DIM
SWA_W = SWA_HEADS * HEAD_DIM
SWA_KV_W = SWA_KV_HEADS * HEAD_DIM
SWA_HEAD_ORDER = (0, 4, 1, 5, 2, 6, 3, 7)

F32 = jnp.float32
BF16 = jnp.bfloat16
VMEM_LIMIT = 56 * 1024 * 1024


def _cparams(sem):
    return pltpu.CompilerParams(dimension_semantics=sem, vmem_limit_bytes=VMEM_LIMIT)


def _split3(x):
    hi = x.astype(BF16)
    r1 = x - hi.astype(F32)
    mid = r1.astype(BF16)
    lo = (r1 - mid.astype(F32)).astype(BF16)
    return hi, mid, lo


W_NAMES = ("q_fox", "k_fox", "v_fox", "q_swa", "k_swa", "v_swa", "gate_fox", "gate_swa", "forget")


def _in_proj_kernel(x_ref, pos_ref, g_ref, invf_ref, bf_ref, bg_ref, place_ref, ones_ref, *refs):
    w = dict(zip(W_NAMES, refs[:len(W_NAMES)]))
    (qf_ref, kf_ref, vf_ref, qs_ref, ks_ref, vs_ref, gf_ref, gs_ref, qx_ref, kx_ref,
     carry_ref) = refs[len(W_NAMES):]
    j = pl.program_id(1)
    tm = x_ref.shape[0]

    @pl.when(j == 0)
    def _():
        carry_ref[...] = jnp.zeros_like(carry_ref)

    x = x_ref[...]
    ms = jnp.mean(x * x, axis=-1, keepdims=True)
    h = ((x * lax.rsqrt(ms + EPS)) * g_ref[...]).astype(BF16)

    def mm(name):
        return jnp.dot(h, w[name][...], preferred_element_type=F32)

    def store_pairs_t(ref, y):
        for pr in range(ref.shape[0]):
            ref[pr] = y[:, pr * LANES:(pr + 1) * LANES].T.astype(BF16)

    store_pairs_t(qf_ref, mm("q_fox"))
    kf_ref[...] = mm("k_fox").astype(BF16)
    store_pairs_t(vf_ref, mm("v_fox"))

    ang = pos_ref[...].astype(F32) * invf_ref[...]
    cos = jnp.cos(ang)
    sin = jnp.sin(ang)
    n_rep = SWA_W // LANES
    cos_q = jnp.concatenate([cos] * n_rep, axis=1)
    sin_q = jnp.concatenate([sin] * n_rep, axis=1)
    def rot_half(y):
        width = y.shape[1]
        lane_y = lax.broadcasted_iota(jnp.int32, y.shape, 1)
        first = (lane_y & (HEAD_DIM - 1)) < HEAD_DIM // 2
        return jnp.where(first, -pltpu.roll(y, width - HEAD_DIM // 2, axis=1),
                         pltpu.roll(y, HEAD_DIM // 2, axis=1))

    q_s = mm("q_swa")
    k_s = mm("k_swa")
    qs_ref[...] = (q_s * cos_q + rot_half(q_s) * sin_q).astype(BF16)
    ks_ref[...] = (k_s * cos + rot_half(k_s) * sin).astype(BF16)
    vs_ref[...] = mm("v_swa").astype(BF16)

    gf_ref[...] = jax.nn.sigmoid(mm("gate_fox") + bg_ref[0:1, :]).astype(BF16)
    gs_ref[...] = jax.nn.sigmoid(mm("gate_swa") + bg_ref[1:2, :]).astype(BF16)

    z = mm("forget") + bf_ref[...]
    lane = lax.broadcasted_iota(jnp.int32, (tm, LANES), 1)
    lf = jnp.minimum(z, 0.0) - jnp.log1p(jnp.exp(-jnp.abs(z)))
    lf = jnp.where(lane < FOX_HEADS, lf, 0.0)
    row = lax.broadcasted_iota(jnp.int32, (tm, tm), 0)
    col = lax.broadcasted_iota(jnp.int32, (tm, tm), 1)
    tri = jnp.where(row >= col, 1.0, 0.0).astype(BF16)
    hi, mid, lo = _split3(lf)
    c = (jnp.dot(tri, hi, preferred_element_type=F32)
         + jnp.dot(tri, mid, preferred_element_type=F32)
         + jnp.dot(tri, lo, preferred_element_type=F32)) + carry_ref[0:1, :]
    carry_ref[0:1, :] = c[tm - 1:tm, :]
    chi, cmid, clo = _split3(c * LOG2E)
    ext = jnp.dot(jnp.concatenate([chi, cmid, clo], axis=1), place_ref[...],
                  preferred_element_type=F32) + ones_ref[...]
    qx_ref[...] = ext[:, :LANES].T.astype(BF16)
    kx_ref[...] = ext[:, LANES:].astype(BF16)


def _bias_placement():
    place = np.zeros((BIAS_TERMS * LANES, 2 * LANES), np.float32)
    ones = np.zeros((1, 2 * LANES), np.float32)
    for h in range(FOX_HEADS):
        for t in range(BIAS_TERMS):
            place[t * LANES + h, BIAS_LANES * h + t] = 1.0
            place[t * LANES + h, LANES + BIAS_LANES * h + BIAS_TERMS + t] = -1.0
            ones[0, BIAS_LANES * h + BIAS_TERMS + t] = 1.0
            ones[0, LANES + BIAS_LANES * h + t] = 1.0
    return jnp.asarray(place, BF16), jnp.asarray(ones, F32)


def _in_proj(x2, pos2, g_attn, weights, invf, bf_pad, b_gate, *, batch, seq, tm):
    T, D = x2.shape
    nj = seq // tm
    w_list = [weights[name] for name in W_NAMES]
    place, ones = _bias_placement()
    row_blk = lambda w: pl.BlockSpec((tm, w), lambda b, j: (b * nj + j, 0))
    const = lambda shape: pl.BlockSpec(shape, lambda b, j: (0, 0))
    n_pair = FOX_HEADS // 2
    pair_t = pl.BlockSpec((None, n_pair, LANES, tm), lambda b, j: (b, 0, 0, j))
    out_shapes = (
        jax.ShapeDtypeStruct((batch, n_pair, LANES, seq), BF16), jax.ShapeDtypeStruct((T, FOX_W), BF16),
        jax.ShapeDtypeStruct((batch, n_pair, LANES, seq), BF16), jax.ShapeDtypeStruct((T, SWA_W), BF16),
        jax.ShapeDtypeStruct((T, SWA_KV_W), BF16), jax.ShapeDtypeStruct((T, SWA_KV_W), BF16),
        jax.ShapeDtypeStruct((T, D), BF16), jax.ShapeDtypeStruct((T, D), BF16),
        jax.ShapeDtypeStruct((batch, LANES, seq), BF16), jax.ShapeDtypeStruct((T, LANES), BF16),
    )
    return pl.pallas_call(
        _in_proj_kernel,
        grid=(batch, nj),
        in_specs=[row_blk(D), row_blk(1), const((1, D)), const((1, LANES)),
                  const((1, LANES)), const((2, D)), const(place.shape), const(ones.shape)]
                 + [const(wt.shape) for wt in w_list],
        out_specs=(pair_t, row_blk(FOX_W), pair_t, row_blk(SWA_W),
                   row_blk(SWA_KV_W), row_blk(SWA_KV_W), row_blk(D), row_blk(D),
                   pl.BlockSpec((None, LANES, tm), lambda b, j: (b, 0, j)), row_blk(LANES)),
        out_shape=out_shapes,
        scratch_shapes=[pltpu.VMEM((8, LANES), F32)],
        compiler_params=_cparams(("arbitrary", "arbitrary")),
        name="in_proj",
    )(x2, pos2, g_attn, invf, bf_pad, b_gate, place, ones, *w_list)


def _fox_kernel(q_ref, qx_ref, k_ref, kx_ref, vt_ref, o_ref, acc_ref, *, tk, unroll):
    p = pl.program_id(1)
    i = pl.program_id(2)
    tq = q_ref.shape[1]
    n_diag = tq // tk
    qt = q_ref[...]
    qxt = qx_ref[...]
    row = lax.broadcasted_iota(jnp.int32, (LANES, tq), 0)
    zero = jnp.zeros_like(qt)
    q_ext = []
    for hh in range(2):
        first = BIAS_LANES * (2 * p + hh)
        head_rows = (row < HEAD_DIM) if hh == 0 else (row >= HEAD_DIM)
        bias_rows = (row >= first) & (row < first + BIAS_LANES)
        q_ext.append(jnp.concatenate([jnp.where(head_rows, qt, zero),
                                      jnp.where(bias_rows, qxt, zero)], axis=0))

    acc_ref[...] = jnp.zeros_like(acc_ref)

    def scores(ks, c0, masked):
        k_ext = jnp.concatenate([k_ref[pl.ds(ks, tk), :], kx_ref[pl.ds(ks, tk), :]], axis=1)
        out = []
        for hh in range(2):
            s = jnp.dot(k_ext, q_ext[hh][:, c0:], preferred_element_type=F32)
            if masked:
                key = lax.broadcasted_iota(jnp.int32, s.shape, 0)
                qry = lax.broadcasted_iota(jnp.int32, s.shape, 1)
                s = jnp.where(key <= qry, s, -jnp.inf)
            out.append(s)
        return out

    def accumulate(ks, c0, s_pair, stats):
        vt = vt_ref[:, pl.ds(ks, tk)]
        out = []
        for hh in range(2):
            m_all, l_all = stats[hh]
            m_old, l_old = m_all[:, c0:], l_all[:, c0:]
            s = s_pair[hh]
            m_new = jnp.maximum(m_old, jnp.max(s, axis=0, keepdims=True))
            alpha = jnp.exp2(m_old - m_new)
            pm = jnp.exp2(s - m_new)
            l_new = alpha * l_old + jnp.sum(pm, axis=0, keepdims=True)
            acc_ref[hh, :, c0:] = alpha * acc_ref[hh, :, c0:] + jnp.dot(
                vt, pm.astype(BF16), preferred_element_type=F32)
            if c0:
                m_new = jnp.concatenate([m_all[:, :c0], m_new], axis=1)
                l_new = jnp.concatenate([l_all[:, :c0], l_new], axis=1)
            out.append((m_new, l_new))
        return tuple(out)

    def run_tiles(first_tile, count, stats, diagonal):
        start = lambda u: pl.multiple_of((first_tile + u) * tk, tk)
        col0 = lambda u: u * tk if diagonal else 0
        s_next = scores(start(0), col0(0), diagonal)
        for u in range(count):
            s_cur = s_next
            if u + 1 < count:
                s_next = scores(start(u + 1), col0(u + 1), diagonal)
            stats = accumulate(start(u), col0(u), s_cur, stats)
        return stats

    init = tuple((jnp.full((1, tq), NEG_BIG, F32), jnp.zeros((1, tq), F32)) for _ in range(2))
    stats = lax.fori_loop(0, i * (n_diag // unroll),
                          lambda jb, st: run_tiles(jb * unroll, unroll, st, False), init)
    stats = run_tiles(i * n_diag, n_diag, stats, True)

    o_t = jnp.where(row < HEAD_DIM, acc_ref[0] * (1.0 / stats[0][1]),
                    acc_ref[1] * (1.0 / stats[1][1]))
    o_ref[...] = o_t.T.astype(BF16)


def _fox_attn(qt, qxt, kf, kx, vt, *, batch, seq, tq, tk, unroll):
    T = kf.shape[0]
    nq = seq // tq
    n_pair = FOX_HEADS // 2
    return pl.pallas_call(
        functools.partial(_fox_kernel, tk=tk, unroll=unroll),
        grid=(batch, n_pair, nq),
        in_specs=[
            pl.BlockSpec((None, None, LANES, tq), lambda b, p, i: (b, p, 0, i)),
            pl.BlockSpec((None, LANES, tq), lambda b, p, i: (b, 0, i)),
            pl.BlockSpec((seq, LANES), lambda b, p, i: (b, p)),
            pl.BlockSpec((seq, LANES), lambda b, p, i: (b, 0)),
            pl.BlockSpec((None, None, LANES, seq), lambda b, p, i: (b, p, 0, 0)),
        ],
        out_specs=pl.BlockSpec((tq, LANES), lambda b, p, i: (b * nq + i, p)),
        out_shape=jax.ShapeDtypeStruct((T, FOX_W), BF16),
        scratch_shapes=[pltpu.VMEM((2, LANES, tq), F32)],
        compiler_params=_cparams(("arbitrary", "arbitrary", "arbitrary")),
        name="fox_attn",
    )(qt, qxt, kf, kx, vt)


def _swa_kernel(sink_ref, q_ref, kc_ref, kp_ref, vc_ref, vp_ref, o_ref):
    i = pl.program_id(1)
    tq = q_ref.shape[0]
    n_win = tq // WINDOW
    n_col = SWA_W // LANES
    lane = lax.broadcasted_iota(jnp.int32, (WINDOW, LANES), 1)
    low = lane < HEAD_DIM
    r = lax.broadcasted_iota(jnp.int32, (WINDOW, 2 * WINDOW), 0)
    c = lax.broadcasted_iota(jnp.int32, (WINDOW, 2 * WINDOW), 1)
    dist = WINDOW + r - c
    band = (dist >= 0) & (dist < WINDOW)
    for w in range(n_win):
        if w == 0:
            k_win = jnp.concatenate([kp_ref[...], kc_ref[0:WINDOW, :]], axis=0)
            v_win = jnp.concatenate([vp_ref[...], vc_ref[0:WINDOW, :]], axis=0)
            first = i == 0
            valid = band & ((c >= WINDOW) | jnp.logical_not(first))
        else:
            k_win = kc_ref[(w - 1) * WINDOW:(w + 1) * WINDOW, :]
            v_win = vc_ref[(w - 1) * WINDOW:(w + 1) * WINDOW, :]
            valid = band
        for col_i in range(n_col):
            qc = q_ref[w * WINDOW:(w + 1) * WINDOW, col_i * LANES:(col_i + 1) * LANES]
            outs = []
            for half in range(2):
                qm = jnp.where(low if half == 0 else jnp.logical_not(low), qc, jnp.zeros_like(qc))
                s = lax.dot_general(qm, k_win, (((1,), (1,)), ((), ())),
                                    preferred_element_type=F32)
                s = jnp.where(valid, s, -jnp.inf)
                sink = sink_ref[2 * col_i + half]
                m = jnp.maximum(jnp.max(s, axis=1, keepdims=True), sink)
                pm = jnp.exp(s - m)
                denom = jnp.sum(pm, axis=1, keepdims=True) + jnp.exp(sink - m)
                pm = pm / denom
                outs.append(jnp.dot(pm.astype(BF16), v_win, preferred_element_type=F32))
            o_ref[w * WINDOW:(w + 1) * WINDOW, col_i * LANES:(col_i + 1) * LANES] = (
                jnp.where(low, outs[0], outs[1]).astype(BF16))


def _swa_attn(sinks_perm, qs, ks, vs, *, batch, seq, tq):
    T = qs.shape[0]
    nq = seq // tq
    n_win = tq // WINDOW
    nb = seq // WINDOW
    cur = lambda w: pl.BlockSpec((tq, w), lambda b, i: (b * nq + i, 0))
    prev = pl.BlockSpec((WINDOW, SWA_KV_W),
                        lambda b, i: (jnp.maximum(b * nb + i * n_win - 1, 0), 0))
    return pl.pallas_call(
        _swa_kernel,
        grid=(batch, nq),
        in_specs=[pl.BlockSpec(memory_space=pltpu.SMEM), cur(SWA_W), cur(SWA_KV_W), prev,
                  cur(SWA_KV_W), prev],
        out_specs=cur(SWA_W),
        out_shape=jax.ShapeDtypeStruct((T, SWA_W), BF16),
        compiler_params=_cparams(("arbitrary", "arbitrary")),
        name="swa_attn",
    )(sinks_perm, qs, ks, ks, vs, vs)


ROUTE_E0, ROUTE_E1, ROUTE_G0, ROUTE_G1, ROUTE_R0, ROUTE_R1 = range(6)
ROUTE_ROWS = 8


def _mix_route_kernel(of_ref, os_ref, gf_ref, gs_ref, x_ref, wpf_ref, wps_ref, wout_ref, gn_ref,
                      wr_ref, br_ref, x1_ref, h2_ref, route_ref, route_t_ref, cnt_ref, carry_ref):
    step = pl.program_id(0)
    tm = x_ref.shape[0]

    @pl.when(step == 0)
    def _():
        carry_ref[...] = jnp.zeros_like(carry_ref)

    a = jnp.dot(of_ref[...], wpf_ref[...], preferred_element_type=F32)
    b = jnp.dot(os_ref[...], wps_ref[...], preferred_element_type=F32)
    merged = gf_ref[...].astype(F32) * a + gs_ref[...].astype(F32) * b
    y = jnp.dot(merged.astype(BF16), wout_ref[...], preferred_element_type=F32)
    x1 = x_ref[...] + y
    x1_ref[...] = x1
    ms = jnp.mean(x1 * x1, axis=-1, keepdims=True)
    h2 = (x1 * lax.rsqrt(ms + EPS)) * gn_ref[...]
    h2_ref[...] = h2

    h_hi = h2.astype(BF16)
    h_lo = (h2 - h_hi.astype(F32)).astype(BF16)
    wr = wr_ref[...]
    r_hi = jnp.dot(h_hi, wr, preferred_element_type=F32)
    logits = (r_hi[:, :LANES] + r_hi[:, LANES:]
              + jnp.dot(h_lo, wr[:, :LANES], preferred_element_type=F32)) + br_ref[...]

    lane = lax.broadcasted_iota(jnp.int32, (tm, LANES), 1)
    is_g = lane < N_GROUPS
    gl = jnp.where(is_g, logits, -jnp.inf)
    gmax = jnp.max(gl, axis=1, keepdims=True)
    gexp = jnp.where(is_g, jnp.exp(gl - gmax), 0.0)
    gprob = gexp / jnp.sum(gexp, axis=1, keepdims=True)
    g_w = jnp.max(gprob, axis=1, keepdims=True)
    g_idx = jnp.min(jnp.where(is_g & (gprob == g_w), lane, LANES), axis=1, keepdims=True)

    e_lane = lane - N_GROUPS
    in_grp = (e_lane >= 0) & (e_lane < N_EXPERTS) & ((e_lane >> 3) == g_idx)
    el = jnp.where(in_grp, logits, -jnp.inf)
    v0 = jnp.max(el, axis=1, keepdims=True)
    i0 = jnp.min(jnp.where(in_grp & (el == v0), lane, LANES), axis=1, keepdims=True)
    el1 = jnp.where(lane == i0, -jnp.inf, el)
    v1 = jnp.max(el1, axis=1, keepdims=True)
    i1 = jnp.min(jnp.where(in_grp & (lane != i0) & (el1 == v1), lane, LANES), axis=1, keepdims=True)
    t = jnp.exp(v1 - v0)
    den = 1.0 + t
    gate0 = g_w * (1.0 / den)
    gate1 = g_w * (t / den)
    e0 = i0 - N_GROUPS
    e1 = i1 - N_GROUPS

    oh0 = jnp.where(lane == e0, 1.0, 0.0)
    oh1 = jnp.where(lane == e1, 1.0, 0.0)
    row = lax.broadcasted_iota(jnp.int32, (tm, tm), 0)
    col = lax.broadcasted_iota(jnp.int32, (tm, tm), 1)
    tri = jnp.where(row > col, 1.0, 0.0).astype(BF16)
    cs0 = jnp.sum(oh0, axis=0, keepdims=True)
    cs1 = jnp.sum(oh1, axis=0, keepdims=True)
    carry = carry_ref[0:1, :]
    pre = jnp.dot(tri, jnp.concatenate([oh0, oh1], axis=1).astype(BF16),
                  preferred_element_type=F32)
    pre0 = pre[:, :LANES] + carry
    pre1 = pre[:, LANES:] + (carry + cs0)
    rank0 = jnp.sum(oh0 * pre0, axis=1, keepdims=True)
    rank1 = jnp.sum(oh1 * pre1, axis=1, keepdims=True)
    new_carry = carry + cs0 + cs1
    carry_ref[0:1, :] = new_carry
    cnt_ref[...] = jnp.broadcast_to(new_carry, cnt_ref.shape)

    route = jnp.where(lane == ROUTE_E0, e0.astype(F32), 0.0)
    route = jnp.where(lane == ROUTE_E1, e1.astype(F32), route)
    route = jnp.where(lane == ROUTE_G0, gate0, route)
    route = jnp.where(lane == ROUTE_G1, gate1, route)
    route = jnp.where(lane == ROUTE_R0, rank0, route)
    route = jnp.where(lane == ROUTE_R1, rank1, route)
    route_ref[...] = route
    route_t_ref[...] = route.T[0:route_t_ref.shape[0], :]


def _mix_route(o_f, o_s, g_f, g_s, x2, wpf, wps, wout, g_ffn, wr_cat, b_r, *, tm):
    T, D = x2.shape
    row_blk = lambda w: pl.BlockSpec((tm, w), lambda i: (i, 0))
    const = lambda shape: pl.BlockSpec(shape, lambda i: (0, 0))
    return pl.pallas_call(
        _mix_route_kernel,
        grid=(T // tm,),
        in_specs=[row_blk(FOX_W), row_blk(SWA_W), row_blk(D), row_blk(D), row_blk(D),
                  const((FOX_W, D)), const((SWA_W, D)), const((D, D)), const((1, D)),
                  const((D, 2 * LANES)), const((1, LANES))],
        out_specs=(row_blk(D), row_blk(D), row_blk(LANES),
                   pl.BlockSpec((ROUTE_ROWS, tm), lambda i: (0, i)), const((8, LANES))),
        out_shape=(jax.ShapeDtypeStruct((T, D), F32), jax.ShapeDtypeStruct((T, D), F32),
                   jax.ShapeDtypeStruct((T, LANES), F32), jax.ShapeDtypeStruct((ROUTE_ROWS, T), F32),
                   jax.ShapeDtypeStruct((8, LANES), F32)),
        scratch_shapes=[pltpu.VMEM((8, LANES), F32)],
        compiler_params=_cparams(("arbitrary",)),
        name="mix_route",
    )(o_f, o_s, g_f, g_s, x2, wpf, wps, wout, g_ffn, wr_cat, b_r)


def _plan_kernel(cnt_ref, route_t_ref, dest_ref, tbl_ref, blk_ref):
    lane = lax.broadcasted_iota(jnp.int32, cnt_ref.shape, 1)
    cnt = jnp.where(lane < N_EXPERTS, cnt_ref[...], 0.0)
    padded = jnp.floor((cnt + (ROW_BLOCK - 1)) * (1.0 / ROW_BLOCK)) * ROW_BLOCK
    pad_end = padded
    shift = 1
    while shift < N_EXPERTS:
        pad_end = pad_end + jnp.where(lane >= shift, pltpu.roll(pad_end, shift, axis=1), 0.0)
        shift *= 2
    pad_start = pad_end - padded
    row = lax.broadcasted_iota(jnp.int32, cnt_ref.shape, 0)
    tbl_ref[...] = jnp.where(row == 0, pad_start, pad_end).astype(jnp.int32)

    fields = route_t_ref[...]
    start_of = jnp.zeros_like(fields)
    for e in range(N_EXPERTS):
        start_of = jnp.where(fields == float(e), pad_start[0:1, e:e + 1], start_of)
    ranks = pltpu.roll(fields, ROUTE_ROWS - ROUTE_R0, axis=0)
    dest_ref[...] = (start_of + ranks).astype(jnp.int32)

    ends = pad_end.T[0:N_EXPERTS, 0:1]
    n_col = blk_ref.shape[1]
    blk_pos = (lax.broadcasted_iota(jnp.int32, (N_EXPERTS, n_col), 1) * ROW_BLOCK).astype(F32)
    owner = jnp.sum(jnp.where(ends <= blk_pos, 1.0, 0.0), axis=0, keepdims=True)
    blk_ref[...] = jnp.broadcast_to(jnp.minimum(owner, N_EXPERTS - 1.0), blk_ref.shape).astype(jnp.int32)


def _plan(counts, route_t, n_blk):
    T = route_t.shape[1]
    n_col = (n_blk + LANES - 1) // LANES * LANES
    full = lambda shape: pl.BlockSpec(shape, lambda i: (0, 0))
    return pl.pallas_call(
        _plan_kernel,
        grid=(1,),
        in_specs=[full((8, LANES)), full((ROUTE_ROWS, T))],
        out_specs=(full((ROUTE_ROWS, T)), full((8, LANES)), full((8, n_col))),
        out_shape=(jax.ShapeDtypeStruct((ROUTE_ROWS, T), jnp.int32),
                   jax.ShapeDtypeStruct((8, LANES), jnp.int32),
                   jax.ShapeDtypeStruct((8, n_col), jnp.int32)),
        compiler_params=_cparams(("arbitrary",)),
        name="plan",
    )(counts, route_t)


def _dispatch_kernel(bounds_ref, dest0_ref, dest1_ref, h_ref, xs_ref, zbuf, sem, zsem):
    step = pl.program_id(0)
    tb = h_ref.shape[0]

    @pl.when(step == 0)
    def _():
        zbuf[...] = jnp.zeros_like(zbuf)

        def tail_copy(e):
            tail = pl.multiple_of(bounds_ref[e + 1] - ROW_BLOCK, ROW_BLOCK)
            return pltpu.make_async_copy(zbuf, xs_ref.at[pl.ds(tail, ROW_BLOCK), :], zsem)

        def for_nonempty(action):
            def body(e, carry):
                @pl.when(bounds_ref[e + 1] > bounds_ref[e])
                def _():
                    action(tail_copy(e))
                return carry
            lax.fori_loop(0, N_EXPERTS, body, 0)

        n_rows = xs_ref.shape[0]
        first_unused = bounds_ref[N_EXPERTS] // ROW_BLOCK

        def unused_copy(blk):
            return pltpu.make_async_copy(
                zbuf, xs_ref.at[pl.ds(pl.multiple_of(blk * ROW_BLOCK, ROW_BLOCK), ROW_BLOCK), :], zsem)

        def for_unused(action):
            def body(blk, carry):
                action(unused_copy(blk))
                return carry
            lax.fori_loop(first_unused, n_rows // ROW_BLOCK, body, 0)

        for_nonempty(lambda cp: cp.start())
        for_unused(lambda cp: cp.start())
        for_nonempty(lambda cp: cp.wait())
        for_unused(lambda cp: cp.wait())

    def start(j, carry):
        base = pl.multiple_of(j * SUBLANES, SUBLANES)
        tile = h_ref.at[pl.ds(base, SUBLANES), :]
        for u in range(SUBLANES):
            for k, dest_ref in enumerate((dest0_ref, dest1_ref)):
                d = dest_ref[base + u]
                pltpu.make_async_copy(tile.at[pl.ds(u, 1), :], xs_ref.at[pl.ds(d, 1), :],
                                      sem).start(priority=k)
        return carry

    lax.fori_loop(0, tb // SUBLANES, start, 0)
    for _ in range(2):
        pltpu.make_async_copy(h_ref, xs_ref.at[pl.ds(0, tb), :], sem).wait()


def _dispatch(bounds, dest0, dest1, h2, n_rows, *, tb):
    T, D = h2.shape
    smem_blk = pl.BlockSpec((tb,), lambda i, bd: (i,), memory_space=pltpu.SMEM)
    grid_spec = pltpu.PrefetchScalarGridSpec(
        num_scalar_prefetch=1,
        grid=(T // tb,),
        in_specs=[smem_blk, smem_blk, pl.BlockSpec((tb, D), lambda i, bd: (i, 0))],
        out_specs=pl.BlockSpec(memory_space=pl.ANY),
        scratch_shapes=[pltpu.VMEM((ROW_BLOCK, D), F32), pltpu.SemaphoreType.DMA(()),
                        pltpu.SemaphoreType.DMA(())],
    )
    return pl.pallas_call(
        _dispatch_kernel,
        grid_spec=grid_spec,
        out_shape=jax.ShapeDtypeStruct((n_rows, D), F32),
        compiler_params=_cparams(("arbitrary",)),
        name="dispatch",
    )(bounds, dest0, dest1, h2)


def _experts_kernel(be_ref, nu_ref, xs_ref, w1_ref, w3_ref, w2_ref, ys_ref, w1b, w3b, w2b):
    i = pl.program_id(0)
    e = be_ref[i]
    e_prev = be_ref[jnp.maximum(i - 1, 0)]
    used = i < nu_ref[0]

    @pl.when(used & ((i == 0) | (e != e_prev)))
    def _():
        w1b[...] = w1_ref[...].astype(BF16)
        w3b[...] = w3_ref[...].astype(BF16)
        w2b[...] = w2_ref[...].astype(BF16)

    @pl.when(used)
    def _():
        xb = xs_ref[...].astype(BF16)
        de = w1b.shape[1]
        n_part = 2
        part = de // n_part
        acts = []
        for q in range(n_part):
            cols = slice(q * part, (q + 1) * part)
            h1 = jnp.dot(xb, w1b[:, cols], preferred_element_type=F32)
            h3 = jnp.dot(xb, w3b[:, cols], preferred_element_type=F32)
            acts.append(((h1 * jax.nn.sigmoid(h1)) * h3).astype(BF16))
        y = jnp.dot(acts[0], w2b[0:part, :], preferred_element_type=F32)
        for q in range(1, n_part):
            y = y + jnp.dot(acts[q], w2b[q * part:(q + 1) * part, :], preferred_element_type=F32)
        ys_ref[...] = y

    @pl.when(jnp.logical_not(used))
    def _():
        ys_ref[...] = jnp.zeros_like(ys_ref)


def _experts(blk_expert, n_used, xs, w1, w3, w2):
    P, D = xs.shape
    de = w1.shape[-1]
    n_blk = P // ROW_BLOCK
    grid_spec = pltpu.PrefetchScalarGridSpec(
        num_scalar_prefetch=2,
        grid=(n_blk,),
        in_specs=[pl.BlockSpec((ROW_BLOCK, D),
                               lambda i, be, nu: (jnp.minimum(i, jnp.maximum(nu[0] - 1, 0)), 0)),
                  pl.BlockSpec((None, D, de), lambda i, be, nu: (be[i], 0, 0)),
                  pl.BlockSpec((None, D, de), lambda i, be, nu: (be[i], 0, 0)),
                  pl.BlockSpec((None, de, D), lambda i, be, nu: (be[i], 0, 0))],
        out_specs=pl.BlockSpec((ROW_BLOCK, D), lambda i, be, nu: (i, 0)),
        scratch_shapes=[pltpu.VMEM((D, de), BF16), pltpu.VMEM((D, de), BF16),
                        pltpu.VMEM((de, D), BF16)],
    )
    return pl.pallas_call(
        _experts_kernel,
        grid_spec=grid_spec,
        out_shape=jax.ShapeDtypeStruct((P, D), F32),
        compiler_params=_cparams(("arbitrary",)),
        name="experts",
    )(blk_expert, n_used, xs, w1, w3, w2)


def _combine_kernel(d0_cur, d1_cur, d0_next, d1_next, x1_ref, route_ref, gfin_ref, ys_ref, o_ref,
                    ybuf, sem):
    i = pl.program_id(0)
    tb = x1_ref.shape[0]
    slot = i % 2

    def issue(dest_refs, buf_slot):
        def start(j, carry):
            base = pl.multiple_of(j * SUBLANES, SUBLANES)
            for k, dest_ref in enumerate(dest_refs):
                tile = ybuf.at[buf_slot, k, pl.ds(base, SUBLANES), :]
                for u in range(SUBLANES):
                    d = dest_ref[base + u]
                    pltpu.make_async_copy(ys_ref.at[pl.ds(d, 1), :], tile.at[pl.ds(u, 1), :],
                                          sem.at[buf_slot]).start(priority=k)
            return carry
        lax.fori_loop(0, tb // SUBLANES, start, 0)

    @pl.when(i == 0)
    def _():
        issue((d0_cur, d1_cur), 0)

    @pl.when(i + 1 < pl.num_programs(0))
    def _():
        issue((d0_next, d1_next), 1 - slot)

    for k in range(2):
        pltpu.make_async_copy(ys_ref.at[pl.ds(0, tb), :], ybuf.at[slot, k], sem.at[slot]).wait()

    route = route_ref[...]
    lane = lax.broadcasted_iota(jnp.int32, route.shape, 1)
    g0 = jnp.sum(jnp.where(lane == ROUTE_G0, route, 0.0), axis=1, keepdims=True)
    g1 = jnp.sum(jnp.where(lane == ROUTE_G1, route, 0.0), axis=1, keepdims=True)
    moe = g0 * ybuf[slot, 0] + g1 * ybuf[slot, 1]
    x2 = x1_ref[...] + moe
    ms = jnp.mean(x2 * x2, axis=-1, keepdims=True)
    o_ref[...] = (x2 * lax.rsqrt(ms + EPS)) * gfin_ref[...]


def _combine(dest0, dest1, x1, route, g_final, ys, *, tb):
    T, D = x1.shape
    nb = T // tb
    cur = pl.BlockSpec((tb,), lambda i: (i,), memory_space=pltpu.SMEM)
    nxt = pl.BlockSpec((tb,), lambda i: (jnp.minimum(i + 1, nb - 1),), memory_space=pltpu.SMEM)
    return pl.pallas_call(
        _combine_kernel,
        grid=(nb,),
        in_specs=[cur, cur, nxt, nxt,
                  pl.BlockSpec((tb, D), lambda i: (i, 0)),
                  pl.BlockSpec((tb, LANES), lambda i: (i, 0)),
                  pl.BlockSpec((1, D), lambda i: (0, 0)),
                  pl.BlockSpec(memory_space=pl.ANY)],
        out_specs=pl.BlockSpec((tb, D), lambda i: (i, 0)),
        out_shape=jax.ShapeDtypeStruct((T, D), F32),
        scratch_shapes=[pltpu.VMEM((2, 2, tb, D), F32), pltpu.SemaphoreType.DMA((2,))],
        compiler_params=_cparams(("arbitrary",)),
        name="combine",
    )(dest0, dest1, dest0, dest1, x1, route, g_final, ys)


def _pick(n, prefs):
    for p in prefs:
        if n % p == 0:
            return p
    return n


def _layer(x2, pos2, batch, seq, attn_norm, w_in, b_forget, b_gate, sinks, w_proj_fox, w_proj_swa,
           w_out, ffn_norm, w_group, b_group, w_expert, b_expert, w1, w3, w2, final_norm):
    T, D = x2.shape
    scale = HEAD_DIM ** -0.5

    sp = np.cumsum([FOX_W, FOX_W, FOX_W, FOX_HEADS, SWA_W, SWA_KV_W, SWA_KV_W, D, D])[:-1]
    wq_f, wk_f, wv_f, w_fl, wq_s, wk_s, wv_s, wg_f, wg_s = jnp.split(w_in, sp, axis=1)
    head_order = np.asarray(SWA_HEAD_ORDER)
    wq_s = wq_s.reshape(D, SWA_HEADS, HEAD_DIM)[:, head_order, :].reshape(D, SWA_W)
    pieces = (wq_f * (scale * LOG2E), wk_f, wv_f, wq_s * scale, wk_s, wv_s,
              wg_f, wg_s, jnp.pad(w_fl, ((0, 0), (0, LANES - FOX_HEADS))))
    weights = {name: piece.astype(BF16) for name, piece in zip(W_NAMES, pieces)}
    half = HEAD_DIM // 2
    inv_freq = ROPE_THETA ** (-jnp.arange(half, dtype=F32) * 2.0 / HEAD_DIM)
    invf = jnp.tile(inv_freq, LANES // half).reshape(1, LANES)
    bf_pad = jnp.pad(b_forget.astype(F32), (0, LANES - FOX_HEADS)).reshape(1, LANES)
    sinks_perm = sinks.astype(F32)[head_order]
    wps = w_proj_swa.reshape(SWA_HEADS, HEAD_DIM, D)[head_order].reshape(SWA_W, D).astype(BF16)
    wpf = w_proj_fox.astype(BF16)
    wout = w_out.astype(BF16)
    w_r = jnp.pad(jnp.concatenate([w_group, w_expert], axis=1).astype(F32),
                  ((0, 0), (0, LANES - N_GROUPS - N_EXPERTS)))
    wr_hi = w_r.astype(BF16)
    wr_lo = (w_r - wr_hi.astype(F32)).astype(BF16)
    wr_cat = jnp.concatenate([wr_hi, wr_lo], axis=1)
    b_r = jnp.pad(jnp.concatenate([b_group, b_expert]).astype(F32),
                  (0, LANES - N_GROUPS - N_EXPERTS)).reshape(1, LANES)

    tm1 = _pick(seq, (512, 256, 128))
    qt, kf, vt, qs, ks, vs, g_f, g_s, qxt, kx = _in_proj(
        x2, pos2, attn_norm.reshape(1, D), weights, invf, bf_pad, b_gate.astype(F32),
        batch=batch, seq=seq, tm=tm1)
    tq = _pick(seq, (FOX_TQ, 128))
    tk = min(tq, FOX_TK)
    o_f = _fox_attn(qt, qxt, kf, kx, vt, batch=batch, seq=seq, tq=tq, tk=tk,
                    unroll=min(FOX_UNROLL, tq // tk))
    o_s = _swa_attn(sinks_perm, qs, ks, vs, batch=batch, seq=seq, tq=_pick(seq, (512, 256, 128)))

    tm3 = _pick(T, (512, 256, 128))
    x1, h2, route, route_t, counts = _mix_route(o_f, o_s, g_f, g_s, x2, wpf, wps, wout,
                                       ffn_norm.reshape(1, D), wr_cat, b_r, tm=tm3)

    P = 2 * T + N_EXPERTS * ROW_BLOCK
    n_blk = P // ROW_BLOCK
    dest, tbl, blk = _plan(counts, route_t, n_blk)
    dest0, dest1 = dest[ROUTE_E0], dest[ROUTE_E1]
    pad_end = tbl[1, :N_EXPERTS]
    bounds = jnp.concatenate([jnp.zeros((1,), jnp.int32), pad_end])
    blk_expert = blk[0, :n_blk]
    n_used = pad_end[-1:] // ROW_BLOCK

    xs = _dispatch(bounds, dest0, dest1, h2, P, tb=_pick(T, (1024, 512, 256, 128)))
    ys = _experts(blk_expert, n_used, xs, w1, w3, w2)
    return _combine(dest0, dest1, x1, route, final_norm.reshape(1, D), ys,
                    tb=_pick(T, (512, 256, 128)))


def kernel(x, positions, attn_norm, w_in, b_forget, b_gate, attn_sinks, w_proj_fox, w_proj_swa,
           w_out, ffn_norm, w_group, b_group, w_expert, b_expert, w1, w3, w2, final_norm):
    B, S, D = x.shape
    depth = attn_norm.shape[0]
    x2 = x.reshape(B * S, D)
    pos2 = positions.reshape(B * S, 1).astype(jnp.int32)
    assert depth == 1, "a single layer is followed directly by the final norm"
    out = _layer(x2, pos2, B, S, attn_norm[0], w_in[0], b_forget[0], b_gate[0], attn_sinks[0],
                 w_proj_fox[0], w_proj_swa[0], w_out[0], ffn_norm[0], w_group[0], b_group[0],
                 w_expert[0], b_expert[0], w1[0], w3[0], w2[0], final_norm)
    return out.reshape(B, S, D)
```

```python
import functools

import numpy as np
import jax
import jax.numpy as jnp
from jax import lax
from jax.experimental import pallas as pl
from jax.experimental.pallas import tpu as pltpu

HEAD_DIM = 64
FOX_HEADS = 8
SWA_HEADS = 8
SWA_KV_HEADS = 2
WINDOW = 128
ROPE_THETA = 10000.0
N_GROUPS = 8
EXPERTS_PER_GROUP = 8
N_EXPERTS = N_GROUPS * EXPERTS_PER_GROUP
ROW_BLOCK = 512
SUBLANES = 8
EPS = 1e-6
LANES = 128
NEG_BIG = -1e30
LOG2E = 1.4426950408889634
BIAS_TERMS = 3
BIAS_LANES = 2 * BIAS_TERMS
FOX_TQ = 1024
FOX_TK = 256
FOX_UNROLL = 4

FOX_W = FOX_HEADS * HEAD_DIM
SWA_W = SWA_HEADS * HEAD_DIM
SWA_KV_W = SWA_KV_HEADS * HEAD_DIM
SWA_HEAD_ORDER = (0, 4, 1, 5, 2, 6, 3, 7)

F32 = jnp.float32
BF16 = jnp.bfloat16
VMEM_LIMIT = 56 * 1024 * 1024


def _cparams(sem):
    return pltpu.CompilerParams(dimension_semantics=sem, vmem_limit_bytes=VMEM_LIMIT)


def _split3(x):
    hi = x.astype(BF16)
    r1 = x - hi.astype(F32)
    mid = r1.astype(BF16)
    lo = (r1 - mid.astype(F32)).astype(BF16)
    return hi, mid, lo


W_NAMES = ("q_fox", "k_fox", "v_fox", "q_swa", "k_swa", "v_swa", "gate_fox", "gate_swa", "forget")


def _in_proj_kernel(x_ref, pos_ref, g_ref, invf_ref, bf_ref, bg_ref, place_ref, ones_ref, *refs):
    w = dict(zip(W_NAMES, refs[:len(W_NAMES)]))
    (qf_ref, kf_ref, vf_ref, qs_ref, ks_ref, vs_ref, gf_ref, gs_ref, qx_ref, kx_ref,
     carry_ref) = refs[len(W_NAMES):]
    j = pl.program_id(1)
    tm = x_ref.shape[0]

    @pl.when(j == 0)
    def _():
        carry_ref[...] = jnp.zeros_like(carry_ref)

    x = x_ref[...]
    ms = jnp.mean(x * x, axis=-1, keepdims=True)
    h = ((x * lax.rsqrt(ms + EPS)) * g_ref[...]).astype(BF16)

    def mm(name):
        return jnp.dot(h, w[name][...], preferred_element_type=F32)

    def store_pairs_t(ref, y):
        for pr in range(ref.shape[0]):
            ref[pr] = y[:, pr * LANES:(pr + 1) * LANES].T.astype(BF16)

    store_pairs_t(qf_ref, mm("q_fox"))
    kf_ref[...] = mm("k_fox").astype(BF16)
    store_pairs_t(vf_ref, mm("v_fox"))

    ang = pos_ref[...].astype(F32) * invf_ref[...]
    cos = jnp.cos(ang)
    sin = jnp.sin(ang)
    n_rep = SWA_W // LANES
    cos_q = jnp.concatenate([cos] * n_rep, axis=1)
    sin_q = jnp.concatenate([sin] * n_rep, axis=1)
    def rot_half(y):
        width = y.shape[1]
        lane_y = lax.broadcasted_iota(jnp.int32, y.shape, 1)
        first = (lane_y & (HEAD_DIM - 1)) < HEAD_DIM // 2
        return jnp.where(first, -pltpu.roll(y, width - HEAD_DIM // 2, axis=1),
                         pltpu.roll(y, HEAD_DIM // 2, axis=1))

    q_s = mm("q_swa")
    k_s = mm("k_swa")
    qs_ref[...] = (q_s * cos_q + rot_half(q_s) * sin_q).astype(BF16)
    ks_ref[...] = (k_s * cos + rot_half(k_s) * sin).astype(BF16)
    vs_ref[...] = mm("v_swa").astype(BF16)

    gf_ref[...] = jax.nn.sigmoid(mm("gate_fox") + bg_ref[0:1, :]).astype(BF16)
    gs_ref[...] = jax.nn.sigmoid(mm("gate_swa") + bg_ref[1:2, :]).astype(BF16)

    z = mm("forget") + bf_ref[...]
    lane = lax.broadcasted_iota(jnp.int32, (tm, LANES), 1)
    lf = jnp.minimum(z, 0.0) - jnp.log1p(jnp.exp(-jnp.abs(z)))
    lf = jnp.where(lane < FOX_HEADS, lf, 0.0)
    row = lax.broadcasted_iota(jnp.int32, (tm, tm), 0)
    col = lax.broadcasted_iota(jnp.int32, (tm, tm), 1)
    tri = jnp.where(row >= col, 1.0, 0.0).astype(BF16)
    hi, mid, lo = _split3(lf)
    c = (jnp.dot(tri, hi, preferred_element_type=F32)
         + jnp.dot(tri, mid, preferred_element_type=F32)
         + jnp.dot(tri, lo, preferred_element_type=F32)) + carry_ref[0:1, :]
    carry_ref[0:1, :] = c[tm - 1:tm, :]
    chi, cmid, clo = _split3(c * LOG2E)
    ext = jnp.dot(jnp.concatenate([chi, cmid, clo], axis=1), place_ref[...],
                  preferred_element_type=F32) + ones_ref[...]
    qx_ref[...] = ext[:, :LANES].T.astype(BF16)
    kx_ref[...] = ext[:, LANES:].astype(BF16)


def _bias_placement():
    place = np.zeros((BIAS_TERMS * LANES, 2 * LANES), np.float32)
    ones = np.zeros((1, 2 * LANES), np.float32)
    for h in range(FOX_HEADS):
        for t in range(BIAS_TERMS):
            place[t * LANES + h, BIAS_LANES * h + t] = 1.0
            place[t * LANES + h, LANES + BIAS_LANES * h + BIAS_TERMS + t] = -1.0
            ones[0, BIAS_LANES * h + BIAS_TERMS + t] = 1.0
            ones[0, LANES + BIAS_LANES * h + t] = 1.0
    return jnp.asarray(place, BF16), jnp.asarray(ones, F32)


def _in_proj(x2, pos2, g_attn, weights, invf, bf_pad, b_gate, *, batch, seq, tm):
    T, D = x2.shape
    nj = seq // tm
    w_list = [weights[name] for name in W_NAMES]
    place, ones = _bias_placement()
    row_blk = lambda w: pl.BlockSpec((tm, w), lambda b, j: (b * nj + j, 0))
    const = lambda shape: pl.BlockSpec(shape, lambda b, j: (0, 0))
    n_pair = FOX_HEADS // 2
    pair_t = pl.BlockSpec((None, n_pair, LANES, tm), lambda b, j: (b, 0, 0, j))
    out_shapes = (
        jax.ShapeDtypeStruct((batch, n_pair, LANES, seq), BF16), jax.ShapeDtypeStruct((T, FOX_W), BF16),
        jax.ShapeDtypeStruct((batch, n_pair, LANES, seq), BF16), jax.ShapeDtypeStruct((T, SWA_W), BF16),
        jax.ShapeDtypeStruct((T, SWA_KV_W), BF16), jax.ShapeDtypeStruct((T, SWA_KV_W), BF16),
        jax.ShapeDtypeStruct((T, D), BF16), jax.ShapeDtypeStruct((T, D), BF16),
        jax.ShapeDtypeStruct((batch, LANES, seq), BF16), jax.ShapeDtypeStruct((T, LANES), BF16),
    )
    return pl.pallas_call(
        _in_proj_kernel,
        grid=(batch, nj),
        in_specs=[row_blk(D), row_blk(1), const((1, D)), const((1, LANES)),
                  const((1, LANES)), const((2, D)), const(place.shape), const(ones.shape)]
                 + [const(wt.shape) for wt in w_list],
        out_specs=(pair_t, row_blk(FOX_W), pair_t, row_blk(SWA_W),
                   row_blk(SWA_KV_W), row_blk(SWA_KV_W), row_blk(D), row_blk(D),
                   pl.BlockSpec((None, LANES, tm), lambda b, j: (b, 0, j)), row_blk(LANES)),
        out_shape=out_shapes,
        scratch_shapes=[pltpu.VMEM((8, LANES), F32)],
        compiler_params=_cparams(("arbitrary", "arbitrary")),
        name="in_proj",
    )(x2, pos2, g_attn, invf, bf_pad, b_gate, place, ones, *w_list)


def _fox_kernel(q_ref, qx_ref, k_ref, kx_ref, vt_ref, o_ref, acc_ref, *, tk, unroll):
    p = pl.program_id(1)
    i = pl.program_id(2)
    tq = q_ref.shape[1]
    n_diag = tq // tk
    qt = q_ref[...]
    qxt = qx_ref[...]
    row = lax.broadcasted_iota(jnp.int32, (LANES, tq), 0)
    zero = jnp.zeros_like(qt)
    q_ext = []
    for hh in range(2):
        first = BIAS_LANES * (2 * p + hh)
        head_rows = (row < HEAD_DIM) if hh == 0 else (row >= HEAD_DIM)
        bias_rows = (row >= first) & (row < first + BIAS_LANES)
        q_ext.append(jnp.concatenate([jnp.where(head_rows, qt, zero),
                                      jnp.where(bias_rows, qxt, zero)], axis=0))

    acc_ref[...] = jnp.zeros_like(acc_ref)

    def scores(ks, c0, masked):
        k_ext = jnp.concatenate([k_ref[pl.ds(ks, tk), :], kx_ref[pl.ds(ks, tk), :]], axis=1)
        out = []
        for hh in range(2):
            s = jnp.dot(k_ext, q_ext[hh][:, c0:], preferred_element_type=F32)
            if masked:
                key = lax.broadcasted_iota(jnp.int32, s.shape, 0)
                qry = lax.broadcasted_iota(jnp.int32, s.shape, 1)
                s = jnp.where(key <= qry, s, -jnp.inf)
            out.append(s)
        return out

    def accumulate(ks, c0, s_pair, stats):
        vt = vt_ref[:, pl.ds(ks, tk)]
        out = []
        for hh in range(2):
            m_all, l_all = stats[hh]
            m_old, l_old = m_all[:, c0:], l_all[:, c0:]
            s = s_pair[hh]
            m_new = jnp.maximum(m_old, jnp.max(s, axis=0, keepdims=True))
            alpha = jnp.exp2(m_old - m_new)
            pm = jnp.exp2(s - m_new)
            l_new = alpha * l_old + jnp.sum(pm, axis=0, keepdims=True)
            acc_ref[hh, :, c0:] = alpha * acc_ref[hh, :, c0:] + jnp.dot(
                vt, pm.astype(BF16), preferred_element_type=F32)
            if c0:
                m_new = jnp.concatenate([m_all[:, :c0], m_new], axis=1)
                l_new = jnp.concatenate([l_all[:, :c0], l_new], axis=1)
            out.append((m_new, l_new))
        return tuple(out)

    def run_tiles(first_tile, count, stats, diagonal):
        start = lambda u: pl.multiple_of((first_tile + u) * tk, tk)
        col0 = lambda u: u * tk if diagonal else 0
        s_next = scores(start(0), col0(0), diagonal)
        for u in range(count):
            s_cur = s_next
            if u + 1 < count:
                s_next = scores(start(u + 1), col0(u + 1), diagonal)
            stats = accumulate(start(u), col0(u), s_cur, stats)
        return stats

    init = tuple((jnp.full((1, tq), NEG_BIG, F32), jnp.zeros((1, tq), F32)) for _ in range(2))
    stats = lax.fori_loop(0, i * (n_diag // unroll),
                          lambda jb, st: run_tiles(jb * unroll, unroll, st, False), init)
    stats = run_tiles(i * n_diag, n_diag, stats, True)

    o_t = jnp.where(row < HEAD_DIM, acc_ref[0] * (1.0 / stats[0][1]),
                    acc_ref[1] * (1.0 / stats[1][1]))
    o_ref[...] = o_t.T.astype(BF16)


def _fox_attn(qt, qxt, kf, kx, vt, *, batch, seq, tq, tk, unroll):
    T = kf.shape[0]
    nq = seq // tq
    n_pair = FOX_HEADS // 2
    return pl.pallas_call(
        functools.partial(_fox_kernel, tk=tk, unroll=unroll),
        grid=(batch, n_pair, nq),
        in_specs=[
            pl.BlockSpec((None, None, LANES, tq), lambda b, p, i: (b, p, 0, i)),
            pl.BlockSpec((None, LANES, tq), lambda b, p, i: (b, 0, i)),
            pl.BlockSpec((seq, LANES), lambda b, p, i: (b, p)),
            pl.BlockSpec((seq, LANES), lambda b, p, i: (b, 0)),
            pl.BlockSpec((None, None, LANES, seq), lambda b, p, i: (b, p, 0, 0)),
        ],
        out_specs=pl.BlockSpec((tq, LANES), lambda b, p, i: (b * nq + i, p)),
        out_shape=jax.ShapeDtypeStruct((T, FOX_W), BF16),
        scratch_shapes=[pltpu.VMEM((2, LANES, tq), F32)],
        compiler_params=_cparams(("arbitrary", "arbitrary", "arbitrary")),
        name="fox_attn",
    )(qt, qxt, kf, kx, vt)


def _swa_kernel(sink_ref, q_ref, kc_ref, kp_ref, vc_ref, vp_ref, o_ref):
    i = pl.program_id(1)
    tq = q_ref.shape[0]
    n_win = tq // WINDOW
    n_col = SWA_W // LANES
    lane = lax.broadcasted_iota(jnp.int32, (WINDOW, LANES), 1)
    low = lane < HEAD_DIM
    r = lax.broadcasted_iota(jnp.int32, (WINDOW, 2 * WINDOW), 0)
    c = lax.broadcasted_iota(jnp.int32, (WINDOW, 2 * WINDOW), 1)
    dist = WINDOW + r - c
    band = (dist >= 0) & (dist < WINDOW)
    for w in range(n_win):
        if w == 0:
            k_win = jnp.concatenate([kp_ref[...], kc_ref[0:WINDOW, :]], axis=0)
            v_win = jnp.concatenate([vp_ref[...], vc_ref[0:WINDOW, :]], axis=0)
            first = i == 0
            valid = band & ((c >= WINDOW) | jnp.logical_not(first))
        else:
            k_win = kc_ref[(w - 1) * WINDOW:(w + 1) * WINDOW, :]
            v_win = vc_ref[(w - 1) * WINDOW:(w + 1) * WINDOW, :]
            valid = band
        for col_i in range(n_col):
            qc = q_ref[w * WINDOW:(w + 1) * WINDOW, col_i * LANES:(col_i + 1) * LANES]
            outs = []
            for half in range(2):
                qm = jnp.where(low if half == 0 else jnp.logical_not(low), qc, jnp.zeros_like(qc))
                s = lax.dot_general(qm, k_win, (((1,), (1,)), ((), ())),
                                    preferred_element_type=F32)
                s = jnp.where(valid, s, -jnp.inf)
                sink = sink_ref[2 * col_i + half]
                m = jnp.maximum(jnp.max(s, axis=1, keepdims=True), sink)
                pm = jnp.exp2(s - m)
                denom = jnp.sum(pm, axis=1, keepdims=True) + jnp.exp2(sink - m)
                outs.append(jnp.dot(pm.astype(BF16), v_win, preferred_element_type=F32)
                            * (1.0 / denom))
            o_ref[w * WINDOW:(w + 1) * WINDOW, col_i * LANES:(col_i + 1) * LANES] = (
                jnp.where(low, outs[0], outs[1]).astype(BF16))


def _swa_attn(sinks_perm, qs, ks, vs, *, batch, seq, tq):
    T = qs.shape[0]
    nq = seq // tq
    n_win = tq // WINDOW
    nb = seq // WINDOW
    cur = lambda w: pl.BlockSpec((tq, w), lambda b, i: (b * nq + i, 0))
    prev = pl.BlockSpec((WINDOW, SWA_KV_W),
                        lambda b, i: (jnp.maximum(b * nb + i * n_win - 1, 0), 0))
    return pl.pallas_call(
        _swa_kernel,
        grid=(batch, nq),
        in_specs=[pl.BlockSpec(memory_space=pltpu.SMEM), cur(SWA_W), cur(SWA_KV_W), prev,
                  cur(SWA_KV_W), prev],
        out_specs=cur(SWA_W),
        out_shape=jax.ShapeDtypeStruct((T, SWA_W), BF16),
        compiler_params=_cparams(("arbitrary", "arbitrary")),
        name="swa_attn",
    )(sinks_perm, qs, ks, ks, vs, vs)


ROUTE_E0, ROUTE_E1, ROUTE_G0, ROUTE_G1, ROUTE_R0, ROUTE_R1 = range(6)
ROUTE_ROWS = 8


def _mix_route_kernel(of_ref, os_ref, gf_ref, gs_ref, x_ref, wpf_ref, wps_ref, wout_ref, gn_ref,
                      wr_ref, br_ref, x1_ref, h2_ref, route_ref, route_t_ref, cnt_ref, carry_ref):
    step = pl.program_id(0)
    tm = x_ref.shape[0]

    @pl.when(step == 0)
    def _():
        carry_ref[...] = jnp.zeros_like(carry_ref)

    a = jnp.dot(of_ref[...], wpf_ref[...], preferred_element_type=F32)
    b = jnp.dot(os_ref[...], wps_ref[...], preferred_element_type=F32)
    merged = gf_ref[...].astype(F32) * a + gs_ref[...].astype(F32) * b
    y = jnp.dot(merged.astype(BF16), wout_ref[...], preferred_element_type=F32)
    x1 = x_ref[...] + y
    x1_ref[...] = x1
    ms = jnp.mean(x1 * x1, axis=-1, keepdims=True)
    h2 = (x1 * lax.rsqrt(ms + EPS)) * gn_ref[...]
    h2_ref[...] = h2

    h_hi = h2.astype(BF16)
    h_lo = (h2 - h_hi.astype(F32)).astype(BF16)
    wr = wr_ref[...]
    r_hi = jnp.dot(h_hi, wr, preferred_element_type=F32)
    logits = (r_hi[:, :LANES] + r_hi[:, LANES:]
              + jnp.dot(h_lo, wr[:, :LANES], preferred_element_type=F32)) + br_ref[...]

    lane = lax.broadcasted_iota(jnp.int32, (tm, LANES), 1)
    is_g = lane < N_GROUPS
    gl = jnp.where(is_g, logits, -jnp.inf)
    gmax = jnp.max(gl, axis=1, keepdims=True)
    gexp = jnp.where(is_g, jnp.exp(gl - gmax), 0.0)
    gprob = gexp / jnp.sum(gexp, axis=1, keepdims=True)
    g_w = jnp.max(gprob, axis=1, keepdims=True)
    g_idx = jnp.min(jnp.where(is_g & (gprob == g_w), lane, LANES), axis=1, keepdims=True)

    e_lane = lane - N_GROUPS
    in_grp = (e_lane >= 0) & (e_lane < N_EXPERTS) & ((e_lane >> 3) == g_idx)
    el = jnp.where(in_grp, logits, -jnp.inf)
    v0 = jnp.max(el, axis=1, keepdims=True)
    i0 = jnp.min(jnp.where(in_grp & (el == v0), lane, LANES), axis=1, keepdims=True)
    el1 = jnp.where(lane == i0, -jnp.inf, el)
    v1 = jnp.max(el1, axis=1, keepdims=True)
    i1 = jnp.min(jnp.where(in_grp & (lane != i0) & (el1 == v1), lane, LANES), axis=1, keepdims=True)
    t = jnp.exp(v1 - v0)
    den = 1.0 + t
    gate0 = g_w * (1.0 / den)
    gate1 = g_w * (t / den)
    e0 = i0 - N_GROUPS
    e1 = i1 - N_GROUPS

    oh0 = jnp.where(lane == e0, 1.0, 0.0)
    oh1 = jnp.where(lane == e1, 1.0, 0.0)
    row = lax.broadcasted_iota(jnp.int32, (tm, tm), 0)
    col = lax.broadcasted_iota(jnp.int32, (tm, tm), 1)
    tri = jnp.where(row > col, 1.0, 0.0).astype(BF16)
    cs0 = jnp.sum(oh0, axis=0, keepdims=True)
    cs1 = jnp.sum(oh1, axis=0, keepdims=True)
    carry = carry_ref[0:1, :]
    pre = jnp.dot(tri, jnp.concatenate([oh0, oh1], axis=1).astype(BF16),
                  preferred_element_type=F32)
    pre0 = pre[:, :LANES] + carry
    pre1 = pre[:, LANES:] + (carry + cs0)
    rank0 = jnp.sum(oh0 * pre0, axis=1, keepdims=True)
    rank1 = jnp.sum(oh1 * pre1, axis=1, keepdims=True)
    new_carry = carry + cs0 + cs1
    carry_ref[0:1, :] = new_carry
    cnt_ref[...] = jnp.broadcast_to(new_carry, cnt_ref.shape)

    route = jnp.where(lane == ROUTE_E0, e0.astype(F32), 0.0)
    route = jnp.where(lane == ROUTE_E1, e1.astype(F32), route)
    route = jnp.where(lane == ROUTE_G0, gate0, route)
    route = jnp.where(lane == ROUTE_G1, gate1, route)
    route = jnp.where(lane == ROUTE_R0, rank0, route)
    route = jnp.where(lane == ROUTE_R1, rank1, route)
    route_ref[...] = route
    route_t_ref[...] = route.T[0:route_t_ref.shape[0], :]


def _mix_route(o_f, o_s, g_f, g_s, x2, wpf, wps, wout, g_ffn, wr_cat, b_r, *, tm):
    T, D = x2.shape
    row_blk = lambda w: pl.BlockSpec((tm, w), lambda i: (i, 0))
    const = lambda shape: pl.BlockSpec(shape, lambda i: (0, 0))
    return pl.pallas_call(
        _mix_route_kernel,
        grid=(T // tm,),
        in_specs=[row_blk(FOX_W), row_blk(SWA_W), row_blk(D), row_blk(D), row_blk(D),
                  const((FOX_W, D)), const((SWA_W, D)), const((D, D)), const((1, D)),
                  const((D, 2 * LANES)), const((1, LANES))],
        out_specs=(row_blk(D), row_blk(D), row_blk(LANES),
                   pl.BlockSpec((ROUTE_ROWS, tm), lambda i: (0, i)), const((8, LANES))),
        out_shape=(jax.ShapeDtypeStruct((T, D), F32), jax.ShapeDtypeStruct((T, D), F32),
                   jax.ShapeDtypeStruct((T, LANES), F32), jax.ShapeDtypeStruct((ROUTE_ROWS, T), F32),
                   jax.ShapeDtypeStruct((8, LANES), F32)),
        scratch_shapes=[pltpu.VMEM((8, LANES), F32)],
        compiler_params=_cparams(("arbitrary",)),
        name="mix_route",
    )(o_f, o_s, g_f, g_s, x2, wpf, wps, wout, g_ffn, wr_cat, b_r)


def _plan_kernel(cnt_ref, route_t_ref, dest_ref, tbl_ref, blk_ref):
    lane = lax.broadcasted_iota(jnp.int32, cnt_ref.shape, 1)
    cnt = jnp.where(lane < N_EXPERTS, cnt_ref[...], 0.0)
    padded = jnp.floor((cnt + (ROW_BLOCK - 1)) * (1.0 / ROW_BLOCK)) * ROW_BLOCK
    pad_end = padded
    shift = 1
    while shift < N_EXPERTS:
        pad_end = pad_end + jnp.where(lane >= shift, pltpu.roll(pad_end, shift, axis=1), 0.0)
        shift *= 2
    pad_start = pad_end - padded
    row = lax.broadcasted_iota(jnp.int32, cnt_ref.shape, 0)
    tbl_ref[...] = jnp.where(row == 0, pad_start, pad_end).astype(jnp.int32)

    fields = route_t_ref[...]
    start_of = jnp.zeros_like(fields)
    for e in range(N_EXPERTS):
        start_of = jnp.where(fields == float(e), pad_start[0:1, e:e + 1], start_of)
    ranks = pltpu.roll(fields, ROUTE_ROWS - ROUTE_R0, axis=0)
    dest_ref[...] = (start_of + ranks).astype(jnp.int32)

    ends = pad_end.T[0:N_EXPERTS, 0:1]
    n_col = blk_ref.shape[1]
    blk_pos = (lax.broadcasted_iota(jnp.int32, (N_EXPERTS, n_col), 1) * ROW_BLOCK).astype(F32)
    owner = jnp.sum(jnp.where(ends <= blk_pos, 1.0, 0.0), axis=0, keepdims=True)
    blk_ref[...] = jnp.broadcast_to(jnp.minimum(owner, N_EXPERTS - 1.0), blk_ref.shape).astype(jnp.int32)


def _plan(counts, route_t, n_blk):
    T = route_t.shape[1]
    n_col = (n_blk + LANES - 1) // LANES * LANES
    full = lambda shape: pl.BlockSpec(shape, lambda i: (0, 0))
    return pl.pallas_call(
        _plan_kernel,
        grid=(1,),
        in_specs=[full((8, LANES)), full((ROUTE_ROWS, T))],
        out_specs=(full((ROUTE_ROWS, T)), full((8, LANES)), full((8, n_col))),
        out_shape=(jax.ShapeDtypeStruct((ROUTE_ROWS, T), jnp.int32),
                   jax.ShapeDtypeStruct((8, LANES), jnp.int32),
                   jax.ShapeDtypeStruct((8, n_col), jnp.int32)),
        compiler_params=_cparams(("arbitrary",)),
        name="plan",
    )(counts, route_t)


def _dispatch_kernel(bounds_ref, dest0_ref, dest1_ref, h_ref, xs_ref, zbuf, sem, zsem):
    step = pl.program_id(0)
    tb = h_ref.shape[0]

    @pl.when(step == 0)
    def _():
        zbuf[...] = jnp.zeros_like(zbuf)

        def tail_copy(e):
            tail = pl.multiple_of(bounds_ref[e + 1] - ROW_BLOCK, ROW_BLOCK)
            return pltpu.make_async_copy(zbuf, xs_ref.at[pl.ds(tail, ROW_BLOCK), :], zsem)

        def for_nonempty(action):
            def body(e, carry):
                @pl.when(bounds_ref[e + 1] > bounds_ref[e])
                def _():
                    action(tail_copy(e))
                return carry
            lax.fori_loop(0, N_EXPERTS, body, 0)

        n_rows = xs_ref.shape[0]
        first_unused = bounds_ref[N_EXPERTS] // ROW_BLOCK

        def unused_copy(blk):
            return pltpu.make_async_copy(
                zbuf, xs_ref.at[pl.ds(pl.multiple_of(blk * ROW_BLOCK, ROW_BLOCK), ROW_BLOCK), :], zsem)

        def for_unused(action):
            def body(blk, carry):
                action(unused_copy(blk))
                return carry
            lax.fori_loop(first_unused, n_rows // ROW_BLOCK, body, 0)

        for_nonempty(lambda cp: cp.start())
        for_unused(lambda cp: cp.start())
        for_nonempty(lambda cp: cp.wait())
        for_unused(lambda cp: cp.wait())

    def start(j, carry):
        base = pl.multiple_of(j * SUBLANES, SUBLANES)
        tile = h_ref.at[pl.ds(base, SUBLANES), :]
        for u in range(SUBLANES):
            for k, dest_ref in enumerate((dest0_ref, dest1_ref)):
                d = dest_ref[base + u]
                pltpu.make_async_copy(tile.at[pl.ds(u, 1), :], xs_ref.at[pl.ds(d, 1), :],
                                      sem).start(priority=k)
        return carry

    lax.fori_loop(0, tb // SUBLANES, start, 0)
    for _ in range(2):
        pltpu.make_async_copy(h_ref, xs_ref.at[pl.ds(0, tb), :], sem).wait()


def _dispatch(bounds, dest0, dest1, h2, n_rows, *, tb):
    T, D = h2.shape
    smem_blk = pl.BlockSpec((tb,), lambda i, bd: (i,), memory_space=pltpu.SMEM)
    grid_spec = pltpu.PrefetchScalarGridSpec(
        num_scalar_prefetch=1,
        grid=(T // tb,),
        in_specs=[smem_blk, smem_blk, pl.BlockSpec((tb, D), lambda i, bd: (i, 0))],
        out_specs=pl.BlockSpec(memory_space=pl.ANY),
        scratch_shapes=[pltpu.VMEM((ROW_BLOCK, D), F32), pltpu.SemaphoreType.DMA(()),
                        pltpu.SemaphoreType.DMA(())],
    )
    return pl.pallas_call(
        _dispatch_kernel,
        grid_spec=grid_spec,
        out_shape=jax.ShapeDtypeStruct((n_rows, D), F32),
        compiler_params=_cparams(("arbitrary",)),
        name="dispatch",
    )(bounds, dest0, dest1, h2)


def _experts_kernel(be_ref, nu_ref, xs_ref, w1_ref, w3_ref, w2_ref, ys_ref,
                    w1f, w3f, w2f, w1b, w3b, w2b, region_ref, wsem):
    i = pl.program_id(0)
    n_used = nu_ref[0]
    e = be_ref[i]
    used = i < n_used
    first = (i == 0) | (e != be_ref[jnp.maximum(i - 1, 0)])

    def weight_copies(expert, slot):
        return (pltpu.make_async_copy(w1_ref.at[expert], w1f.at[slot], wsem.at[slot]),
                pltpu.make_async_copy(w3_ref.at[expert], w3f.at[slot], wsem.at[slot]),
                pltpu.make_async_copy(w2_ref.at[expert], w2f.at[slot], wsem.at[slot]))

    @pl.when(i == 0)
    def _():
        region_ref[0] = 0
        for cp in weight_copies(e, 0):
            cp.start()

    @pl.when(used & first)
    def _():
        slot = region_ref[0] % 2
        region_ref[0] = region_ref[0] + 1
        for cp in weight_copies(e, slot):
            cp.wait()
        w1b[...] = w1f[slot].astype(BF16)
        w3b[...] = w3f[slot].astype(BF16)
        w2b[...] = w2f[slot].astype(BF16)
        nxt = lax.while_loop(lambda j: (j < n_used) & (be_ref[jnp.minimum(j, n_used - 1)] == e),
                             lambda j: j + 1, i + 1)

        @pl.when(nxt < n_used)
        def _():
            for cp in weight_copies(be_ref[jnp.minimum(nxt, n_used - 1)], 1 - slot):
                cp.start()

    @pl.when(used)
    def _():
        xb = xs_ref[...].astype(BF16)
        de = w1b.shape[1]
        n_part = 2
        part = de // n_part
        acts = []
        for q in range(n_part):
            cols = slice(q * part, (q + 1) * part)
            h1 = jnp.dot(xb, w1b[:, cols], preferred_element_type=F32)
            h3 = jnp.dot(xb, w3b[:, cols], preferred_element_type=F32)
            acts.append(((h1 * jax.nn.sigmoid(h1)) * h3).astype(BF16))
        y = jnp.dot(acts[0], w2b[0:part, :], preferred_element_type=F32)
        for q in range(1, n_part):
            y = y + jnp.dot(acts[q], w2b[q * part:(q + 1) * part, :], preferred_element_type=F32)
        ys_ref[...] = y

    @pl.when(jnp.logical_not(used))
    def _():
        ys_ref[...] = jnp.zeros_like(ys_ref)


def _experts(blk_expert, n_used, xs, w1, w3, w2):
    P, D = xs.shape
    de = w1.shape[-1]
    n_blk = P // ROW_BLOCK
    grid_spec = pltpu.PrefetchScalarGridSpec(
        num_scalar_prefetch=2,
        grid=(n_blk,),
        in_specs=[pl.BlockSpec((ROW_BLOCK, D),
                               lambda i, be, nu: (jnp.minimum(i, jnp.maximum(nu[0] - 1, 0)), 0)),
                  pl.BlockSpec(memory_space=pl.ANY), pl.BlockSpec(memory_space=pl.ANY),
                  pl.BlockSpec(memory_space=pl.ANY)],
        out_specs=pl.BlockSpec((ROW_BLOCK, D), lambda i, be, nu: (i, 0)),
        scratch_shapes=[pltpu.VMEM((2, D, de), F32), pltpu.VMEM((2, D, de), F32),
                        pltpu.VMEM((2, de, D), F32),
                        pltpu.VMEM((D, de), BF16), pltpu.VMEM((D, de), BF16),
                        pltpu.VMEM((de, D), BF16),
                        pltpu.SMEM((1,), jnp.int32), pltpu.SemaphoreType.DMA((2,))],
    )
    return pl.pallas_call(
        _experts_kernel,
        grid_spec=grid_spec,
        out_shape=jax.ShapeDtypeStruct((P, D), F32),
        compiler_params=_cparams(("arbitrary",)),
        name="experts",
    )(blk_expert, n_used, xs, w1, w3, w2)


def _combine_kernel(d0_cur, d1_cur, d0_next, d1_next, x1_ref, route_ref, gfin_ref, ys_ref, o_ref,
                    ybuf, sem):
    i = pl.program_id(0)
    tb = x1_ref.shape[0]
    slot = i % 2

    def issue(dest_refs, buf_slot):
        def start(j, carry):
            base = pl.multiple_of(j * SUBLANES, SUBLANES)
            for k, dest_ref in enumerate(dest_refs):
                tile = ybuf.at[buf_slot, k, pl.ds(base, SUBLANES), :]
                for u in range(SUBLANES):
                    d = dest_ref[base + u]
                    pltpu.make_async_copy(ys_ref.at[pl.ds(d, 1), :], tile.at[pl.ds(u, 1), :],
                                          sem.at[buf_slot]).start(priority=k)
            return carry
        lax.fori_loop(0, tb // SUBLANES, start, 0)

    @pl.when(i == 0)
    def _():
        issue((d0_cur, d1_cur), 0)

    @pl.when(i + 1 < pl.num_programs(0))
    def _():
        issue((d0_next, d1_next), 1 - slot)

    for k in range(2):
        pltpu.make_async_copy(ys_ref.at[pl.ds(0, tb), :], ybuf.at[slot, k], sem.at[slot]).wait()

    route = route_ref[...]
    lane = lax.broadcasted_iota(jnp.int32, route.shape, 1)
    g0 = jnp.sum(jnp.where(lane == ROUTE_G0, route, 0.0), axis=1, keepdims=True)
    g1 = jnp.sum(jnp.where(lane == ROUTE_G1, route, 0.0), axis=1, keepdims=True)
    moe = g0 * ybuf[slot, 0] + g1 * ybuf[slot, 1]
    x2 = x1_ref[...] + moe
    ms = jnp.mean(x2 * x2, axis=-1, keepdims=True)
    o_ref[...] = (x2 * lax.rsqrt(ms + EPS)) * gfin_ref[...]


def _combine(dest0, dest1, x1, route, g_final, ys, *, tb):
    T, D = x1.shape
    nb = T // tb
    cur = pl.BlockSpec((tb,), lambda i: (i,), memory_space=pltpu.SMEM)
    nxt = pl.BlockSpec((tb,), lambda i: (jnp.minimum(i + 1, nb - 1),), memory_space=pltpu.SMEM)
    return pl.pallas_call(
        _combine_kernel,
        grid=(nb,),
        in_specs=[cur, cur, nxt, nxt,
                  pl.BlockSpec((tb, D), lambda i: (i, 0)),
                  pl.BlockSpec((tb, LANES), lambda i: (i, 0)),
                  pl.BlockSpec((1, D), lambda i: (0, 0)),
                  pl.BlockSpec(memory_space=pl.ANY)],
        out_specs=pl.BlockSpec((tb, D), lambda i: (i, 0)),
        out_shape=jax.ShapeDtypeStruct((T, D), F32),
        scratch_shapes=[pltpu.VMEM((2, 2, tb, D), F32), pltpu.SemaphoreType.DMA((2,))],
        compiler_params=_cparams(("arbitrary",)),
        name="combine",
    )(dest0, dest1, dest0, dest1, x1, route, g_final, ys)


def _pick(n, prefs):
    for p in prefs:
        if n % p == 0:
            return p
    return n


def _layer(x2, pos2, batch, seq, attn_norm, w_in, b_forget, b_gate, sinks, w_proj_fox, w_proj_swa,
           w_out, ffn_norm, w_group, b_group, w_expert, b_expert, w1, w3, w2, final_norm):
    T, D = x2.shape
    scale = HEAD_DIM ** -0.5

    sp = np.cumsum([FOX_W, FOX_W, FOX_W, FOX_HEADS, SWA_W, SWA_KV_W, SWA_KV_W, D, D])[:-1]
    wq_f, wk_f, wv_f, w_fl, wq_s, wk_s, wv_s, wg_f, wg_s = jnp.split(w_in, sp, axis=1)
    head_order = np.asarray(SWA_HEAD_ORDER)
    wq_s = wq_s.reshape(D, SWA_HEADS, HEAD_DIM)[:, head_order, :].reshape(D, SWA_W)
    pieces = (wq_f * (scale * LOG2E), wk_f, wv_f, wq_s * (scale * LOG2E), wk_s, wv_s,
              wg_f, wg_s, jnp.pad(w_fl, ((0, 0), (0, LANES - FOX_HEADS))))
    weights = {name: piece.astype(BF16) for name, piece in zip(W_NAMES, pieces)}
    half = HEAD_DIM // 2
    inv_freq = ROPE_THETA ** (-jnp.arange(half, dtype=F32) * 2.0 / HEAD_DIM)
    invf = jnp.tile(inv_freq, LANES // half).reshape(1, LANES)
    bf_pad = jnp.pad(b_forget.astype(F32), (0, LANES - FOX_HEADS)).reshape(1, LANES)
    sinks_perm = sinks.astype(F32)[head_order] * LOG2E
    wps = w_proj_swa.reshape(SWA_HEADS, HEAD_DIM, D)[head_order].reshape(SWA_W, D).astype(BF16)
    wpf = w_proj_fox.astype(BF16)
    wout = w_out.astype(BF16)
    w_r = jnp.pad(jnp.concatenate([w_group, w_expert], axis=1).astype(F32),
                  ((0, 0), (0, LANES - N_GROUPS - N_EXPERTS)))
    wr_hi = w_r.astype(BF16)
    wr_lo = (w_r - wr_hi.astype(F32)).astype(BF16)
    wr_cat = jnp.concatenate([wr_hi, wr_lo], axis=1)
    b_r = jnp.pad(jnp.concatenate([b_group, b_expert]).astype(F32),
                  (0, LANES - N_GROUPS - N_EXPERTS)).reshape(1, LANES)

    tm1 = _pick(seq, (512, 256, 128))
    qt, kf, vt, qs, ks, vs, g_f, g_s, qxt, kx = _in_proj(
        x2, pos2, attn_norm.reshape(1, D), weights, invf, bf_pad, b_gate.astype(F32),
        batch=batch, seq=seq, tm=tm1)
    tq = _pick(seq, (FOX_TQ, 128))
    tk = min(tq, FOX_TK)
    o_f = _fox_attn(qt, qxt, kf, kx, vt, batch=batch, seq=seq, tq=tq, tk=tk,
                    unroll=min(FOX_UNROLL, tq // tk))
    o_s = _swa_attn(sinks_perm, qs, ks, vs, batch=batch, seq=seq, tq=_pick(seq, (512, 256, 128)))

    tm3 = _pick(T, (512, 256, 128))
    x1, h2, route, route_t, counts = _mix_route(o_f, o_s, g_f, g_s, x2, wpf, wps, wout,
                                       ffn_norm.reshape(1, D), wr_cat, b_r, tm=tm3)

    P = 2 * T + N_EXPERTS * ROW_BLOCK
    n_blk = P // ROW_BLOCK
    dest, tbl, blk = _plan(counts, route_t, n_blk)
    dest0, dest1 = dest[ROUTE_E0], dest[ROUTE_E1]
    pad_end = tbl[1, :N_EXPERTS]
    bounds = jnp.concatenate([jnp.zeros((1,), jnp.int32), pad_end])
    blk_expert = blk[0, :n_blk]
    n_used = pad_end[-1:] // ROW_BLOCK

    xs = _dispatch(bounds, dest0, dest1, h2, P, tb=_pick(T, (1024, 512, 256, 128)))
    ys = _experts(blk_expert, n_used, xs, w1, w3, w2)
    return _combine(dest0, dest1, x1, route, final_norm.reshape(1, D), ys,
                    tb=_pick(T, (512, 256, 128)))


def kernel(x, positions, attn_norm, w_in, b_forget, b_gate, attn_sinks, w_proj_fox, w_proj_swa,
           w_out, ffn_norm, w_group, b_group, w_expert, b_expert, w1, w3, w2, final_norm):
    B, S, D = x.shape
    depth = attn_norm.shape[0]
    x2 = x.reshape(B * S, D)
    pos2 = positions.reshape(B * S, 1).astype(jnp.int32)
    assert depth == 1, "a single layer is followed directly by the final norm"
    out = _layer(x2, pos2, B, S, attn_norm[0], w_in[0], b_forget[0], b_gate[0], attn_sinks[0],
                 w_proj_fox[0], w_proj_swa[0], w_out[0], ffn_norm[0], w_group[0], b_group[0],
                 w_expert[0], b_expert[0], w1[0], w3[0], w2[0], final_norm)
    return out.reshape(B, S, D)
```

```python
import functools

import numpy as np
import jax
import jax.numpy as jnp
from jax import lax
from jax.experimental import pallas as pl
from jax.experimental.pallas import tpu as pltpu

HEAD_DIM = 64
FOX_HEADS = 8
SWA_HEADS = 8
SWA_KV_HEADS = 2
WINDOW = 128
ROPE_THETA = 10000.0
N_GROUPS = 8
EXPERTS_PER_GROUP = 8
N_EXPERTS = N_GROUPS * EXPERTS_PER_GROUP
ROW_BLOCK = 512
SUBLANES = 8
EPS = 1e-6
LANES = 128
NEG_BIG = -1e30
LOG2E = 1.4426950408889634
BIAS_TERMS = 3
BIAS_LANES = 2 * BIAS_TERMS
FOX_TQ = 1024
FOX_TK = 256
FOX_UNROLL = 4

FOX_W = FOX_HEADS * HEAD_DIM
SWA_W = SWA_HEADS * HEAD_DIM
SWA_KV_W = SWA_KV_HEADS * HEAD_DIM
SWA_HEAD_ORDER = (0, 4, 1, 5, 2, 6, 3, 7)

F32 = jnp.float32
BF16 = jnp.bfloat16
VMEM_LIMIT = 56 * 1024 * 1024


def _cparams(sem):
    return pltpu.CompilerParams(dimension_semantics=sem, vmem_limit_bytes=VMEM_LIMIT)


def _split3(x):
    hi = x.astype(BF16)
    r1 = x - hi.astype(F32)
    mid = r1.astype(BF16)
    lo = (r1 - mid.astype(F32)).astype(BF16)
    return hi, mid, lo


W_NAMES = ("q_fox", "k_fox", "v_fox", "q_swa", "kv_swa", "gate_fox", "gate_swa", "forget")


def _in_proj_kernel(x_ref, pos_ref, g_ref, invf_ref, bf_ref, bg_ref, place_ref, ones_ref, *refs):
    w = dict(zip(W_NAMES, refs[:len(W_NAMES)]))
    (qf_ref, kf_ref, vf_ref, qs_ref, ks_ref, vs_ref, gf_ref, gs_ref, qx_ref, kx_ref,
     carry_ref) = refs[len(W_NAMES):]
    j = pl.program_id(1)
    tm = x_ref.shape[0]

    @pl.when(j == 0)
    def _():
        carry_ref[...] = jnp.zeros_like(carry_ref)

    x = x_ref[...]
    ms = jnp.mean(x * x, axis=-1, keepdims=True)
    h = ((x * lax.rsqrt(ms + EPS)) * g_ref[...]).astype(BF16)

    def mm(name):
        return jnp.dot(h, w[name][...], preferred_element_type=F32)

    def store_pairs_t(ref, y):
        for pr in range(ref.shape[0]):
            ref[pr] = y[:, pr * LANES:(pr + 1) * LANES].T.astype(BF16)

    store_pairs_t(qf_ref, mm("q_fox"))
    kf_ref[...] = mm("k_fox").astype(BF16)
    store_pairs_t(vf_ref, mm("v_fox"))

    ang = pos_ref[...].astype(F32) * invf_ref[...]
    cos = jnp.cos(ang)
    sin = jnp.sin(ang)
    n_rep = SWA_W // LANES
    cos_q = jnp.concatenate([cos] * n_rep, axis=1)
    sin_q = jnp.concatenate([sin] * n_rep, axis=1)
    def rot_half(y):
        width = y.shape[1]
        lane_y = lax.broadcasted_iota(jnp.int32, y.shape, 1)
        first = (lane_y & (HEAD_DIM - 1)) < HEAD_DIM // 2
        return jnp.where(first, -pltpu.roll(y, width - HEAD_DIM // 2, axis=1),
                         pltpu.roll(y, HEAD_DIM // 2, axis=1))

    q_s = mm("q_swa")
    kv_s = mm("kv_swa")
    k_s = kv_s[:, :SWA_KV_W]
    qs_ref[...] = (q_s * cos_q + rot_half(q_s) * sin_q).astype(BF16)
    ks_ref[...] = (k_s * cos + rot_half(k_s) * sin).astype(BF16)
    vs_ref[...] = kv_s[:, SWA_KV_W:].astype(BF16)

    gf_ref[...] = jax.nn.sigmoid(mm("gate_fox") + bg_ref[0:1, :]).astype(BF16)
    gs_ref[...] = jax.nn.sigmoid(mm("gate_swa") + bg_ref[1:2, :]).astype(BF16)

    z = mm("forget") + bf_ref[...]
    lane = lax.broadcasted_iota(jnp.int32, (tm, LANES), 1)
    lf = jnp.minimum(z, 0.0) - jnp.log1p(jnp.exp(-jnp.abs(z)))
    lf = jnp.where(lane < FOX_HEADS, lf, 0.0)
    row = lax.broadcasted_iota(jnp.int32, (tm, tm), 0)
    col = lax.broadcasted_iota(jnp.int32, (tm, tm), 1)
    tri = jnp.where(row >= col, 1.0, 0.0).astype(BF16)
    hi, mid, lo = _split3(lf)
    c = (jnp.dot(tri, hi, preferred_element_type=F32)
         + jnp.dot(tri, mid, preferred_element_type=F32)
         + jnp.dot(tri, lo, preferred_element_type=F32)) + carry_ref[0:1, :]
    carry_ref[0:1, :] = c[tm - 1:tm, :]
    chi, cmid, clo = _split3(c * LOG2E)
    ext = jnp.dot(jnp.concatenate([chi, cmid, clo], axis=1), place_ref[...],
                  preferred_element_type=F32) + ones_ref[...]
    qx_ref[...] = ext[:, :LANES].T.astype(BF16)
    kx_ref[...] = ext[:, LANES:].astype(BF16)


def _bias_placement():
    place = np.zeros((BIAS_TERMS * LANES, 2 * LANES), np.float32)
    ones = np.zeros((1, 2 * LANES), np.float32)
    for h in range(FOX_HEADS):
        for t in range(BIAS_TERMS):
            place[t * LANES + h, BIAS_LANES * h + t] = 1.0
            place[t * LANES + h, LANES + BIAS_LANES * h + BIAS_TERMS + t] = -1.0
            ones[0, BIAS_LANES * h + BIAS_TERMS + t] = 1.0
            ones[0, LANES + BIAS_LANES * h + t] = 1.0
    return jnp.asarray(place, BF16), jnp.asarray(ones, F32)


def _in_proj(x2, pos2, g_attn, weights, invf, bf_pad, b_gate, *, batch, seq, tm):
    T, D = x2.shape
    nj = seq // tm
    w_list = [weights[name] for name in W_NAMES]
    place, ones = _bias_placement()
    row_blk = lambda w: pl.BlockSpec((tm, w), lambda b, j: (b * nj + j, 0))
    const = lambda shape: pl.BlockSpec(shape, lambda b, j: (0, 0))
    n_pair = FOX_HEADS // 2
    pair_t = pl.BlockSpec((None, n_pair, LANES, tm), lambda b, j: (b, 0, 0, j))
    out_shapes = (
        jax.ShapeDtypeStruct((batch, n_pair, LANES, seq), BF16), jax.ShapeDtypeStruct((T, FOX_W), BF16),
        jax.ShapeDtypeStruct((batch, n_pair, LANES, seq), BF16), jax.ShapeDtypeStruct((T, SWA_W), BF16),
        jax.ShapeDtypeStruct((T, SWA_KV_W), BF16), jax.ShapeDtypeStruct((T, SWA_KV_W), BF16),
        jax.ShapeDtypeStruct((T, D), BF16), jax.ShapeDtypeStruct((T, D), BF16),
        jax.ShapeDtypeStruct((batch, LANES, seq), BF16), jax.ShapeDtypeStruct((T, LANES), BF16),
    )
    return pl.pallas_call(
        _in_proj_kernel,
        grid=(batch, nj),
        in_specs=[row_blk(D), row_blk(1), const((1, D)), const((1, LANES)),
                  const((1, LANES)), const((2, D)), const(place.shape), const(ones.shape)]
                 + [const(wt.shape) for wt in w_list],
        out_specs=(pair_t, row_blk(FOX_W), pair_t, row_blk(SWA_W),
                   row_blk(SWA_KV_W), row_blk(SWA_KV_W), row_blk(D), row_blk(D),
                   pl.BlockSpec((None, LANES, tm), lambda b, j: (b, 0, j)), row_blk(LANES)),
        out_shape=out_shapes,
        scratch_shapes=[pltpu.VMEM((8, LANES), F32)],
        compiler_params=_cparams(("arbitrary", "arbitrary")),
        name="in_proj",
    )(x2, pos2, g_attn, invf, bf_pad, b_gate, place, ones, *w_list)


def _fox_kernel(q_ref, qx_ref, k_ref, kx_ref, vt_ref, o_ref, acc_ref, *, tk, unroll):
    p = pl.program_id(1)
    i = pl.program_id(2)
    tq = q_ref.shape[1]
    n_diag = tq // tk
    qt = q_ref[...]
    qxt = qx_ref[...]
    row = lax.broadcasted_iota(jnp.int32, (LANES, tq), 0)
    zero = jnp.zeros_like(qt)
    q_ext = []
    for hh in range(2):
        first = BIAS_LANES * (2 * p + hh)
        head_rows = (row < HEAD_DIM) if hh == 0 else (row >= HEAD_DIM)
        bias_rows = (row >= first) & (row < first + BIAS_LANES)
        q_ext.append(jnp.concatenate([jnp.where(head_rows, qt, zero),
                                      jnp.where(bias_rows, qxt, zero)], axis=0))

    acc_ref[...] = jnp.zeros_like(acc_ref)

    def scores(ks, c0, masked):
        k_ext = jnp.concatenate([k_ref[pl.ds(ks, tk), :], kx_ref[pl.ds(ks, tk), :]], axis=1)
        out = []
        for hh in range(2):
            s = jnp.dot(k_ext, q_ext[hh][:, c0:], preferred_element_type=F32)
            if masked:
                key = lax.broadcasted_iota(jnp.int32, s.shape, 0)
                qry = lax.broadcasted_iota(jnp.int32, s.shape, 1)
                s = jnp.where(key <= qry, s, -jnp.inf)
            out.append(s)
        return out

    def accumulate(ks, c0, s_pair, stats):
        vt = vt_ref[:, pl.ds(ks, tk)]
        out = []
        for hh in range(2):
            m_all, l_all = stats[hh]
            m_old, l_old = m_all[:, c0:], l_all[:, c0:]
            s = s_pair[hh]
            m_new = jnp.maximum(m_old, jnp.max(s, axis=0, keepdims=True))
            alpha = jnp.exp2(m_old - m_new)
            pm = jnp.exp2(s - m_new)
            l_new = alpha * l_old + jnp.sum(pm, axis=0, keepdims=True)
            acc_ref[hh, :, c0:] = alpha * acc_ref[hh, :, c0:] + jnp.dot(
                vt, pm.astype(BF16), preferred_element_type=F32)
            if c0:
                m_new = jnp.concatenate([m_all[:, :c0], m_new], axis=1)
                l_new = jnp.concatenate([l_all[:, :c0], l_new], axis=1)
            out.append((m_new, l_new))
        return tuple(out)

    def run_tiles(first_tile, count, stats, diagonal):
        start = lambda u: pl.multiple_of((first_tile + u) * tk, tk)
        col0 = lambda u: u * tk if diagonal else 0
        s_next = scores(start(0), col0(0), diagonal)
        for u in range(count):
            s_cur = s_next
            if u + 1 < count:
                s_next = scores(start(u + 1), col0(u + 1), diagonal)
            stats = accumulate(start(u), col0(u), s_cur, stats)
        return stats

    init = tuple((jnp.full((1, tq), NEG_BIG, F32), jnp.zeros((1, tq), F32)) for _ in range(2))
    stats = lax.fori_loop(0, i * (n_diag // unroll),
                          lambda jb, st: run_tiles(jb * unroll, unroll, st, False), init)
    stats = run_tiles(i * n_diag, n_diag, stats, True)

    o_t = jnp.where(row < HEAD_DIM, acc_ref[0] * (1.0 / stats[0][1]),
                    acc_ref[1] * (1.0 / stats[1][1]))
    o_ref[...] = o_t.T.astype(BF16)


def _fox_attn(qt, qxt, kf, kx, vt, *, batch, seq, tq, tk, unroll):
    T = kf.shape[0]
    nq = seq // tq
    n_pair = FOX_HEADS // 2
    return pl.pallas_call(
        functools.partial(_fox_kernel, tk=tk, unroll=unroll),
        grid=(batch, n_pair, nq),
        in_specs=[
            pl.BlockSpec((None, None, LANES, tq), lambda b, p, i: (b, p, 0, i)),
            pl.BlockSpec((None, LANES, tq), lambda b, p, i: (b, 0, i)),
            pl.BlockSpec((seq, LANES), lambda b, p, i: (b, p)),
            pl.BlockSpec((seq, LANES), lambda b, p, i: (b, 0)),
            pl.BlockSpec((None, None, LANES, seq), lambda b, p, i: (b, p, 0, 0)),
        ],
        out_specs=pl.BlockSpec((tq, LANES), lambda b, p, i: (b * nq + i, p)),
        out_shape=jax.ShapeDtypeStruct((T, FOX_W), BF16),
        scratch_shapes=[pltpu.VMEM((2, LANES, tq), F32)],
        compiler_params=_cparams(("arbitrary", "arbitrary", "arbitrary")),
        name="fox_attn",
    )(qt, qxt, kf, kx, vt)


def _swa_kernel(sink_ref, q_ref, kc_ref, kp_ref, vc_ref, vp_ref, o_ref):
    i = pl.program_id(1)
    tq = q_ref.shape[0]
    n_win = tq // WINDOW
    n_col = SWA_W // LANES
    lane = lax.broadcasted_iota(jnp.int32, (WINDOW, LANES), 1)
    low = lane < HEAD_DIM
    r = lax.broadcasted_iota(jnp.int32, (WINDOW, 2 * WINDOW), 0)
    c = lax.broadcasted_iota(jnp.int32, (WINDOW, 2 * WINDOW), 1)
    dist = WINDOW + r - c
    band = (dist >= 0) & (dist < WINDOW)
    for w in range(n_win):
        if w == 0:
            k_win = jnp.concatenate([kp_ref[...], kc_ref[0:WINDOW, :]], axis=0)
            v_win = jnp.concatenate([vp_ref[...], vc_ref[0:WINDOW, :]], axis=0)
            first = i == 0
            valid = band & ((c >= WINDOW) | jnp.logical_not(first))
        else:
            k_win = kc_ref[(w - 1) * WINDOW:(w + 1) * WINDOW, :]
            v_win = vc_ref[(w - 1) * WINDOW:(w + 1) * WINDOW, :]
            valid = band
        for col_i in range(n_col):
            qc = q_ref[w * WINDOW:(w + 1) * WINDOW, col_i * LANES:(col_i + 1) * LANES]
            outs = []
            for half in range(2):
                qm = jnp.where(low if half == 0 else jnp.logical_not(low), qc, jnp.zeros_like(qc))
                s = lax.dot_general(qm, k_win, (((1,), (1,)), ((), ())),
                                    preferred_element_type=F32)
                s = jnp.where(valid, s, -jnp.inf)
                sink = sink_ref[2 * col_i + half]
                m = jnp.maximum(jnp.max(s, axis=1, keepdims=True), sink)
                pm = jnp.exp2(s - m)
                denom = jnp.sum(pm, axis=1, keepdims=True) + jnp.exp2(sink - m)
                outs.append(jnp.dot(pm.astype(BF16), v_win, preferred_element_type=F32)
                            * (1.0 / denom))
            o_ref[w * WINDOW:(w + 1) * WINDOW, col_i * LANES:(col_i + 1) * LANES] = (
                jnp.where(low, outs[0], outs[1]).astype(BF16))


def _swa_attn(sinks_perm, qs, ks, vs, *, batch, seq, tq):
    T = qs.shape[0]
    nq = seq // tq
    n_win = tq // WINDOW
    nb = seq // WINDOW
    cur = lambda w: pl.BlockSpec((tq, w), lambda b, i: (b * nq + i, 0))
    prev = pl.BlockSpec((WINDOW, SWA_KV_W),
                        lambda b, i: (jnp.maximum(b * nb + i * n_win - 1, 0), 0))
    return pl.pallas_call(
        _swa_kernel,
        grid=(batch, nq),
        in_specs=[pl.BlockSpec(memory_space=pltpu.SMEM), cur(SWA_W), cur(SWA_KV_W), prev,
                  cur(SWA_KV_W), prev],
        out_specs=cur(SWA_W),
        out_shape=jax.ShapeDtypeStruct((T, SWA_W), BF16),
        compiler_params=_cparams(("arbitrary", "arbitrary")),
        name="swa_attn",
    )(sinks_perm, qs, ks, ks, vs, vs)


ROUTE_E0, ROUTE_E1, ROUTE_G0, ROUTE_G1, ROUTE_R0, ROUTE_R1 = range(6)
ROUTE_ROWS = 8


def _mix_route_kernel(of_ref, os_ref, gf_ref, gs_ref, x_ref, wpf_ref, wps_ref, wout_ref, gn_ref,
                      wr_ref, br_ref, x1_ref, h2_ref, route_ref, route_t_ref, cnt_ref, carry_ref):
    step = pl.program_id(0)
    tm = x_ref.shape[0]

    @pl.when(step == 0)
    def _():
        carry_ref[...] = jnp.zeros_like(carry_ref)

    a = jnp.dot(of_ref[...], wpf_ref[...], preferred_element_type=F32)
    b = jnp.dot(os_ref[...], wps_ref[...], preferred_element_type=F32)
    merged = gf_ref[...].astype(F32) * a + gs_ref[...].astype(F32) * b
    y = jnp.dot(merged.astype(BF16), wout_ref[...], preferred_element_type=F32)
    x1 = x_ref[...] + y
    x1_ref[...] = x1
    ms = jnp.mean(x1 * x1, axis=-1, keepdims=True)
    h2 = (x1 * lax.rsqrt(ms + EPS)) * gn_ref[...]
    h2_ref[...] = h2

    h_hi = h2.astype(BF16)
    h_lo = (h2 - h_hi.astype(F32)).astype(BF16)
    wr = wr_ref[...]
    r_hi = jnp.dot(h_hi, wr, preferred_element_type=F32)
    logits = (r_hi[:, :LANES] + r_hi[:, LANES:]
              + jnp.dot(h_lo, wr[:, :LANES], preferred_element_type=F32)) + br_ref[...]

    lane = lax.broadcasted_iota(jnp.int32, (tm, LANES), 1)
    is_g = lane < N_GROUPS
    gl = jnp.where(is_g, logits, -jnp.inf)
    gmax = jnp.max(gl, axis=1, keepdims=True)
    gexp = jnp.where(is_g, jnp.exp(gl - gmax), 0.0)
    gprob = gexp / jnp.sum(gexp, axis=1, keepdims=True)
    g_w = jnp.max(gprob, axis=1, keepdims=True)
    g_idx = jnp.min(jnp.where(is_g & (gprob == g_w), lane, LANES), axis=1, keepdims=True)

    e_lane = lane - N_GROUPS
    in_grp = (e_lane >= 0) & (e_lane < N_EXPERTS) & ((e_lane >> 3) == g_idx)
    el = jnp.where(in_grp, logits, -jnp.inf)
    v0 = jnp.max(el, axis=1, keepdims=True)
    i0 = jnp.min(jnp.where(in_grp & (el == v0), lane, LANES), axis=1, keepdims=True)
    el1 = jnp.where(lane == i0, -jnp.inf, el)
    v1 = jnp.max(el1, axis=1, keepdims=True)
    i1 = jnp.min(jnp.where(in_grp & (lane != i0) & (el1 == v1), lane, LANES), axis=1, keepdims=True)
    t = jnp.exp(v1 - v0)
    den = 1.0 + t
    gate0 = g_w * (1.0 / den)
    gate1 = g_w * (t / den)
    e0 = i0 - N_GROUPS
    e1 = i1 - N_GROUPS

    oh0 = jnp.where(lane == e0, 1.0, 0.0)
    oh1 = jnp.where(lane == e1, 1.0, 0.0)
    row = lax.broadcasted_iota(jnp.int32, (tm, tm), 0)
    col = lax.broadcasted_iota(jnp.int32, (tm, tm), 1)
    tri = jnp.where(row > col, 1.0, 0.0).astype(BF16)
    cs0 = jnp.sum(oh0, axis=0, keepdims=True)
    cs1 = jnp.sum(oh1, axis=0, keepdims=True)
    carry = carry_ref[0:1, :]
    pre = jnp.dot(tri, jnp.concatenate([oh0, oh1], axis=1).astype(BF16),
                  preferred_element_type=F32)
    pre0 = pre[:, :LANES] + carry
    pre1 = pre[:, LANES:] + (carry + cs0)
    rank0 = jnp.sum(oh0 * pre0, axis=1, keepdims=True)
    rank1 = jnp.sum(oh1 * pre1, axis=1, keepdims=True)
    new_carry = carry + cs0 + cs1
    carry_ref[0:1, :] = new_carry
    cnt_ref[...] = jnp.broadcast_to(new_carry, cnt_ref.shape)

    route = jnp.where(lane == ROUTE_E0, e0.astype(F32), 0.0)
    route = jnp.where(lane == ROUTE_E1, e1.astype(F32), route)
    route = jnp.where(lane == ROUTE_G0, gate0, route)
    route = jnp.where(lane == ROUTE_G1, gate1, route)
    route = jnp.where(lane == ROUTE_R0, rank0, route)
    route = jnp.where(lane == ROUTE_R1, rank1, route)
    route_ref[...] = route
    route_t_ref[...] = route.T[0:route_t_ref.shape[0], :]


def _mix_route(o_f, o_s, g_f, g_s, x2, wpf, wps, wout, g_ffn, wr_cat, b_r, *, tm):
    T, D = x2.shape
    row_blk = lambda w: pl.BlockSpec((tm, w), lambda i: (i, 0))
    const = lambda shape: pl.BlockSpec(shape, lambda i: (0, 0))
    return pl.pallas_call(
        _mix_route_kernel,
        grid=(T // tm,),
        in_specs=[row_blk(FOX_W), row_blk(SWA_W), row_blk(D), row_blk(D), row_blk(D),
                  const((FOX_W, D)), const((SWA_W, D)), const((D, D)), const((1, D)),
                  const((D, 2 * LANES)), const((1, LANES))],
        out_specs=(row_blk(D), row_blk(D), row_blk(LANES),
                   pl.BlockSpec((ROUTE_ROWS, tm), lambda i: (0, i)), const((8, LANES))),
        out_shape=(jax.ShapeDtypeStruct((T, D), F32), jax.ShapeDtypeStruct((T, D), F32),
                   jax.ShapeDtypeStruct((T, LANES), F32), jax.ShapeDtypeStruct((ROUTE_ROWS, T), F32),
                   jax.ShapeDtypeStruct((8, LANES), F32)),
        scratch_shapes=[pltpu.VMEM((8, LANES), F32)],
        compiler_params=_cparams(("arbitrary",)),
        name="mix_route",
    )(o_f, o_s, g_f, g_s, x2, wpf, wps, wout, g_ffn, wr_cat, b_r)


def _plan_kernel(cnt_ref, route_t_ref, dest_ref, tbl_ref, blk_ref):
    lane = lax.broadcasted_iota(jnp.int32, cnt_ref.shape, 1)
    cnt = jnp.where(lane < N_EXPERTS, cnt_ref[...], 0.0)
    padded = jnp.floor((cnt + (ROW_BLOCK - 1)) * (1.0 / ROW_BLOCK)) * ROW_BLOCK
    pad_end = padded
    shift = 1
    while shift < N_EXPERTS:
        pad_end = pad_end + jnp.where(lane >= shift, pltpu.roll(pad_end, shift, axis=1), 0.0)
        shift *= 2
    pad_start = pad_end - padded
    row = lax.broadcasted_iota(jnp.int32, cnt_ref.shape, 0)
    tbl_ref[...] = jnp.where(row == 0, pad_start, pad_end).astype(jnp.int32)

    fields = route_t_ref[...]
    start_of = jnp.zeros_like(fields)
    for e in range(N_EXPERTS):
        start_of = jnp.where(fields == float(e), pad_start[0:1, e:e + 1], start_of)
    ranks = pltpu.roll(fields, ROUTE_ROWS - ROUTE_R0, axis=0)
    dest_ref[...] = (start_of + ranks).astype(jnp.int32)

    ends = pad_end.T[0:N_EXPERTS, 0:1]
    n_col = blk_ref.shape[1]
    blk_pos = (lax.broadcasted_iota(jnp.int32, (N_EXPERTS, n_col), 1) * ROW_BLOCK).astype(F32)
    owner = jnp.minimum(jnp.sum(jnp.where(ends <= blk_pos, 1.0, 0.0), axis=0, keepdims=True),
                        N_EXPERTS - 1.0)
    token_end = (pad_start + cnt).T[0:N_EXPERTS, 0:1]
    expert = lax.broadcasted_iota(jnp.int32, (N_EXPERTS, n_col), 0).astype(F32)
    owner_end = jnp.sum(jnp.where(expert == owner, token_end, 0.0), axis=0, keepdims=True)
    valid = jnp.clip(owner_end - blk_pos[0:1, :], 0.0, float(ROW_BLOCK))
    blk_row = lax.broadcasted_iota(jnp.int32, blk_ref.shape, 0)
    blk_ref[...] = jnp.where(blk_row == 0, owner, valid).astype(jnp.int32)


def _plan(counts, route_t, n_blk):
    T = route_t.shape[1]
    n_col = (n_blk + LANES - 1) // LANES * LANES
    full = lambda shape: pl.BlockSpec(shape, lambda i: (0, 0))
    return pl.pallas_call(
        _plan_kernel,
        grid=(1,),
        in_specs=[full((8, LANES)), full((ROUTE_ROWS, T))],
        out_specs=(full((ROUTE_ROWS, T)), full((8, LANES)), full((8, n_col))),
        out_shape=(jax.ShapeDtypeStruct((ROUTE_ROWS, T), jnp.int32),
                   jax.ShapeDtypeStruct((8, LANES), jnp.int32),
                   jax.ShapeDtypeStruct((8, n_col), jnp.int32)),
        compiler_params=_cparams(("arbitrary",)),
        name="plan",
    )(counts, route_t)


def _dispatch_kernel(bounds_ref, dest0_ref, dest1_ref, h_ref, xs_ref, zbuf, sem, zsem):
    step = pl.program_id(0)
    tb = h_ref.shape[0]

    @pl.when(step == 0)
    def _():
        zbuf[...] = jnp.zeros_like(zbuf)

        def tail_copy(e):
            tail = pl.multiple_of(bounds_ref[e + 1] - ROW_BLOCK, ROW_BLOCK)
            return pltpu.make_async_copy(zbuf, xs_ref.at[pl.ds(tail, ROW_BLOCK), :], zsem)

        def for_nonempty(action):
            def body(e, carry):
                @pl.when(bounds_ref[e + 1] > bounds_ref[e])
                def _():
                    action(tail_copy(e))
                return carry
            lax.fori_loop(0, N_EXPERTS, body, 0)

        n_rows = xs_ref.shape[0]
        first_unused = bounds_ref[N_EXPERTS] // ROW_BLOCK

        def unused_copy(blk):
            return pltpu.make_async_copy(
                zbuf, xs_ref.at[pl.ds(pl.multiple_of(blk * ROW_BLOCK, ROW_BLOCK), ROW_BLOCK), :], zsem)

        def for_unused(action):
            def body(blk, carry):
                action(unused_copy(blk))
                return carry
            lax.fori_loop(first_unused, n_rows // ROW_BLOCK, body, 0)

        for_nonempty(lambda cp: cp.start())
        for_unused(lambda cp: cp.start())
        for_nonempty(lambda cp: cp.wait())
        for_unused(lambda cp: cp.wait())

    def start(j, carry):
        base = pl.multiple_of(j * SUBLANES, SUBLANES)
        tile = h_ref.at[pl.ds(base, SUBLANES), :]
        for u in range(SUBLANES):
            for k, dest_ref in enumerate((dest0_ref, dest1_ref)):
                d = dest_ref[base + u]
                pltpu.make_async_copy(tile.at[pl.ds(u, 1), :], xs_ref.at[pl.ds(d, 1), :],
                                      sem).start(priority=k)
        return carry

    lax.fori_loop(0, tb // SUBLANES, start, 0)
    for _ in range(2):
        pltpu.make_async_copy(h_ref, xs_ref.at[pl.ds(0, tb), :], sem).wait()


def _dispatch(bounds, dest0, dest1, h2, n_rows, *, tb):
    T, D = h2.shape
    smem_blk = pl.BlockSpec((tb,), lambda i, bd: (i,), memory_space=pltpu.SMEM)
    grid_spec = pltpu.PrefetchScalarGridSpec(
        num_scalar_prefetch=1,
        grid=(T // tb,),
        in_specs=[smem_blk, smem_blk, pl.BlockSpec((tb, D), lambda i, bd: (i, 0))],
        out_specs=pl.BlockSpec(memory_space=pl.ANY),
        scratch_shapes=[pltpu.VMEM((ROW_BLOCK, D), F32), pltpu.SemaphoreType.DMA(()),
                        pltpu.SemaphoreType.DMA(())],
    )
    return pl.pallas_call(
        _dispatch_kernel,
        grid_spec=grid_spec,
        out_shape=jax.ShapeDtypeStruct((n_rows, D), F32),
        compiler_params=_cparams(("arbitrary",)),
        name="dispatch",
    )(bounds, dest0, dest1, h2)


def _experts_kernel(be_ref, nu_ref, rows_ref, xs_ref, w1_ref, w3_ref, w2_ref, ys_ref,
                    w1f, w3f, w2f, w1b, w3b, w2b, region_ref, wsem):
    i = pl.program_id(0)
    n_used = nu_ref[0]
    e = be_ref[i]
    used = i < n_used
    first = (i == 0) | (e != be_ref[jnp.maximum(i - 1, 0)])

    def weight_copies(expert, slot):
        return (pltpu.make_async_copy(w1_ref.at[expert], w1f.at[slot], wsem.at[slot]),
                pltpu.make_async_copy(w3_ref.at[expert], w3f.at[slot], wsem.at[slot]),
                pltpu.make_async_copy(w2_ref.at[expert], w2f.at[slot], wsem.at[slot]))

    @pl.when(i == 0)
    def _():
        region_ref[0] = 0
        for cp in weight_copies(e, 0):
            cp.start()

    @pl.when(used & first)
    def _():
        slot = region_ref[0] % 2
        region_ref[0] = region_ref[0] + 1
        for cp in weight_copies(e, slot):
            cp.wait()
        w1b[...] = w1f[slot].astype(BF16)
        w3b[...] = w3f[slot].astype(BF16)
        w2b[...] = w2f[slot].astype(BF16)
        nxt = lax.while_loop(lambda j: (j < n_used) & (be_ref[jnp.minimum(j, n_used - 1)] == e),
                             lambda j: j + 1, i + 1)

        @pl.when(nxt < n_used)
        def _():
            for cp in weight_copies(be_ref[jnp.minimum(nxt, n_used - 1)], 1 - slot):
                cp.start()

    def swiglu(n_rows):
        xb = xs_ref[0:n_rows, :].astype(BF16)
        de = w1b.shape[1]
        n_part = 2
        part = de // n_part
        acts = []
        for q in range(n_part):
            cols = slice(q * part, (q + 1) * part)
            h1 = jnp.dot(xb, w1b[:, cols], preferred_element_type=F32)
            h3 = jnp.dot(xb, w3b[:, cols], preferred_element_type=F32)
            acts.append(((h1 * jax.nn.sigmoid(h1)) * h3).astype(BF16))
        y = jnp.dot(acts[0], w2b[0:part, :], preferred_element_type=F32)
        for q in range(1, n_part):
            y = y + jnp.dot(acts[q], w2b[q * part:(q + 1) * part, :], preferred_element_type=F32)
        ys_ref[0:n_rows, :] = y

    n_rows = ys_ref.shape[0]
    half_only = rows_ref[i] <= n_rows // 2

    @pl.when(used & jnp.logical_not(half_only))
    def _():
        swiglu(n_rows)

    @pl.when(used & half_only)
    def _():
        swiglu(n_rows // 2)
        ys_ref[n_rows // 2:, :] = jnp.zeros((n_rows - n_rows // 2, ys_ref.shape[1]), F32)

    @pl.when(jnp.logical_not(used))
    def _():
        ys_ref[...] = jnp.zeros_like(ys_ref)


def _experts(blk_expert, n_used, blk_rows, xs, w1, w3, w2):
    P, D = xs.shape
    de = w1.shape[-1]
    n_blk = P // ROW_BLOCK
    grid_spec = pltpu.PrefetchScalarGridSpec(
        num_scalar_prefetch=3,
        grid=(n_blk,),
        in_specs=[pl.BlockSpec((ROW_BLOCK, D),
                               lambda i, be, nu, br: (jnp.minimum(i, jnp.maximum(nu[0] - 1, 0)), 0)),
                  pl.BlockSpec(memory_space=pl.ANY), pl.BlockSpec(memory_space=pl.ANY),
                  pl.BlockSpec(memory_space=pl.ANY)],
        out_specs=pl.BlockSpec((ROW_BLOCK, D), lambda i, be, nu, br: (i, 0)),
        scratch_shapes=[pltpu.VMEM((2, D, de), F32), pltpu.VMEM((2, D, de), F32),
                        pltpu.VMEM((2, de, D), F32),
                        pltpu.VMEM((D, de), BF16), pltpu.VMEM((D, de), BF16),
                        pltpu.VMEM((de, D), BF16),
                        pltpu.SMEM((1,), jnp.int32), pltpu.SemaphoreType.DMA((2,))],
    )
    return pl.pallas_call(
        _experts_kernel,
        grid_spec=grid_spec,
        out_shape=jax.ShapeDtypeStruct((P, D), F32),
        compiler_params=_cparams(("arbitrary",)),
        name="experts",
    )(blk_expert, n_used, blk_rows, xs, w1, w3, w2)


def _combine_kernel(d0_cur, d1_cur, d0_next, d1_next, x1_ref, route_ref, gfin_ref, ys_ref, o_ref,
                    ybuf, sem):
    i = pl.program_id(0)
    tb = x1_ref.shape[0]
    slot = i % 2

    def issue(dest_refs, buf_slot):
        def start(j, carry):
            base = pl.multiple_of(j * SUBLANES, SUBLANES)
            for k, dest_ref in enumerate(dest_refs):
                tile = ybuf.at[buf_slot, k, pl.ds(base, SUBLANES), :]
                for u in range(SUBLANES):
                    d = dest_ref[base + u]
                    pltpu.make_async_copy(ys_ref.at[pl.ds(d, 1), :], tile.at[pl.ds(u, 1), :],
                                          sem.at[buf_slot]).start(priority=k)
            return carry
        lax.fori_loop(0, tb // SUBLANES, start, 0)

    @pl.when(i == 0)
    def _():
        issue((d0_cur, d1_cur), 0)

    @pl.when(i + 1 < pl.num_programs(0))
    def _():
        issue((d0_next, d1_next), 1 - slot)

    for k in range(2):
        pltpu.make_async_copy(ys_ref.at[pl.ds(0, tb), :], ybuf.at[slot, k], sem.at[slot]).wait()

    route = route_ref[...]
    lane = lax.broadcasted_iota(jnp.int32, route.shape, 1)
    g0 = jnp.sum(jnp.where(lane == ROUTE_G0, route, 0.0), axis=1, keepdims=True)
    g1 = jnp.sum(jnp.where(lane == ROUTE_G1, route, 0.0), axis=1, keepdims=True)
    moe = g0 * ybuf[slot, 0] + g1 * ybuf[slot, 1]
    x2 = x1_ref[...] + moe
    ms = jnp.mean(x2 * x2, axis=-1, keepdims=True)
    o_ref[...] = (x2 * lax.rsqrt(ms + EPS)) * gfin_ref[...]


def _combine(dest0, dest1, x1, route, g_final, ys, *, tb):
    T, D = x1.shape
    nb = T // tb
    cur = pl.BlockSpec((tb,), lambda i: (i,), memory_space=pltpu.SMEM)
    nxt = pl.BlockSpec((tb,), lambda i: (jnp.minimum(i + 1, nb - 1),), memory_space=pltpu.SMEM)
    return pl.pallas_call(
        _combine_kernel,
        grid=(nb,),
        in_specs=[cur, cur, nxt, nxt,
                  pl.BlockSpec((tb, D), lambda i: (i, 0)),
                  pl.BlockSpec((tb, LANES), lambda i: (i, 0)),
                  pl.BlockSpec((1, D), lambda i: (0, 0)),
                  pl.BlockSpec(memory_space=pl.ANY)],
        out_specs=pl.BlockSpec((tb, D), lambda i: (i, 0)),
        out_shape=jax.ShapeDtypeStruct((T, D), F32),
        scratch_shapes=[pltpu.VMEM((2, 2, tb, D), F32), pltpu.SemaphoreType.DMA((2,))],
        compiler_params=_cparams(("arbitrary",)),
        name="combine",
    )(dest0, dest1, dest0, dest1, x1, route, g_final, ys)


def _pick(n, prefs):
    for p in prefs:
        if n % p == 0:
            return p
    return n


def _layer(x2, pos2, batch, seq, attn_norm, w_in, b_forget, b_gate, sinks, w_proj_fox, w_proj_swa,
           w_out, ffn_norm, w_group, b_group, w_expert, b_expert, w1, w3, w2, final_norm):
    T, D = x2.shape
    scale = HEAD_DIM ** -0.5

    sp = np.cumsum([FOX_W, FOX_W, FOX_W, FOX_HEADS, SWA_W, SWA_KV_W, SWA_KV_W, D, D])[:-1]
    wq_f, wk_f, wv_f, w_fl, wq_s, wk_s, wv_s, wg_f, wg_s = jnp.split(w_in, sp, axis=1)
    head_order = np.asarray(SWA_HEAD_ORDER)
    wq_s = wq_s.reshape(D, SWA_HEADS, HEAD_DIM)[:, head_order, :].reshape(D, SWA_W)
    pieces = (wq_f * (scale * LOG2E), wk_f, wv_f, wq_s * (scale * LOG2E),
              jnp.concatenate([wk_s, wv_s], axis=1),
              wg_f, wg_s, jnp.pad(w_fl, ((0, 0), (0, LANES - FOX_HEADS))))
    weights = {name: piece.astype(BF16) for name, piece in zip(W_NAMES, pieces)}
    half = HEAD_DIM // 2
    inv_freq = ROPE_THETA ** (-jnp.arange(half, dtype=F32) * 2.0 / HEAD_DIM)
    invf = jnp.tile(inv_freq, LANES // half).reshape(1, LANES)
    bf_pad = jnp.pad(b_forget.astype(F32), (0, LANES - FOX_HEADS)).reshape(1, LANES)
    sinks_perm = sinks.astype(F32)[head_order] * LOG2E
    wps = w_proj_swa.reshape(SWA_HEADS, HEAD_DIM, D)[head_order].reshape(SWA_W, D).astype(BF16)
    wpf = w_proj_fox.astype(BF16)
    wout = w_out.astype(BF16)
    w_r = jnp.pad(jnp.concatenate([w_group, w_expert], axis=1).astype(F32),
                  ((0, 0), (0, LANES - N_GROUPS - N_EXPERTS)))
    wr_hi = w_r.astype(BF16)
    wr_lo = (w_r - wr_hi.astype(F32)).astype(BF16)
    wr_cat = jnp.concatenate([wr_hi, wr_lo], axis=1)
    b_r = jnp.pad(jnp.concatenate([b_group, b_expert]).astype(F32),
                  (0, LANES - N_GROUPS - N_EXPERTS)).reshape(1, LANES)

    tm1 = _pick(seq, (512, 256, 128))
    qt, kf, vt, qs, ks, vs, g_f, g_s, qxt, kx = _in_proj(
        x2, pos2, attn_norm.reshape(1, D), weights, invf, bf_pad, b_gate.astype(F32),
        batch=batch, seq=seq, tm=tm1)
    tq = _pick(seq, (FOX_TQ, 128))
    tk = min(tq, FOX_TK)
    o_f = _fox_attn(qt, qxt, kf, kx, vt, batch=batch, seq=seq, tq=tq, tk=tk,
                    unroll=min(FOX_UNROLL, tq // tk))
    o_s = _swa_attn(sinks_perm, qs, ks, vs, batch=batch, seq=seq, tq=_pick(seq, (512, 256, 128)))

    tm3 = _pick(T, (512, 256, 128))
    x1, h2, route, route_t, counts = _mix_route(o_f, o_s, g_f, g_s, x2, wpf, wps, wout,
                                       ffn_norm.reshape(1, D), wr_cat, b_r, tm=tm3)

    P = 2 * T + N_EXPERTS * ROW_BLOCK
    n_blk = P // ROW_BLOCK
    dest, tbl, blk = _plan(counts, route_t, n_blk)
    dest0, dest1 = dest[ROUTE_E0], dest[ROUTE_E1]
    pad_end = tbl[1, :N_EXPERTS]
    bounds = jnp.concatenate([jnp.zeros((1,), jnp.int32), pad_end])
    blk_expert, blk_rows = blk[0, :n_blk], blk[1, :n_blk]
    n_used = pad_end[-1:] // ROW_BLOCK

    xs = _dispatch(bounds, dest0, dest1, h2, P, tb=_pick(T, (2048, 1024, 512, 256, 128)))
    ys = _experts(blk_expert, n_used, blk_rows, xs, w1, w3, w2)
    return _combine(dest0, dest1, x1, route, final_norm.reshape(1, D), ys,
                    tb=_pick(T, (1024, 512, 256, 128)))


def kernel(x, positions, attn_norm, w_in, b_forget, b_gate, attn_sinks, w_proj_fox, w_proj_swa,
           w_out, ffn_norm, w_group, b_group, w_expert, b_expert, w1, w3, w2, final_norm):
    B, S, D = x.shape
    depth = attn_norm.shape[0]
    x2 = x.reshape(B * S, D)
    pos2 = positions.reshape(B * S, 1).astype(jnp.int32)
    assert depth == 1, "a single layer is followed directly by the final norm"
    out = _layer(x2, pos2, B, S, attn_norm[0], w_in[0], b_forget[0], b_gate[0], attn_sinks[0],
                 w_proj_fox[0], w_proj_swa[0], w_out[0], ffn_norm[0], w_group[0], b_group[0],
                 w_expert[0], b_expert[0], w1[0], w3[0], w2[0], final_norm)
    return out.reshape(B, S, D)
```

```python
import functools

import numpy as np
import jax
import jax.numpy as jnp
from jax import lax
from jax.experimental import pallas as pl
from jax.experimental.pallas import tpu as pltpu

HEAD_DIM = 64
FOX_HEADS = 8
SWA_HEADS = 8
SWA_KV_HEADS = 2
WINDOW = 128
ROPE_THETA = 10000.0
N_GROUPS = 8
EXPERTS_PER_GROUP = 8
N_EXPERTS = N_GROUPS * EXPERTS_PER_GROUP
ROW_BLOCK = 512
SUBLANES = 8
EPS = 1e-6
LANES = 128
NEG_BIG = -1e30
LOG2E = 1.4426950408889634
BIAS_TERMS = 3
BIAS_LANES = 2 * BIAS_TERMS
FOX_TQ = 1024
FOX_TK = 256
FOX_UNROLL = 4
ONES_ROWS = 16

FOX_W = FOX_HEADS * HEAD_DIM
SWA_W = SWA_HEADS * HEAD_DIM
SWA_KV_W = SWA_KV_HEADS * HEAD_DIM
SWA_HEAD_ORDER = (0, 4, 1, 5, 2, 6, 3, 7)

F32 = jnp.float32
BF16 = jnp.bfloat16
VMEM_LIMIT = 56 * 1024 * 1024


def _cparams(sem):
    return pltpu.CompilerParams(dimension_semantics=sem, vmem_limit_bytes=VMEM_LIMIT)


def _split3(x):
    hi = x.astype(BF16)
    r1 = x - hi.astype(F32)
    mid = r1.astype(BF16)
    lo = (r1 - mid.astype(F32)).astype(BF16)
    return hi, mid, lo


W_NAMES = ("q_fox", "k_fox", "v_fox", "q_swa", "kv_swa", "gate_fox", "gate_swa", "forget")


def _in_proj_kernel(x_ref, pos_ref, g_ref, invf_ref, bf_ref, bg_ref, place_ref, ones_ref, *refs):
    w = dict(zip(W_NAMES, refs[:len(W_NAMES)]))
    (qf_ref, kf_ref, vf_ref, qs_ref, ks_ref, vs_ref, gf_ref, gs_ref, qx_ref, kx_ref,
     carry_ref) = refs[len(W_NAMES):]
    j = pl.program_id(1)
    tm = x_ref.shape[0]

    @pl.when(j == 0)
    def _():
        carry_ref[...] = jnp.zeros_like(carry_ref)

    x = x_ref[...]
    ms = jnp.mean(x * x, axis=-1, keepdims=True)
    h = ((x * lax.rsqrt(ms + EPS)) * g_ref[...]).astype(BF16)

    def mm(name):
        return jnp.dot(h, w[name][...], preferred_element_type=F32)

    def store_pairs_t(ref, y):
        for pr in range(ref.shape[0]):
            ref[pr] = y[:, pr * LANES:(pr + 1) * LANES].T.astype(BF16)

    store_pairs_t(qf_ref, mm("q_fox"))
    kf_ref[...] = mm("k_fox").astype(BF16)
    store_pairs_t(vf_ref, mm("v_fox"))

    ang = pos_ref[...].astype(F32) * invf_ref[...]
    cos = jnp.cos(ang)
    sin = jnp.sin(ang)
    n_rep = SWA_W // LANES
    cos_q = jnp.concatenate([cos] * n_rep, axis=1)
    sin_q = jnp.concatenate([sin] * n_rep, axis=1)
    def rot_half(y):
        width = y.shape[1]
        lane_y = lax.broadcasted_iota(jnp.int32, y.shape, 1)
        first = (lane_y & (HEAD_DIM - 1)) < HEAD_DIM // 2
        return jnp.where(first, -pltpu.roll(y, width - HEAD_DIM // 2, axis=1),
                         pltpu.roll(y, HEAD_DIM // 2, axis=1))

    q_s = mm("q_swa")
    kv_s = mm("kv_swa")
    k_s = kv_s[:, :SWA_KV_W]
    qs_ref[...] = (q_s * cos_q + rot_half(q_s) * sin_q).astype(BF16)
    ks_ref[...] = (k_s * cos + rot_half(k_s) * sin).astype(BF16)
    vs_ref[...] = kv_s[:, SWA_KV_W:].astype(BF16)

    gf_ref[...] = jax.nn.sigmoid(mm("gate_fox") + bg_ref[0:1, :]).astype(BF16)
    gs_ref[...] = jax.nn.sigmoid(mm("gate_swa") + bg_ref[1:2, :]).astype(BF16)

    z = mm("forget") + bf_ref[...]
    lane = lax.broadcasted_iota(jnp.int32, (tm, LANES), 1)
    lf = jnp.minimum(z, 0.0) - jnp.log1p(jnp.exp(-jnp.abs(z)))
    lf = jnp.where(lane < FOX_HEADS, lf, 0.0)
    row = lax.broadcasted_iota(jnp.int32, (tm, tm), 0)
    col = lax.broadcasted_iota(jnp.int32, (tm, tm), 1)
    tri = jnp.where(row >= col, 1.0, 0.0).astype(BF16)
    hi, mid, lo = _split3(lf)
    sums = jnp.dot(tri, jnp.concatenate([hi, mid, lo], axis=1), preferred_element_type=F32)
    c = (sums[:, :LANES] + sums[:, LANES:2 * LANES] + sums[:, 2 * LANES:]) + carry_ref[0:1, :]
    carry_ref[0:1, :] = c[tm - 1:tm, :]
    chi, cmid, clo = _split3(c * LOG2E)
    ext = jnp.dot(jnp.concatenate([chi, cmid, clo], axis=1), place_ref[...],
                  preferred_element_type=F32) + ones_ref[...]
    qx_ref[...] = ext[:, :LANES].T.astype(BF16)
    kx_ref[...] = ext[:, LANES:].astype(BF16)


def _bias_placement():
    place = np.zeros((BIAS_TERMS * LANES, 2 * LANES), np.float32)
    ones = np.zeros((1, 2 * LANES), np.float32)
    for h in range(FOX_HEADS):
        for t in range(BIAS_TERMS):
            place[t * LANES + h, BIAS_LANES * h + t] = 1.0
            place[t * LANES + h, LANES + BIAS_LANES * h + BIAS_TERMS + t] = -1.0
            ones[0, BIAS_LANES * h + BIAS_TERMS + t] = 1.0
            ones[0, LANES + BIAS_LANES * h + t] = 1.0
    return jnp.asarray(place, BF16), jnp.asarray(ones, F32)


def _in_proj(x2, pos2, g_attn, weights, invf, bf_pad, b_gate, *, batch, seq, tm):
    T, D = x2.shape
    nj = seq // tm
    w_list = [weights[name] for name in W_NAMES]
    place, ones = _bias_placement()
    row_blk = lambda w: pl.BlockSpec((tm, w), lambda b, j: (b * nj + j, 0))
    const = lambda shape: pl.BlockSpec(shape, lambda b, j: (0, 0))
    n_pair = FOX_HEADS // 2
    pair_t = pl.BlockSpec((None, n_pair, LANES, tm), lambda b, j: (b, 0, 0, j))
    out_shapes = (
        jax.ShapeDtypeStruct((batch, n_pair, LANES, seq), BF16), jax.ShapeDtypeStruct((T, FOX_W), BF16),
        jax.ShapeDtypeStruct((batch, n_pair, LANES, seq), BF16), jax.ShapeDtypeStruct((T, SWA_W), BF16),
        jax.ShapeDtypeStruct((T, SWA_KV_W), BF16), jax.ShapeDtypeStruct((T, SWA_KV_W), BF16),
        jax.ShapeDtypeStruct((T, D), BF16), jax.ShapeDtypeStruct((T, D), BF16),
        jax.ShapeDtypeStruct((batch, LANES, seq), BF16), jax.ShapeDtypeStruct((T, LANES), BF16),
    )
    return pl.pallas_call(
        _in_proj_kernel,
        grid=(batch, nj),
        in_specs=[row_blk(D), row_blk(1), const((1, D)), const((1, LANES)),
                  const((1, LANES)), const((2, D)), const(place.shape), const(ones.shape)]
                 + [const(wt.shape) for wt in w_list],
        out_specs=(pair_t, row_blk(FOX_W), pair_t, row_blk(SWA_W),
                   row_blk(SWA_KV_W), row_blk(SWA_KV_W), row_blk(D), row_blk(D),
                   pl.BlockSpec((None, LANES, tm), lambda b, j: (b, 0, j)), row_blk(LANES)),
        out_shape=out_shapes,
        scratch_shapes=[pltpu.VMEM((8, LANES), F32)],
        compiler_params=_cparams(("arbitrary", "arbitrary")),
        name="in_proj",
    )(x2, pos2, g_attn, invf, bf_pad, b_gate, place, ones, *w_list)


def _fox_kernel(q_ref, qx_ref, k_ref, kx_ref, vt_ref, o_ref, acc_ref, *, tk, unroll):
    p = pl.program_id(1)
    i = pl.program_id(2)
    tq = q_ref.shape[1]
    n_diag = tq // tk
    qt = q_ref[...]
    qxt = qx_ref[...]
    row = lax.broadcasted_iota(jnp.int32, (LANES, tq), 0)
    zero = jnp.zeros_like(qt)
    q_ext = []
    for hh in range(2):
        first = BIAS_LANES * (2 * p + hh)
        head_rows = (row < HEAD_DIM) if hh == 0 else (row >= HEAD_DIM)
        bias_rows = (row >= first) & (row < first + BIAS_LANES)
        q_ext.append(jnp.concatenate([jnp.where(head_rows, qt, zero),
                                      jnp.where(bias_rows, qxt, zero)], axis=0))

    acc_ref[...] = jnp.zeros_like(acc_ref)

    def scores(ks, c0, masked):
        k_ext = jnp.concatenate([k_ref[pl.ds(ks, tk), :], kx_ref[pl.ds(ks, tk), :]], axis=1)
        out = []
        for hh in range(2):
            s = jnp.dot(k_ext, q_ext[hh][:, c0:], preferred_element_type=F32)
            if masked:
                key = lax.broadcasted_iota(jnp.int32, s.shape, 0)
                qry = lax.broadcasted_iota(jnp.int32, s.shape, 1)
                s = jnp.where(key <= qry, s, -jnp.inf)
            out.append(s)
        return out

    ones_rows = jnp.ones((ONES_ROWS, tk), BF16)

    def accumulate(ks, c0, s_pair, stats):
        vt = jnp.concatenate([vt_ref[:, pl.ds(ks, tk)], ones_rows], axis=0)
        out = []
        for hh in range(2):
            m_all = stats[hh]
            m_old = m_all[:, c0:]
            s = s_pair[hh]
            m_new = jnp.maximum(m_old, jnp.max(s, axis=0, keepdims=True))
            alpha = jnp.exp2(m_old - m_new)
            pm = jnp.exp2(s - m_new)
            acc_ref[hh, :, c0:] = alpha * acc_ref[hh, :, c0:] + jnp.dot(
                vt, pm.astype(BF16), preferred_element_type=F32)
            if c0:
                m_new = jnp.concatenate([m_all[:, :c0], m_new], axis=1)
            out.append(m_new)
        return tuple(out)

    def run_tiles(first_tile, count, stats, diagonal):
        start = lambda u: pl.multiple_of((first_tile + u) * tk, tk)
        col0 = lambda u: u * tk if diagonal else 0
        s_next = scores(start(0), col0(0), diagonal)
        for u in range(count):
            s_cur = s_next
            if u + 1 < count:
                s_next = scores(start(u + 1), col0(u + 1), diagonal)
            stats = accumulate(start(u), col0(u), s_cur, stats)
        return stats

    init = tuple(jnp.full((1, tq), NEG_BIG, F32) for _ in range(2))
    stats = lax.fori_loop(0, i * (n_diag // unroll),
                          lambda jb, st: run_tiles(jb * unroll, unroll, st, False), init)
    stats = run_tiles(i * n_diag, n_diag, stats, True)

    del stats
    o_t = jnp.where(row < HEAD_DIM, acc_ref[0, 0:LANES, :] * (1.0 / acc_ref[0, LANES:LANES + 1, :]),
                    acc_ref[1, 0:LANES, :] * (1.0 / acc_ref[1, LANES:LANES + 1, :]))
    o_ref[...] = o_t.T.astype(BF16)


def _fox_attn(qt, qxt, kf, kx, vt, *, batch, seq, tq, tk, unroll):
    T = kf.shape[0]
    nq = seq // tq
    n_pair = FOX_HEADS // 2
    return pl.pallas_call(
        functools.partial(_fox_kernel, tk=tk, unroll=unroll),
        grid=(batch, n_pair, nq),
        in_specs=[
            pl.BlockSpec((None, None, LANES, tq), lambda b, p, i: (b, p, 0, i)),
            pl.BlockSpec((None, LANES, tq), lambda b, p, i: (b, 0, i)),
            pl.BlockSpec((seq, LANES), lambda b, p, i: (b, p)),
            pl.BlockSpec((seq, LANES), lambda b, p, i: (b, 0)),
            pl.BlockSpec((None, None, LANES, seq), lambda b, p, i: (b, p, 0, 0)),
        ],
        out_specs=pl.BlockSpec((tq, LANES), lambda b, p, i: (b * nq + i, p)),
        out_shape=jax.ShapeDtypeStruct((T, FOX_W), BF16),
        scratch_shapes=[pltpu.VMEM((2, LANES + ONES_ROWS, tq), F32)],
        compiler_params=_cparams(("arbitrary", "arbitrary", "arbitrary")),
        name="fox_attn",
    )(qt, qxt, kf, kx, vt)


def _swa_kernel(sink_ref, q_ref, kc_ref, kp_ref, vc_ref, vp_ref, o_ref):
    i = pl.program_id(1)
    tq = q_ref.shape[0]
    n_win = tq // WINDOW
    n_col = SWA_W // LANES
    lane = lax.broadcasted_iota(jnp.int32, (WINDOW, LANES), 1)
    low = lane < HEAD_DIM
    r = lax.broadcasted_iota(jnp.int32, (WINDOW, 2 * WINDOW), 0)
    c = lax.broadcasted_iota(jnp.int32, (WINDOW, 2 * WINDOW), 1)
    dist = WINDOW + r - c
    band = (dist >= 0) & (dist < WINDOW)
    for w in range(n_win):
        if w == 0:
            k_win = jnp.concatenate([kp_ref[...], kc_ref[0:WINDOW, :]], axis=0)
            v_win = jnp.concatenate([vp_ref[...], vc_ref[0:WINDOW, :]], axis=0)
            first = i == 0
            valid = band & ((c >= WINDOW) | jnp.logical_not(first))
        else:
            k_win = kc_ref[(w - 1) * WINDOW:(w + 1) * WINDOW, :]
            v_win = vc_ref[(w - 1) * WINDOW:(w + 1) * WINDOW, :]
            valid = band
        for col_i in range(n_col):
            qc = q_ref[w * WINDOW:(w + 1) * WINDOW, col_i * LANES:(col_i + 1) * LANES]
            outs = []
            for half in range(2):
                qm = jnp.where(low if half == 0 else jnp.logical_not(low), qc, jnp.zeros_like(qc))
                s = lax.dot_general(qm, k_win, (((1,), (1,)), ((), ())),
                                    preferred_element_type=F32)
                s = jnp.where(valid, s, -jnp.inf)
                sink = sink_ref[2 * col_i + half]
                m = jnp.maximum(jnp.max(s, axis=1, keepdims=True), sink)
                pm = jnp.exp2(s - m)
                denom = jnp.sum(pm, axis=1, keepdims=True) + jnp.exp2(sink - m)
                outs.append(jnp.dot(pm.astype(BF16), v_win, preferred_element_type=F32)
                            * (1.0 / denom))
            o_ref[w * WINDOW:(w + 1) * WINDOW, col_i * LANES:(col_i + 1) * LANES] = (
                jnp.where(low, outs[0], outs[1]).astype(BF16))


def _swa_attn(sinks_perm, qs, ks, vs, *, batch, seq, tq):
    T = qs.shape[0]
    nq = seq // tq
    n_win = tq // WINDOW
    nb = seq // WINDOW
    cur = lambda w: pl.BlockSpec((tq, w), lambda b, i: (b * nq + i, 0))
    prev = pl.BlockSpec((WINDOW, SWA_KV_W),
                        lambda b, i: (jnp.maximum(b * nb + i * n_win - 1, 0), 0))
    return pl.pallas_call(
        _swa_kernel,
        grid=(batch, nq),
        in_specs=[pl.BlockSpec(memory_space=pltpu.SMEM), cur(SWA_W), cur(SWA_KV_W), prev,
                  cur(SWA_KV_W), prev],
        out_specs=cur(SWA_W),
        out_shape=jax.ShapeDtypeStruct((T, SWA_W), BF16),
        compiler_params=_cparams(("arbitrary", "arbitrary")),
        name="swa_attn",
    )(sinks_perm, qs, ks, ks, vs, vs)


ROUTE_E0, ROUTE_E1, ROUTE_G0, ROUTE_G1, ROUTE_R0, ROUTE_R1 = range(6)
ROUTE_ROWS = 8


def _mix_route_kernel(of_ref, os_ref, gf_ref, gs_ref, x_ref, wpf_ref, wps_ref, wout_ref, gn_ref,
                      wr_ref, br_ref, x1_ref, h2_ref, route_ref, route_t_ref, cnt_ref, carry_ref):
    step = pl.program_id(0)
    tm = x_ref.shape[0]

    @pl.when(step == 0)
    def _():
        carry_ref[...] = jnp.zeros_like(carry_ref)

    a = jnp.dot(of_ref[...], wpf_ref[...], preferred_element_type=F32)
    b = jnp.dot(os_ref[...], wps_ref[...], preferred_element_type=F32)
    merged = gf_ref[...].astype(F32) * a + gs_ref[...].astype(F32) * b
    y = jnp.dot(merged.astype(BF16), wout_ref[...], preferred_element_type=F32)
    x1 = x_ref[...] + y
    x1_ref[...] = x1
    ms = jnp.mean(x1 * x1, axis=-1, keepdims=True)
    h2 = (x1 * lax.rsqrt(ms + EPS)) * gn_ref[...]
    h2_ref[...] = h2

    h_hi = h2.astype(BF16)
    h_lo = (h2 - h_hi.astype(F32)).astype(BF16)
    wr = wr_ref[...]
    r_hi = jnp.dot(h_hi, wr, preferred_element_type=F32)
    logits = (r_hi[:, :LANES] + r_hi[:, LANES:]
              + jnp.dot(h_lo, wr[:, :LANES], preferred_element_type=F32)) + br_ref[...]

    lane = lax.broadcasted_iota(jnp.int32, (tm, LANES), 1)
    is_g = lane < N_GROUPS
    gl = jnp.where(is_g, logits, -jnp.inf)
    gmax = jnp.max(gl, axis=1, keepdims=True)
    gexp = jnp.where(is_g, jnp.exp(gl - gmax), 0.0)
    gprob = gexp / jnp.sum(gexp, axis=1, keepdims=True)
    g_w = jnp.max(gprob, axis=1, keepdims=True)
    g_idx = jnp.min(jnp.where(is_g & (gprob == g_w), lane, LANES), axis=1, keepdims=True)

    e_lane = lane - N_GROUPS
    in_grp = (e_lane >= 0) & (e_lane < N_EXPERTS) & ((e_lane >> 3) == g_idx)
    el = jnp.where(in_grp, logits, -jnp.inf)
    v0 = jnp.max(el, axis=1, keepdims=True)
    i0 = jnp.min(jnp.where(in_grp & (el == v0), lane, LANES), axis=1, keepdims=True)
    el1 = jnp.where(lane == i0, -jnp.inf, el)
    v1 = jnp.max(el1, axis=1, keepdims=True)
    i1 = jnp.min(jnp.where(in_grp & (lane != i0) & (el1 == v1), lane, LANES), axis=1, keepdims=True)
    t = jnp.exp(v1 - v0)
    den = 1.0 + t
    gate0 = g_w * (1.0 / den)
    gate1 = g_w * (t / den)
    e0 = i0 - N_GROUPS
    e1 = i1 - N_GROUPS

    oh0 = jnp.where(lane == e0, 1.0, 0.0)
    oh1 = jnp.where(lane == e1, 1.0, 0.0)
    row = lax.broadcasted_iota(jnp.int32, (tm, tm), 0)
    col = lax.broadcasted_iota(jnp.int32, (tm, tm), 1)
    tri = jnp.where(row > col, 1.0, 0.0).astype(BF16)
    cs0 = jnp.sum(oh0, axis=0, keepdims=True)
    cs1 = jnp.sum(oh1, axis=0, keepdims=True)
    carry = carry_ref[0:1, :]
    pre = jnp.dot(tri, jnp.concatenate([oh0, oh1], axis=1).astype(BF16),
                  preferred_element_type=F32)
    pre0 = pre[:, :LANES] + carry
    pre1 = pre[:, LANES:] + (carry + cs0)
    rank0 = jnp.sum(oh0 * pre0, axis=1, keepdims=True)
    rank1 = jnp.sum(oh1 * pre1, axis=1, keepdims=True)
    new_carry = carry + cs0 + cs1
    carry_ref[0:1, :] = new_carry
    cnt_ref[...] = jnp.broadcast_to(new_carry, cnt_ref.shape)

    route = jnp.where(lane == ROUTE_E0, e0.astype(F32), 0.0)
    route = jnp.where(lane == ROUTE_E1, e1.astype(F32), route)
    route = jnp.where(lane == ROUTE_G0, gate0, route)
    route = jnp.where(lane == ROUTE_G1, gate1, route)
    route = jnp.where(lane == ROUTE_R0, rank0, route)
    route = jnp.where(lane == ROUTE_R1, rank1, route)
    route_ref[...] = route
    route_t_ref[...] = route.T[0:route_t_ref.shape[0], :]


def _mix_route(o_f, o_s, g_f, g_s, x2, wpf, wps, wout, g_ffn, wr_cat, b_r, *, tm):
    T, D = x2.shape
    row_blk = lambda w: pl.BlockSpec((tm, w), lambda i: (i, 0))
    const = lambda shape: pl.BlockSpec(shape, lambda i: (0, 0))
    return pl.pallas_call(
        _mix_route_kernel,
        grid=(T // tm,),
        in_specs=[row_blk(FOX_W), row_blk(SWA_W), row_blk(D), row_blk(D), row_blk(D),
                  const((FOX_W, D)), const((SWA_W, D)), const((D, D)), const((1, D)),
                  const((D, 2 * LANES)), const((1, LANES))],
        out_specs=(row_blk(D), row_blk(D), row_blk(LANES),
                   pl.BlockSpec((ROUTE_ROWS, tm), lambda i: (0, i)), const((8, LANES))),
        out_shape=(jax.ShapeDtypeStruct((T, D), F32), jax.ShapeDtypeStruct((T, D), F32),
                   jax.ShapeDtypeStruct((T, LANES), F32), jax.ShapeDtypeStruct((ROUTE_ROWS, T), F32),
                   jax.ShapeDtypeStruct((8, LANES), F32)),
        scratch_shapes=[pltpu.VMEM((8, LANES), F32)],
        compiler_params=_cparams(("arbitrary",)),
        name="mix_route",
    )(o_f, o_s, g_f, g_s, x2, wpf, wps, wout, g_ffn, wr_cat, b_r)


def _plan_kernel(cnt_ref, route_t_ref, dest_ref, tbl_ref, blk_ref):
    lane = lax.broadcasted_iota(jnp.int32, cnt_ref.shape, 1)
    cnt = jnp.where(lane < N_EXPERTS, cnt_ref[...], 0.0)
    padded = jnp.floor((cnt + (ROW_BLOCK - 1)) * (1.0 / ROW_BLOCK)) * ROW_BLOCK
    pad_end = padded
    shift = 1
    while shift < N_EXPERTS:
        pad_end = pad_end + jnp.where(lane >= shift, pltpu.roll(pad_end, shift, axis=1), 0.0)
        shift *= 2
    pad_start = pad_end - padded
    row = lax.broadcasted_iota(jnp.int32, cnt_ref.shape, 0)
    tbl_ref[...] = jnp.where(row == 0, pad_start, pad_end).astype(jnp.int32)

    fields = route_t_ref[...]
    start_of = jnp.zeros_like(fields)
    for e in range(N_EXPERTS):
        start_of = jnp.where(fields == float(e), pad_start[0:1, e:e + 1], start_of)
    ranks = pltpu.roll(fields, ROUTE_ROWS - ROUTE_R0, axis=0)
    dest_ref[...] = (start_of + ranks).astype(jnp.int32)

    ends = pad_end.T[0:N_EXPERTS, 0:1]
    n_col = blk_ref.shape[1]
    blk_pos = (lax.broadcasted_iota(jnp.int32, (N_EXPERTS, n_col), 1) * ROW_BLOCK).astype(F32)
    owner = jnp.minimum(jnp.sum(jnp.where(ends <= blk_pos, 1.0, 0.0), axis=0, keepdims=True),
                        N_EXPERTS - 1.0)
    token_end = (pad_start + cnt).T[0:N_EXPERTS, 0:1]
    expert = lax.broadcasted_iota(jnp.int32, (N_EXPERTS, n_col), 0).astype(F32)
    owner_end = jnp.sum(jnp.where(expert == owner, token_end, 0.0), axis=0, keepdims=True)
    valid = jnp.clip(owner_end - blk_pos[0:1, :], 0.0, float(ROW_BLOCK))
    blk_row = lax.broadcasted_iota(jnp.int32, blk_ref.shape, 0)
    blk_ref[...] = jnp.where(blk_row == 0, owner, valid).astype(jnp.int32)


def _plan(counts, route_t, n_blk):
    T = route_t.shape[1]
    n_col = (n_blk + LANES - 1) // LANES * LANES
    full = lambda shape: pl.BlockSpec(shape, lambda i: (0, 0))
    return pl.pallas_call(
        _plan_kernel,
        grid=(1,),
        in_specs=[full((8, LANES)), full((ROUTE_ROWS, T))],
        out_specs=(full((ROUTE_ROWS, T)), full((8, LANES)), full((8, n_col))),
        out_shape=(jax.ShapeDtypeStruct((ROUTE_ROWS, T), jnp.int32),
                   jax.ShapeDtypeStruct((8, LANES), jnp.int32),
                   jax.ShapeDtypeStruct((8, n_col), jnp.int32)),
        compiler_params=_cparams(("arbitrary",)),
        name="plan",
    )(counts, route_t)


def _dispatch_kernel(bounds_ref, dest0_ref, dest1_ref, h_ref, xs_ref, zbuf, sem, zsem):
    step = pl.program_id(0)
    tb = h_ref.shape[0]

    @pl.when(step == 0)
    def _():
        zbuf[...] = jnp.zeros_like(zbuf)

        def tail_copy(e):
            tail = pl.multiple_of(bounds_ref[e + 1] - ROW_BLOCK, ROW_BLOCK)
            return pltpu.make_async_copy(zbuf, xs_ref.at[pl.ds(tail, ROW_BLOCK), :], zsem)

        def for_nonempty(action):
            def body(e, carry):
                @pl.when(bounds_ref[e + 1] > bounds_ref[e])
                def _():
                    action(tail_copy(e))
                return carry
            lax.fori_loop(0, N_EXPERTS, body, 0)

        n_rows = xs_ref.shape[0]
        first_unused = bounds_ref[N_EXPERTS] // ROW_BLOCK

        def unused_copy(blk):
            return pltpu.make_async_copy(
                zbuf, xs_ref.at[pl.ds(pl.multiple_of(blk * ROW_BLOCK, ROW_BLOCK), ROW_BLOCK), :], zsem)

        def for_unused(action):
            def body(blk, carry):
                action(unused_copy(blk))
                return carry
            lax.fori_loop(first_unused, n_rows // ROW_BLOCK, body, 0)

        for_nonempty(lambda cp: cp.start())
        for_unused(lambda cp: cp.start())
        for_nonempty(lambda cp: cp.wait())
        for_unused(lambda cp: cp.wait())

    def start(j, carry):
        base = pl.multiple_of(j * SUBLANES, SUBLANES)
        tile = h_ref.at[pl.ds(base, SUBLANES), :]
        for u in range(SUBLANES):
            for k, dest_ref in enumerate((dest0_ref, dest1_ref)):
                d = dest_ref[base + u]
                pltpu.make_async_copy(tile.at[pl.ds(u, 1), :], xs_ref.at[pl.ds(d, 1), :],
                                      sem).start(priority=k)
        return carry

    lax.fori_loop(0, tb // SUBLANES, start, 0)
    for _ in range(2):
        pltpu.make_async_copy(h_ref, xs_ref.at[pl.ds(0, tb), :], sem).wait()


def _dispatch(bounds, dest0, dest1, h2, n_rows, *, tb):
    T, D = h2.shape
    smem_blk = pl.BlockSpec((tb,), lambda i, bd: (i,), memory_space=pltpu.SMEM)
    grid_spec = pltpu.PrefetchScalarGridSpec(
        num_scalar_prefetch=1,
        grid=(T // tb,),
        in_specs=[smem_blk, smem_blk, pl.BlockSpec((tb, D), lambda i, bd: (i, 0))],
        out_specs=pl.BlockSpec(memory_space=pl.ANY),
        scratch_shapes=[pltpu.VMEM((ROW_BLOCK, D), F32), pltpu.SemaphoreType.DMA(()),
                        pltpu.SemaphoreType.DMA(())],
    )
    return pl.pallas_call(
        _dispatch_kernel,
        grid_spec=grid_spec,
        out_shape=jax.ShapeDtypeStruct((n_rows, D), F32),
        compiler_params=_cparams(("arbitrary",)),
        name="dispatch",
    )(bounds, dest0, dest1, h2)


def _experts_kernel(be_ref, nu_ref, rows_ref, xs_ref, w1_ref, w3_ref, w2_ref, ys_ref,
                    w1f, w3f, w2f, w1b, w3b, w2b, region_ref, wsem):
    i = pl.program_id(0)
    n_used = nu_ref[0]
    e = be_ref[i]
    used = i < n_used
    first = (i == 0) | (e != be_ref[jnp.maximum(i - 1, 0)])

    def weight_copies(expert, slot):
        return (pltpu.make_async_copy(w1_ref.at[expert], w1f.at[slot], wsem.at[slot]),
                pltpu.make_async_copy(w3_ref.at[expert], w3f.at[slot], wsem.at[slot]),
                pltpu.make_async_copy(w2_ref.at[expert], w2f.at[slot], wsem.at[slot]))

    @pl.when(i == 0)
    def _():
        region_ref[0] = 0
        for cp in weight_copies(e, 0):
            cp.start()

    @pl.when(used & first)
    def _():
        slot = region_ref[0] % 2
        region_ref[0] = region_ref[0] + 1
        for cp in weight_copies(e, slot):
            cp.wait()
        w1b[...] = w1f[slot].astype(BF16)
        w3b[...] = w3f[slot].astype(BF16)
        w2b[...] = w2f[slot].astype(BF16)
        nxt = lax.while_loop(lambda j: (j < n_used) & (be_ref[jnp.minimum(j, n_used - 1)] == e),
                             lambda j: j + 1, i + 1)

        @pl.when(nxt < n_used)
        def _():
            for cp in weight_copies(be_ref[jnp.minimum(nxt, n_used - 1)], 1 - slot):
                cp.start()

    def swiglu(n_rows):
        xb = xs_ref[0:n_rows, :].astype(BF16)
        de = w1b.shape[1]
        n_part = 2
        part = de // n_part
        acts = []
        for q in range(n_part):
            cols = slice(q * part, (q + 1) * part)
            h1 = jnp.dot(xb, w1b[:, cols], preferred_element_type=F32)
            h3 = jnp.dot(xb, w3b[:, cols], preferred_element_type=F32)
            acts.append(((h1 * jax.nn.sigmoid(h1)) * h3).astype(BF16))
        y = jnp.dot(acts[0], w2b[0:part, :], preferred_element_type=F32)
        for q in range(1, n_part):
            y = y + jnp.dot(acts[q], w2b[q * part:(q + 1) * part, :], preferred_element_type=F32)
        ys_ref[0:n_rows, :] = y

    n_rows = ys_ref.shape[0]
    half_only = rows_ref[i] <= n_rows // 2

    @pl.when(used & jnp.logical_not(half_only))
    def _():
        swiglu(n_rows)

    @pl.when(used & half_only)
    def _():
        swiglu(n_rows // 2)
        ys_ref[n_rows // 2:, :] = jnp.zeros((n_rows - n_rows // 2, ys_ref.shape[1]), F32)

    @pl.when(jnp.logical_not(used))
    def _():
        ys_ref[...] = jnp.zeros_like(ys_ref)


def _experts(blk_expert, n_used, blk_rows, xs, w1, w3, w2):
    P, D = xs.shape
    de = w1.shape[-1]
    n_blk = P // ROW_BLOCK
    grid_spec = pltpu.PrefetchScalarGridSpec(
        num_scalar_prefetch=3,
        grid=(n_blk,),
        in_specs=[pl.BlockSpec((ROW_BLOCK, D),
                               lambda i, be, nu, br: (jnp.minimum(i, jnp.maximum(nu[0] - 1, 0)), 0)),
                  pl.BlockSpec(memory_space=pl.ANY), pl.BlockSpec(memory_space=pl.ANY),
                  pl.BlockSpec(memory_space=pl.ANY)],
        out_specs=pl.BlockSpec((ROW_BLOCK, D), lambda i, be, nu, br: (i, 0)),
        scratch_shapes=[pltpu.VMEM((2, D, de), F32), pltpu.VMEM((2, D, de), F32),
                        pltpu.VMEM((2, de, D), F32),
                        pltpu.VMEM((D, de), BF16), pltpu.VMEM((D, de), BF16),
                        pltpu.VMEM((de, D), BF16),
                        pltpu.SMEM((1,), jnp.int32), pltpu.SemaphoreType.DMA((2,))],
    )
    return pl.pallas_call(
        _experts_kernel,
        grid_spec=grid_spec,
        out_shape=jax.ShapeDtypeStruct((P, D), F32),
        compiler_params=_cparams(("arbitrary",)),
        name="experts",
    )(blk_expert, n_used, blk_rows, xs, w1, w3, w2)


def _combine_kernel(d0_cur, d1_cur, d0_next, d1_next, x1_ref, route_ref, gfin_ref, ys_ref, o_ref,
                    ybuf, sem):
    i = pl.program_id(0)
    tb = x1_ref.shape[0]
    slot = i % 2

    def issue(dest_refs, buf_slot):
        def start(j, carry):
            base = pl.multiple_of(j * SUBLANES, SUBLANES)
            for k, dest_ref in enumerate(dest_refs):
                tile = ybuf.at[buf_slot, k, pl.ds(base, SUBLANES), :]
                for u in range(SUBLANES):
                    d = dest_ref[base + u]
                    pltpu.make_async_copy(ys_ref.at[pl.ds(d, 1), :], tile.at[pl.ds(u, 1), :],
                                          sem.at[buf_slot]).start(priority=k)
            return carry
        lax.fori_loop(0, tb // SUBLANES, start, 0)

    @pl.when(i == 0)
    def _():
        issue((d0_cur, d1_cur), 0)

    @pl.when(i + 1 < pl.num_programs(0))
    def _():
        issue((d0_next, d1_next), 1 - slot)

    for k in range(2):
        pltpu.make_async_copy(ys_ref.at[pl.ds(0, tb), :], ybuf.at[slot, k], sem.at[slot]).wait()

    route = route_ref[...]
    lane = lax.broadcasted_iota(jnp.int32, route.shape, 1)
    g0 = jnp.sum(jnp.where(lane == ROUTE_G0, route, 0.0), axis=1, keepdims=True)
    g1 = jnp.sum(jnp.where(lane == ROUTE_G1, route, 0.0), axis=1, keepdims=True)
    moe = g0 * ybuf[slot, 0] + g1 * ybuf[slot, 1]
    x2 = x1_ref[...] + moe
    ms = jnp.mean(x2 * x2, axis=-1, keepdims=True)
    o_ref[...] = (x2 * lax.rsqrt(ms + EPS)) * gfin_ref[...]


def _combine(dest0, dest1, x1, route, g_final, ys, *, tb):
    T, D = x1.shape
    nb = T // tb
    cur = pl.BlockSpec((tb,), lambda i: (i,), memory_space=pltpu.SMEM)
    nxt = pl.BlockSpec((tb,), lambda i: (jnp.minimum(i + 1, nb - 1),), memory_space=pltpu.SMEM)
    return pl.pallas_call(
        _combine_kernel,
        grid=(nb,),
        in_specs=[cur, cur, nxt, nxt,
                  pl.BlockSpec((tb, D), lambda i: (i, 0)),
                  pl.BlockSpec((tb, LANES), lambda i: (i, 0)),
                  pl.BlockSpec((1, D), lambda i: (0, 0)),
                  pl.BlockSpec(memory_space=pl.ANY)],
        out_specs=pl.BlockSpec((tb, D), lambda i: (i, 0)),
        out_shape=jax.ShapeDtypeStruct((T, D), F32),
        scratch_shapes=[pltpu.VMEM((2, 2, tb, D), F32), pltpu.SemaphoreType.DMA((2,))],
        compiler_params=_cparams(("arbitrary",)),
        name="combine",
    )(dest0, dest1, dest0, dest1, x1, route, g_final, ys)


def _pick(n, prefs):
    for p in prefs:
        if n % p == 0:
            return p
    return n


def _layer(x2, pos2, batch, seq, attn_norm, w_in, b_forget, b_gate, sinks, w_proj_fox, w_proj_swa,
           w_out, ffn_norm, w_group, b_group, w_expert, b_expert, w1, w3, w2, final_norm):
    T, D = x2.shape
    scale = HEAD_DIM ** -0.5

    sp = np.cumsum([FOX_W, FOX_W, FOX_W, FOX_HEADS, SWA_W, SWA_KV_W, SWA_KV_W, D, D])[:-1]
    wq_f, wk_f, wv_f, w_fl, wq_s, wk_s, wv_s, wg_f, wg_s = jnp.split(w_in, sp, axis=1)
    head_order = np.asarray(SWA_HEAD_ORDER)
    wq_s = wq_s.reshape(D, SWA_HEADS, HEAD_DIM)[:, head_order, :].reshape(D, SWA_W)
    pieces = (wq_f * (scale * LOG2E), wk_f, wv_f, wq_s * (scale * LOG2E),
              jnp.concatenate([wk_s, wv_s], axis=1),
              wg_f, wg_s, jnp.pad(w_fl, ((0, 0), (0, LANES - FOX_HEADS))))
    weights = {name: piece.astype(BF16) for name, piece in zip(W_NAMES, pieces)}
    half = HEAD_DIM // 2
    inv_freq = ROPE_THETA ** (-jnp.arange(half, dtype=F32) * 2.0 / HEAD_DIM)
    invf = jnp.tile(inv_freq, LANES // half).reshape(1, LANES)
    bf_pad = jnp.pad(b_forget.astype(F32), (0, LANES - FOX_HEADS)).reshape(1, LANES)
    sinks_perm = sinks.astype(F32)[head_order] * LOG2E
    wps = w_proj_swa.reshape(SWA_HEADS, HEAD_DIM, D)[head_order].reshape(SWA_W, D).astype(BF16)
    wpf = w_proj_fox.astype(BF16)
    wout = w_out.astype(BF16)
    w_r = jnp.pad(jnp.concatenate([w_group, w_expert], axis=1).astype(F32),
                  ((0, 0), (0, LANES - N_GROUPS - N_EXPERTS)))
    wr_hi = w_r.astype(BF16)
    wr_lo = (w_r - wr_hi.astype(F32)).astype(BF16)
    wr_cat = jnp.concatenate([wr_hi, wr_lo], axis=1)
    b_r = jnp.pad(jnp.concatenate([b_group, b_expert]).astype(F32),
                  (0, LANES - N_GROUPS - N_EXPERTS)).reshape(1, LANES)

    tm1 = _pick(seq, (512, 256, 128))
    qt, kf, vt, qs, ks, vs, g_f, g_s, qxt, kx = _in_proj(
        x2, pos2, attn_norm.reshape(1, D), weights, invf, bf_pad, b_gate.astype(F32),
        batch=batch, seq=seq, tm=tm1)
    tq = _pick(seq, (FOX_TQ, 128))
    tk = min(tq, FOX_TK)
    o_f = _fox_attn(qt, qxt, kf, kx, vt, batch=batch, seq=seq, tq=tq, tk=tk,
                    unroll=min(FOX_UNROLL, tq // tk))
    o_s = _swa_attn(sinks_perm, qs, ks, vs, batch=batch, seq=seq, tq=_pick(seq, (512, 256, 128)))

    tm3 = _pick(T, (512, 256, 128))
    x1, h2, route, route_t, counts = _mix_route(o_f, o_s, g_f, g_s, x2, wpf, wps, wout,
                                       ffn_norm.reshape(1, D), wr_cat, b_r, tm=tm3)

    P = 2 * T + N_EXPERTS * ROW_BLOCK
    n_blk = P // ROW_BLOCK
    dest, tbl, blk = _plan(counts, route_t, n_blk)
    dest0, dest1 = dest[ROUTE_E0], dest[ROUTE_E1]
    pad_end = tbl[1, :N_EXPERTS]
    bounds = jnp.concatenate([jnp.zeros((1,), jnp.int32), pad_end])
    blk_expert, blk_rows = blk[0, :n_blk], blk[1, :n_blk]
    n_used = pad_end[-1:] // ROW_BLOCK

    xs = _dispatch(bounds, dest0, dest1, h2, P, tb=_pick(T, (2048, 1024, 512, 256, 128)))
    ys = _experts(blk_expert, n_used, blk_rows, xs, w1, w3, w2)
    return _combine(dest0, dest1, x1, route, final_norm.reshape(1, D), ys,
                    tb=_pick(T, (512, 256, 128)))


def kernel(x, positions, attn_norm, w_in, b_forget, b_gate, attn_sinks, w_proj_fox, w_proj_swa,
           w_out, ffn_norm, w_group, b_group, w_expert, b_expert, w1, w3, w2, final_norm):
    B, S, D = x.shape
    depth = attn_norm.shape[0]
    x2 = x.reshape(B * S, D)
    pos2 = positions.reshape(B * S, 1).astype(jnp.int32)
    assert depth == 1, "a single layer is followed directly by the final norm"
    out = _layer(x2, pos2, B, S, attn_norm[0], w_in[0], b_forget[0], b_gate[0], attn_sinks[0],
                 w_proj_fox[0], w_proj_swa[0], w_out[0], ffn_norm[0], w_group[0], b_group[0],
                 w_expert[0], b_expert[0], w1[0], w3[0], w2[0], final_norm)
    return out.reshape(B, S, D)
```

```python
import functools

import numpy as np
import jax
import jax.numpy as jnp
from jax import lax
from jax.experimental import pallas as pl
from jax.experimental.pallas import tpu as pltpu

HEAD_DIM = 64
FOX_HEADS = 8
SWA_HEADS = 8
SWA_KV_HEADS = 2
WINDOW = 128
ROPE_THETA = 10000.0
N_GROUPS = 8
EXPERTS_PER_GROUP = 8
N_EXPERTS = N_GROUPS * EXPERTS_PER_GROUP
ROW_BLOCK = 512
SUBLANES = 8
EPS = 1e-6
LANES = 128
NEG_BIG = -1e30
LOG2E = 1.4426950408889634
BIAS_TERMS = 3
BIAS_LANES = 2 * BIAS_TERMS
FOX_TQ = 1024
FOX_TK = 256
FOX_UNROLL = 4
ONES_ROWS = 16

FOX_W = FOX_HEADS * HEAD_DIM
SWA_W = SWA_HEADS * HEAD_DIM
SWA_KV_W = SWA_KV_HEADS * HEAD_DIM
SWA_HEAD_ORDER = (0, 4, 1, 5, 2, 6, 3, 7)

F32 = jnp.float32
BF16 = jnp.bfloat16
VMEM_LIMIT = 56 * 1024 * 1024


def _cparams(sem):
    return pltpu.CompilerParams(dimension_semantics=sem, vmem_limit_bytes=VMEM_LIMIT)


def _split3(x):
    hi = x.astype(BF16)
    r1 = x - hi.astype(F32)
    mid = r1.astype(BF16)
    lo = (r1 - mid.astype(F32)).astype(BF16)
    return hi, mid, lo


W_NAMES = ("q_fox", "k_fox", "v_fox", "q_swa", "kv_swa", "gate_fox", "gate_swa", "forget")


def _in_proj_kernel(x_ref, pos_ref, g_ref, invf_ref, bf_ref, bg_ref, place_ref, ones_ref, *refs):
    w = dict(zip(W_NAMES, refs[:len(W_NAMES)]))
    (qf_ref, kf_ref, vf_ref, qs_ref, ks_ref, vs_ref, gf_ref, gs_ref, qx_ref, kx_ref,
     carry_ref) = refs[len(W_NAMES):]
    j = pl.program_id(1)
    tm = x_ref.shape[0]

    @pl.when(j == 0)
    def _():
        carry_ref[...] = jnp.zeros_like(carry_ref)

    x = x_ref[...]
    ms = jnp.mean(x * x, axis=-1, keepdims=True)
    h = ((x * lax.rsqrt(ms + EPS)) * g_ref[...]).astype(BF16)

    def mm(name):
        return jnp.dot(h, w[name][...], preferred_element_type=F32)

    def store_pairs_t(ref, y):
        for pr in range(ref.shape[0]):
            ref[pr] = y[:, pr * LANES:(pr + 1) * LANES].T.astype(BF16)

    store_pairs_t(qf_ref, mm("q_fox"))
    kf_ref[...] = mm("k_fox").astype(BF16)
    store_pairs_t(vf_ref, mm("v_fox"))

    ang = pos_ref[...].astype(F32) * invf_ref[...]
    cos = jnp.cos(ang)
    sin = jnp.sin(ang)
    n_rep = SWA_W // LANES
    cos_q = jnp.concatenate([cos] * n_rep, axis=1)
    sin_q = jnp.concatenate([sin] * n_rep, axis=1)
    def rot_half(y):
        width = y.shape[1]
        lane_y = lax.broadcasted_iota(jnp.int32, y.shape, 1)
        first = (lane_y & (HEAD_DIM - 1)) < HEAD_DIM // 2
        return jnp.where(first, -pltpu.roll(y, width - HEAD_DIM // 2, axis=1),
                         pltpu.roll(y, HEAD_DIM // 2, axis=1))

    q_s = mm("q_swa")
    kv_s = mm("kv_swa")
    k_s = kv_s[:, :SWA_KV_W]
    qs_ref[...] = (q_s * cos_q + rot_half(q_s) * sin_q).astype(BF16)
    ks_ref[...] = (k_s * cos + rot_half(k_s) * sin).astype(BF16)
    vs_ref[...] = kv_s[:, SWA_KV_W:].astype(BF16)

    gf_ref[...] = jax.nn.sigmoid(mm("gate_fox") + bg_ref[0:1, :]).astype(BF16)
    gs_ref[...] = jax.nn.sigmoid(mm("gate_swa") + bg_ref[1:2, :]).astype(BF16)

    z = mm("forget") + bf_ref[...]
    lane = lax.broadcasted_iota(jnp.int32, (tm, LANES), 1)
    lf = jnp.minimum(z, 0.0) - jnp.log1p(jnp.exp(-jnp.abs(z)))
    lf = jnp.where(lane < FOX_HEADS, lf, 0.0)
    row = lax.broadcasted_iota(jnp.int32, (tm, tm), 0)
    col = lax.broadcasted_iota(jnp.int32, (tm, tm), 1)
    tri = jnp.where(row >= col, 1.0, 0.0).astype(BF16)
    hi, mid, lo = _split3(lf)
    sums = jnp.dot(tri, jnp.concatenate([hi, mid, lo], axis=1), preferred_element_type=F32)
    c = (sums[:, :LANES] + sums[:, LANES:2 * LANES] + sums[:, 2 * LANES:]) + carry_ref[0:1, :]
    carry_ref[0:1, :] = c[tm - 1:tm, :]
    chi, cmid, clo = _split3(c * LOG2E)
    ext = jnp.dot(jnp.concatenate([chi, cmid, clo], axis=1), place_ref[...],
                  preferred_element_type=F32) + ones_ref[...]
    qx_ref[...] = ext[:, :LANES].T.astype(BF16)
    kx_ref[...] = ext[:, LANES:].astype(BF16)


def _bias_placement():
    place = np.zeros((BIAS_TERMS * LANES, 2 * LANES), np.float32)
    ones = np.zeros((1, 2 * LANES), np.float32)
    for h in range(FOX_HEADS):
        for t in range(BIAS_TERMS):
            place[t * LANES + h, BIAS_LANES * h + t] = 1.0
            place[t * LANES + h, LANES + BIAS_LANES * h + BIAS_TERMS + t] = -1.0
            ones[0, BIAS_LANES * h + BIAS_TERMS + t] = 1.0
            ones[0, LANES + BIAS_LANES * h + t] = 1.0
    return jnp.asarray(place, BF16), jnp.asarray(ones, F32)


def _in_proj(x2, pos2, g_attn, weights, invf, bf_pad, b_gate, *, batch, seq, tm):
    T, D = x2.shape
    nj = seq // tm
    w_list = [weights[name] for name in W_NAMES]
    place, ones = _bias_placement()
    row_blk = lambda w: pl.BlockSpec((tm, w), lambda b, j: (b * nj + j, 0))
    const = lambda shape: pl.BlockSpec(shape, lambda b, j: (0, 0))
    n_pair = FOX_HEADS // 2
    pair_t = pl.BlockSpec((None, n_pair, LANES, tm), lambda b, j: (b, 0, 0, j))
    out_shapes = (
        jax.ShapeDtypeStruct((batch, n_pair, LANES, seq), BF16), jax.ShapeDtypeStruct((T, FOX_W), BF16),
        jax.ShapeDtypeStruct((batch, n_pair, LANES, seq), BF16), jax.ShapeDtypeStruct((T, SWA_W), BF16),
        jax.ShapeDtypeStruct((T, SWA_KV_W), BF16), jax.ShapeDtypeStruct((T, SWA_KV_W), BF16),
        jax.ShapeDtypeStruct((T, D), BF16), jax.ShapeDtypeStruct((T, D), BF16),
        jax.ShapeDtypeStruct((batch, LANES, seq), BF16), jax.ShapeDtypeStruct((T, LANES), BF16),
    )
    return pl.pallas_call(
        _in_proj_kernel,
        grid=(batch, nj),
        in_specs=[row_blk(D), row_blk(1), const((1, D)), const((1, LANES)),
                  const((1, LANES)), const((2, D)), const(place.shape), const(ones.shape)]
                 + [const(wt.shape) for wt in w_list],
        out_specs=(pair_t, row_blk(FOX_W), pair_t, row_blk(SWA_W),
                   row_blk(SWA_KV_W), row_blk(SWA_KV_W), row_blk(D), row_blk(D),
                   pl.BlockSpec((None, LANES, tm), lambda b, j: (b, 0, j)), row_blk(LANES)),
        out_shape=out_shapes,
        scratch_shapes=[pltpu.VMEM((8, LANES), F32)],
        compiler_params=_cparams(("arbitrary", "arbitrary")),
        name="in_proj",
    )(x2, pos2, g_attn, invf, bf_pad, b_gate, place, ones, *w_list)


def _fox_kernel(q_ref, qx_ref, k_ref, kx_ref, vt_ref, o_ref, acc_ref, *, tk, unroll):
    p = pl.program_id(1)
    i = pl.program_id(2)
    tq = q_ref.shape[1]
    n_diag = tq // tk
    qt = q_ref[...]
    qxt = qx_ref[...]
    row = lax.broadcasted_iota(jnp.int32, (LANES, tq), 0)
    zero = jnp.zeros_like(qt)
    q_ext = []
    for hh in range(2):
        first = BIAS_LANES * (2 * p + hh)
        head_rows = (row < HEAD_DIM) if hh == 0 else (row >= HEAD_DIM)
        bias_rows = (row >= first) & (row < first + BIAS_LANES)
        q_ext.append(jnp.concatenate([jnp.where(head_rows, qt, zero),
                                      jnp.where(bias_rows, qxt, zero)], axis=0))

    acc_ref[...] = jnp.zeros_like(acc_ref)

    def scores(ks, c0, masked):
        k_ext = jnp.concatenate([k_ref[pl.ds(ks, tk), :], kx_ref[pl.ds(ks, tk), :]], axis=1)
        out = []
        for hh in range(2):
            s = jnp.dot(k_ext, q_ext[hh][:, c0:], preferred_element_type=F32)
            if masked:
                key = lax.broadcasted_iota(jnp.int32, s.shape, 0)
                qry = lax.broadcasted_iota(jnp.int32, s.shape, 1)
                s = jnp.where(key <= qry, s, -jnp.inf)
            out.append(s)
        return out

    ones_rows = jnp.ones((ONES_ROWS, tk), BF16)

    def accumulate(ks, c0, s_pair, stats):
        vt = jnp.concatenate([vt_ref[:, pl.ds(ks, tk)], ones_rows], axis=0)
        out = []
        for hh in range(2):
            m_all = stats[hh]
            m_old = m_all[:, c0:]
            s = s_pair[hh]
            m_new = jnp.maximum(m_old, jnp.max(s, axis=0, keepdims=True))
            alpha = jnp.exp2(m_old - m_new)
            pm = jnp.exp2(s - m_new)
            acc_ref[hh, :, c0:] = alpha * acc_ref[hh, :, c0:] + jnp.dot(
                vt, pm.astype(BF16), preferred_element_type=F32)
            if c0:
                m_new = jnp.concatenate([m_all[:, :c0], m_new], axis=1)
            out.append(m_new)
        return tuple(out)

    def run_tiles(first_tile, count, stats, diagonal):
        start = lambda u: pl.multiple_of((first_tile + u) * tk, tk)
        col0 = lambda u: u * tk if diagonal else 0
        s_next = scores(start(0), col0(0), diagonal)
        for u in range(count):
            s_cur = s_next
            if u + 1 < count:
                s_next = scores(start(u + 1), col0(u + 1), diagonal)
            stats = accumulate(start(u), col0(u), s_cur, stats)
        return stats

    init = tuple(jnp.full((1, tq), NEG_BIG, F32) for _ in range(2))
    stats = lax.fori_loop(0, i * (n_diag // unroll),
                          lambda jb, st: run_tiles(jb * unroll, unroll, st, False), init)
    stats = run_tiles(i * n_diag, n_diag, stats, True)

    del stats
    o_t = jnp.where(row < HEAD_DIM, acc_ref[0, 0:LANES, :] * (1.0 / acc_ref[0, LANES:LANES + 1, :]),
                    acc_ref[1, 0:LANES, :] * (1.0 / acc_ref[1, LANES:LANES + 1, :]))
    o_ref[...] = o_t.T.astype(BF16)


def _fox_attn(qt, qxt, kf, kx, vt, *, batch, seq, tq, tk, unroll):
    T = kf.shape[0]
    nq = seq // tq
    n_pair = FOX_HEADS // 2
    return pl.pallas_call(
        functools.partial(_fox_kernel, tk=tk, unroll=unroll),
        grid=(batch, n_pair, nq),
        in_specs=[
            pl.BlockSpec((None, None, LANES, tq), lambda b, p, i: (b, p, 0, i)),
            pl.BlockSpec((None, LANES, tq), lambda b, p, i: (b, 0, i)),
            pl.BlockSpec((seq, LANES), lambda b, p, i: (b, p)),
            pl.BlockSpec((seq, LANES), lambda b, p, i: (b, 0)),
            pl.BlockSpec((None, None, LANES, seq), lambda b, p, i: (b, p, 0, 0)),
        ],
        out_specs=pl.BlockSpec((tq, LANES), lambda b, p, i: (b * nq + i, p)),
        out_shape=jax.ShapeDtypeStruct((T, FOX_W), BF16),
        scratch_shapes=[pltpu.VMEM((2, LANES + ONES_ROWS, tq), F32)],
        compiler_params=_cparams(("arbitrary", "arbitrary", "arbitrary")),
        name="fox_attn",
    )(qt, qxt, kf, kx, vt)


def _swa_kernel(sink_ref, q_ref, kc_ref, kp_ref, vc_ref, vp_ref, o_ref):
    i = pl.program_id(1)
    tq = q_ref.shape[0]
    n_win = tq // WINDOW
    n_col = SWA_W // LANES
    lane = lax.broadcasted_iota(jnp.int32, (WINDOW, LANES), 1)
    low = lane < HEAD_DIM
    r = lax.broadcasted_iota(jnp.int32, (WINDOW, 2 * WINDOW), 0)
    c = lax.broadcasted_iota(jnp.int32, (WINDOW, 2 * WINDOW), 1)
    dist = WINDOW + r - c
    band = (dist >= 0) & (dist < WINDOW)
    for w in range(n_win):
        if w == 0:
            k_win = jnp.concatenate([kp_ref[...], kc_ref[0:WINDOW, :]], axis=0)
            v_win = jnp.concatenate([vp_ref[...], vc_ref[0:WINDOW, :]], axis=0)
            first = i == 0
            valid = band & ((c >= WINDOW) | jnp.logical_not(first))
        else:
            k_win = kc_ref[(w - 1) * WINDOW:(w + 1) * WINDOW, :]
            v_win = vc_ref[(w - 1) * WINDOW:(w + 1) * WINDOW, :]
            valid = band
        for col_i in range(n_col):
            qc = q_ref[w * WINDOW:(w + 1) * WINDOW, col_i * LANES:(col_i + 1) * LANES]
            outs = []
            for half in range(2):
                qm = jnp.where(low if half == 0 else jnp.logical_not(low), qc, jnp.zeros_like(qc))
                s = lax.dot_general(qm, k_win, (((1,), (1,)), ((), ())),
                                    preferred_element_type=F32)
                s = jnp.where(valid, s, -jnp.inf)
                sink = sink_ref[2 * col_i + half]
                m = jnp.maximum(jnp.max(s, axis=1, keepdims=True), sink)
                pm = jnp.exp2(s - m)
                denom = jnp.sum(pm, axis=1, keepdims=True) + jnp.exp2(sink - m)
                outs.append(jnp.dot(pm.astype(BF16), v_win, preferred_element_type=F32)
                            * (1.0 / denom))
            o_ref[w * WINDOW:(w + 1) * WINDOW, col_i * LANES:(col_i + 1) * LANES] = (
                jnp.where(low, outs[0], outs[1]).astype(BF16))


def _swa_attn(sinks_perm, qs, ks, vs, *, batch, seq, tq):
    T = qs.shape[0]
    nq = seq // tq
    n_win = tq // WINDOW
    nb = seq // WINDOW
    cur = lambda w: pl.BlockSpec((tq, w), lambda b, i: (b * nq + i, 0))
    prev = pl.BlockSpec((WINDOW, SWA_KV_W),
                        lambda b, i: (jnp.maximum(b * nb + i * n_win - 1, 0), 0))
    return pl.pallas_call(
        _swa_kernel,
        grid=(batch, nq),
        in_specs=[pl.BlockSpec(memory_space=pltpu.SMEM), cur(SWA_W), cur(SWA_KV_W), prev,
                  cur(SWA_KV_W), prev],
        out_specs=cur(SWA_W),
        out_shape=jax.ShapeDtypeStruct((T, SWA_W), BF16),
        compiler_params=_cparams(("arbitrary", "arbitrary")),
        name="swa_attn",
    )(sinks_perm, qs, ks, ks, vs, vs)


ROUTE_E0, ROUTE_E1, ROUTE_G0, ROUTE_G1, ROUTE_R0, ROUTE_R1 = range(6)
ROUTE_ROWS = 8


def _mix_route_kernel(of_ref, os_ref, gf_ref, gs_ref, x_ref, wpf_ref, wps_ref, wout_ref, gn_ref,
                      wr_ref, br_ref, x1_ref, h2_ref, route_ref, route_t_ref, cnt_ref,
                      carry_ref, logits_ref):
    step = pl.program_id(0)
    tm = x_ref.shape[0]

    @pl.when(step == 0)
    def _():
        carry_ref[...] = jnp.zeros_like(carry_ref)
        logits_ref[...] = jnp.zeros_like(logits_ref)

    logits = logits_ref[...]

    a = jnp.dot(of_ref[...], wpf_ref[...], preferred_element_type=F32)
    b = jnp.dot(os_ref[...], wps_ref[...], preferred_element_type=F32)
    merged = gf_ref[...].astype(F32) * a + gs_ref[...].astype(F32) * b
    y = jnp.dot(merged.astype(BF16), wout_ref[...], preferred_element_type=F32)
    x1 = x_ref[...] + y
    x1_ref[...] = x1
    ms = jnp.mean(x1 * x1, axis=-1, keepdims=True)
    h2 = (x1 * lax.rsqrt(ms + EPS)) * gn_ref[...]
    h2_ref[...] = h2

    h_hi = h2.astype(BF16)
    h_lo = (h2 - h_hi.astype(F32)).astype(BF16)
    wr = wr_ref[...]
    r_hi = jnp.dot(h_hi, wr, preferred_element_type=F32)
    logits_ref[...] = (r_hi[:, :LANES] + r_hi[:, LANES:]
                       + jnp.dot(h_lo, wr[:, :LANES], preferred_element_type=F32)) + br_ref[...]

    lane = lax.broadcasted_iota(jnp.int32, (tm, LANES), 1)
    is_g = lane < N_GROUPS
    gl = jnp.where(is_g, logits, -jnp.inf)
    gmax = jnp.max(gl, axis=1, keepdims=True)
    gexp = jnp.where(is_g, jnp.exp(gl - gmax), 0.0)
    gprob = gexp / jnp.sum(gexp, axis=1, keepdims=True)
    g_w = jnp.max(gprob, axis=1, keepdims=True)
    g_idx = jnp.min(jnp.where(is_g & (gprob == g_w), lane, LANES), axis=1, keepdims=True)

    e_lane = lane - N_GROUPS
    in_grp = (e_lane >= 0) & (e_lane < N_EXPERTS) & ((e_lane >> 3) == g_idx)
    el = jnp.where(in_grp, logits, -jnp.inf)
    v0 = jnp.max(el, axis=1, keepdims=True)
    i0 = jnp.min(jnp.where(in_grp & (el == v0), lane, LANES), axis=1, keepdims=True)
    el1 = jnp.where(lane == i0, -jnp.inf, el)
    v1 = jnp.max(el1, axis=1, keepdims=True)
    i1 = jnp.min(jnp.where(in_grp & (lane != i0) & (el1 == v1), lane, LANES), axis=1, keepdims=True)
    t = jnp.exp(v1 - v0)
    den = 1.0 + t
    gate0 = g_w * (1.0 / den)
    gate1 = g_w * (t / den)
    e0 = i0 - N_GROUPS
    e1 = i1 - N_GROUPS

    oh0 = jnp.where(lane == e0, 1.0, 0.0)
    oh1 = jnp.where(lane == e1, 1.0, 0.0)
    row = lax.broadcasted_iota(jnp.int32, (tm, tm), 0)
    col = lax.broadcasted_iota(jnp.int32, (tm, tm), 1)
    tri = jnp.where(row > col, 1.0, 0.0).astype(BF16)
    cs0 = jnp.sum(oh0, axis=0, keepdims=True)
    cs1 = jnp.sum(oh1, axis=0, keepdims=True)
    carry = carry_ref[0:1, :]
    pre = jnp.dot(tri, jnp.concatenate([oh0, oh1], axis=1).astype(BF16),
                  preferred_element_type=F32)
    pre0 = pre[:, :LANES] + carry
    pre1 = pre[:, LANES:] + (carry + cs0)
    rank0 = jnp.sum(oh0 * pre0, axis=1, keepdims=True)
    rank1 = jnp.sum(oh1 * pre1, axis=1, keepdims=True)
    new_carry = carry + cs0 + cs1
    carry_ref[0:1, :] = new_carry
    cnt_ref[...] = jnp.broadcast_to(new_carry, cnt_ref.shape)

    route = jnp.where(lane == ROUTE_E0, e0.astype(F32), 0.0)
    route = jnp.where(lane == ROUTE_E1, e1.astype(F32), route)
    route = jnp.where(lane == ROUTE_G0, gate0, route)
    route = jnp.where(lane == ROUTE_G1, gate1, route)
    route = jnp.where(lane == ROUTE_R0, rank0, route)
    route = jnp.where(lane == ROUTE_R1, rank1, route)
    route_ref[...] = route
    route_t_ref[...] = route.T[0:route_t_ref.shape[0], :]

    @pl.when(step == 0)
    def _():
        carry_ref[...] = jnp.zeros_like(carry_ref)


def _mix_route(o_f, o_s, g_f, g_s, x2, wpf, wps, wout, g_ffn, wr_cat, b_r, *, tm):
    T, D = x2.shape
    nb = T // tm
    cur = lambda i: jnp.minimum(i, nb - 1)
    prev = lambda i: jnp.maximum(i - 1, 0)
    row_blk = lambda w: pl.BlockSpec((tm, w), lambda i: (cur(i), 0))
    const = lambda shape: pl.BlockSpec(shape, lambda i: (0, 0))
    return pl.pallas_call(
        _mix_route_kernel,
        grid=(nb + 1,),
        in_specs=[row_blk(FOX_W), row_blk(SWA_W), row_blk(D), row_blk(D), row_blk(D),
                  const((FOX_W, D)), const((SWA_W, D)), const((D, D)), const((1, D)),
                  const((D, 2 * LANES)), const((1, LANES))],
        out_specs=(row_blk(D), row_blk(D), pl.BlockSpec((tm, LANES), lambda i: (prev(i), 0)),
                   pl.BlockSpec((ROUTE_ROWS, tm), lambda i: (0, prev(i))), const((8, LANES))),
        out_shape=(jax.ShapeDtypeStruct((T, D), F32), jax.ShapeDtypeStruct((T, D), F32),
                   jax.ShapeDtypeStruct((T, LANES), F32), jax.ShapeDtypeStruct((ROUTE_ROWS, T), F32),
                   jax.ShapeDtypeStruct((8, LANES), F32)),
        scratch_shapes=[pltpu.VMEM((8, LANES), F32), pltpu.VMEM((tm, LANES), F32)],
        compiler_params=_cparams(("arbitrary",)),
        name="mix_route",
    )(o_f, o_s, g_f, g_s, x2, wpf, wps, wout, g_ffn, wr_cat, b_r)


def _plan_kernel(cnt_ref, route_t_ref, dest_ref, tbl_ref, blk_ref):
    lane = lax.broadcasted_iota(jnp.int32, cnt_ref.shape, 1)
    cnt = jnp.where(lane < N_EXPERTS, cnt_ref[...], 0.0)
    padded = jnp.floor((cnt + (ROW_BLOCK - 1)) * (1.0 / ROW_BLOCK)) * ROW_BLOCK
    pad_end = padded
    shift = 1
    while shift < N_EXPERTS:
        pad_end = pad_end + jnp.where(lane >= shift, pltpu.roll(pad_end, shift, axis=1), 0.0)
        shift *= 2
    pad_start = pad_end - padded
    row = lax.broadcasted_iota(jnp.int32, cnt_ref.shape, 0)
    tbl_ref[...] = jnp.where(row == 0, pad_start, pad_end).astype(jnp.int32)

    fields = route_t_ref[...]
    start_of = jnp.zeros_like(fields)
    for e in range(N_EXPERTS):
        start_of = jnp.where(fields == float(e), pad_start[0:1, e:e + 1], start_of)
    ranks = pltpu.roll(fields, ROUTE_ROWS - ROUTE_R0, axis=0)
    dest_ref[...] = (start_of + ranks).astype(jnp.int32)

    ends = pad_end.T[0:N_EXPERTS, 0:1]
    n_col = blk_ref.shape[1]
    blk_pos = (lax.broadcasted_iota(jnp.int32, (N_EXPERTS, n_col), 1) * ROW_BLOCK).astype(F32)
    owner = jnp.minimum(jnp.sum(jnp.where(ends <= blk_pos, 1.0, 0.0), axis=0, keepdims=True),
                        N_EXPERTS - 1.0)
    token_end = (pad_start + cnt).T[0:N_EXPERTS, 0:1]
    expert = lax.broadcasted_iota(jnp.int32, (N_EXPERTS, n_col), 0).astype(F32)
    owner_end = jnp.sum(jnp.where(expert == owner, token_end, 0.0), axis=0, keepdims=True)
    valid = jnp.clip(owner_end - blk_pos[0:1, :], 0.0, float(ROW_BLOCK))
    blk_row = lax.broadcasted_iota(jnp.int32, blk_ref.shape, 0)
    blk_ref[...] = jnp.where(blk_row == 0, owner, valid).astype(jnp.int32)


def _plan(counts, route_t, n_blk):
    T = route_t.shape[1]
    n_col = (n_blk + LANES - 1) // LANES * LANES
    full = lambda shape: pl.BlockSpec(shape, lambda i: (0, 0))
    return pl.pallas_call(
        _plan_kernel,
        grid=(1,),
        in_specs=[full((8, LANES)), full((ROUTE_ROWS, T))],
        out_specs=(full((ROUTE_ROWS, T)), full((8, LANES)), full((8, n_col))),
        out_shape=(jax.ShapeDtypeStruct((ROUTE_ROWS, T), jnp.int32),
                   jax.ShapeDtypeStruct((8, LANES), jnp.int32),
                   jax.ShapeDtypeStruct((8, n_col), jnp.int32)),
        compiler_params=_cparams(("arbitrary",)),
        name="plan",
    )(counts, route_t)


def _dispatch_kernel(bounds_ref, dest0_ref, dest1_ref, h_ref, xs_ref, zbuf, sem, zsem):
    step = pl.program_id(0)
    tb = h_ref.shape[0]

    @pl.when(step == 0)
    def _():
        zbuf[...] = jnp.zeros_like(zbuf)

        def tail_copy(e):
            tail = pl.multiple_of(bounds_ref[e + 1] - ROW_BLOCK, ROW_BLOCK)
            return pltpu.make_async_copy(zbuf, xs_ref.at[pl.ds(tail, ROW_BLOCK), :], zsem)

        def for_nonempty(action):
            def body(e, carry):
                @pl.when(bounds_ref[e + 1] > bounds_ref[e])
                def _():
                    action(tail_copy(e))
                return carry
            lax.fori_loop(0, N_EXPERTS, body, 0)

        n_rows = xs_ref.shape[0]
        first_unused = bounds_ref[N_EXPERTS] // ROW_BLOCK

        def unused_copy(blk):
            return pltpu.make_async_copy(
                zbuf, xs_ref.at[pl.ds(pl.multiple_of(blk * ROW_BLOCK, ROW_BLOCK), ROW_BLOCK), :], zsem)

        def for_unused(action):
            def body(blk, carry):
                action(unused_copy(blk))
                return carry
            lax.fori_loop(first_unused, n_rows // ROW_BLOCK, body, 0)

        for_nonempty(lambda cp: cp.start())
        for_unused(lambda cp: cp.start())
        for_nonempty(lambda cp: cp.wait())
        for_unused(lambda cp: cp.wait())

    def start(j, carry):
        base = pl.multiple_of(j * SUBLANES, SUBLANES)
        tile = h_ref.at[pl.ds(base, SUBLANES), :]
        for u in range(SUBLANES):
            for k, dest_ref in enumerate((dest0_ref, dest1_ref)):
                d = dest_ref[base + u]
                pltpu.make_async_copy(tile.at[pl.ds(u, 1), :], xs_ref.at[pl.ds(d, 1), :],
                                      sem).start(priority=k)
        return carry

    lax.fori_loop(0, tb // SUBLANES, start, 0)
    for _ in range(2):
        pltpu.make_async_copy(h_ref, xs_ref.at[pl.ds(0, tb), :], sem).wait()


def _dispatch(bounds, dest0, dest1, h2, n_rows, *, tb):
    T, D = h2.shape
    smem_blk = pl.BlockSpec((tb,), lambda i, bd: (i,), memory_space=pltpu.SMEM)
    grid_spec = pltpu.PrefetchScalarGridSpec(
        num_scalar_prefetch=1,
        grid=(T // tb,),
        in_specs=[smem_blk, smem_blk, pl.BlockSpec((tb, D), lambda i, bd: (i, 0))],
        out_specs=pl.BlockSpec(memory_space=pl.ANY),
        scratch_shapes=[pltpu.VMEM((ROW_BLOCK, D), F32), pltpu.SemaphoreType.DMA(()),
                        pltpu.SemaphoreType.DMA(())],
    )
    return pl.pallas_call(
        _dispatch_kernel,
        grid_spec=grid_spec,
        out_shape=jax.ShapeDtypeStruct((n_rows, D), F32),
        compiler_params=_cparams(("arbitrary",)),
        name="dispatch",
    )(bounds, dest0, dest1, h2)


def _experts_kernel(be_ref, nu_ref, rows_ref, xs_ref, w1_ref, w3_ref, w2_ref, ys_ref,
                    w1f, w3f, w2f, w1b, w3b, w2b, region_ref, wsem):
    i = pl.program_id(0)
    n_used = nu_ref[0]
    e = be_ref[i]
    used = i < n_used
    first = (i == 0) | (e != be_ref[jnp.maximum(i - 1, 0)])

    def weight_copies(expert, slot):
        return (pltpu.make_async_copy(w1_ref.at[expert], w1f.at[slot], wsem.at[slot]),
                pltpu.make_async_copy(w3_ref.at[expert], w3f.at[slot], wsem.at[slot]),
                pltpu.make_async_copy(w2_ref.at[expert], w2f.at[slot], wsem.at[slot]))

    @pl.when(i == 0)
    def _():
        region_ref[0] = 0
        for cp in weight_copies(e, 0):
            cp.start()

    @pl.when(used & first)
    def _():
        slot = region_ref[0] % 2
        region_ref[0] = region_ref[0] + 1
        for cp in weight_copies(e, slot):
            cp.wait()
        w1b[...] = w1f[slot].astype(BF16)
        w3b[...] = w3f[slot].astype(BF16)
        w2b[...] = w2f[slot].astype(BF16)
        nxt = lax.while_loop(lambda j: (j < n_used) & (be_ref[jnp.minimum(j, n_used - 1)] == e),
                             lambda j: j + 1, i + 1)

        @pl.when(nxt < n_used)
        def _():
            for cp in weight_copies(be_ref[jnp.minimum(nxt, n_used - 1)], 1 - slot):
                cp.start()

    def swiglu(n_rows):
        xb = xs_ref[0:n_rows, :].astype(BF16)
        de = w1b.shape[1]
        n_part = 2
        part = de // n_part
        acts = []
        for q in range(n_part):
            cols = slice(q * part, (q + 1) * part)
            h1 = jnp.dot(xb, w1b[:, cols], preferred_element_type=F32)
            h3 = jnp.dot(xb, w3b[:, cols], preferred_element_type=F32)
            acts.append(((h1 * jax.nn.sigmoid(h1)) * h3).astype(BF16))
        y = jnp.dot(acts[0], w2b[0:part, :], preferred_element_type=F32)
        for q in range(1, n_part):
            y = y + jnp.dot(acts[q], w2b[q * part:(q + 1) * part, :], preferred_element_type=F32)
        ys_ref[0:n_rows, :] = y

    n_rows = ys_ref.shape[0]
    half_only = rows_ref[i] <= n_rows // 2

    @pl.when(used & jnp.logical_not(half_only))
    def _():
        swiglu(n_rows)

    @pl.when(used & half_only)
    def _():
        swiglu(n_rows // 2)
        ys_ref[n_rows // 2:, :] = jnp.zeros((n_rows - n_rows // 2, ys_ref.shape[1]), F32)

    @pl.when(jnp.logical_not(used))
    def _():
        ys_ref[...] = jnp.zeros_like(ys_ref)


def _experts(blk_expert, n_used, blk_rows, xs, w1, w3, w2):
    P, D = xs.shape
    de = w1.shape[-1]
    n_blk = P // ROW_BLOCK
    grid_spec = pltpu.PrefetchScalarGridSpec(
        num_scalar_prefetch=3,
        grid=(n_blk,),
        in_specs=[pl.BlockSpec((ROW_BLOCK, D),
                               lambda i, be, nu, br: (jnp.minimum(i, jnp.maximum(nu[0] - 1, 0)), 0)),
                  pl.BlockSpec(memory_space=pl.ANY), pl.BlockSpec(memory_space=pl.ANY),
                  pl.BlockSpec(memory_space=pl.ANY)],
        out_specs=pl.BlockSpec((ROW_BLOCK, D), lambda i, be, nu, br: (i, 0)),
        scratch_shapes=[pltpu.VMEM((2, D, de), F32), pltpu.VMEM((2, D, de), F32),
                        pltpu.VMEM((2, de, D), F32),
                        pltpu.VMEM((D, de), BF16), pltpu.VMEM((D, de), BF16),
                        pltpu.VMEM((de, D), BF16),
                        pltpu.SMEM((1,), jnp.int32), pltpu.SemaphoreType.DMA((2,))],
    )
    return pl.pallas_call(
        _experts_kernel,
        grid_spec=grid_spec,
        out_shape=jax.ShapeDtypeStruct((P, D), F32),
        compiler_params=_cparams(("arbitrary",)),
        name="experts",
    )(blk_expert, n_used, blk_rows, xs, w1, w3, w2)


def _combine_kernel(d0_cur, d1_cur, d0_next, d1_next, x1_ref, route_ref, gfin_ref, ys_ref, o_ref,
                    ybuf, sem):
    i = pl.program_id(0)
    tb = x1_ref.shape[0]
    slot = i % 2

    def issue(dest_refs, buf_slot):
        def start(j, carry):
            base = pl.multiple_of(j * SUBLANES, SUBLANES)
            for k, dest_ref in enumerate(dest_refs):
                tile = ybuf.at[buf_slot, k, pl.ds(base, SUBLANES), :]
                for u in range(SUBLANES):
                    d = dest_ref[base + u]
                    pltpu.make_async_copy(ys_ref.at[pl.ds(d, 1), :], tile.at[pl.ds(u, 1), :],
                                          sem.at[buf_slot]).start(priority=k)
            return carry
        lax.fori_loop(0, tb // SUBLANES, start, 0)

    @pl.when(i == 0)
    def _():
        issue((d0_cur, d1_cur), 0)

    @pl.when(i + 1 < pl.num_programs(0))
    def _():
        issue((d0_next, d1_next), 1 - slot)

    for k in range(2):
        pltpu.make_async_copy(ys_ref.at[pl.ds(0, tb), :], ybuf.at[slot, k], sem.at[slot]).wait()

    route = route_ref[...]
    lane = lax.broadcasted_iota(jnp.int32, route.shape, 1)
    g0 = jnp.sum(jnp.where(lane == ROUTE_G0, route, 0.0), axis=1, keepdims=True)
    g1 = jnp.sum(jnp.where(lane == ROUTE_G1, route, 0.0), axis=1, keepdims=True)
    moe = g0 * ybuf[slot, 0] + g1 * ybuf[slot, 1]
    x2 = x1_ref[...] + moe
    ms = jnp.mean(x2 * x2, axis=-1, keepdims=True)
    o_ref[...] = (x2 * lax.rsqrt(ms + EPS)) * gfin_ref[...]


def _combine(dest0, dest1, x1, route, g_final, ys, *, tb):
    T, D = x1.shape
    nb = T // tb
    cur = pl.BlockSpec((tb,), lambda i: (i,), memory_space=pltpu.SMEM)
    nxt = pl.BlockSpec((tb,), lambda i: (jnp.minimum(i + 1, nb - 1),), memory_space=pltpu.SMEM)
    return pl.pallas_call(
        _combine_kernel,
        grid=(nb,),
        in_specs=[cur, cur, nxt, nxt,
                  pl.BlockSpec((tb, D), lambda i: (i, 0)),
                  pl.BlockSpec((tb, LANES), lambda i: (i, 0)),
                  pl.BlockSpec((1, D), lambda i: (0, 0)),
                  pl.BlockSpec(memory_space=pl.ANY)],
        out_specs=pl.BlockSpec((tb, D), lambda i: (i, 0)),
        out_shape=jax.ShapeDtypeStruct((T, D), F32),
        scratch_shapes=[pltpu.VMEM((2, 2, tb, D), F32), pltpu.SemaphoreType.DMA((2,))],
        compiler_params=_cparams(("arbitrary",)),
        name="combine",
    )(dest0, dest1, dest0, dest1, x1, route, g_final, ys)


def _pick(n, prefs):
    for p in prefs:
        if n % p == 0:
            return p
    return n


def _layer(x2, pos2, batch, seq, attn_norm, w_in, b_forget, b_gate, sinks, w_proj_fox, w_proj_swa,
           w_out, ffn_norm, w_group, b_group, w_expert, b_expert, w1, w3, w2, final_norm):
    T, D = x2.shape
    scale = HEAD_DIM ** -0.5

    sp = np.cumsum([FOX_W, FOX_W, FOX_W, FOX_HEADS, SWA_W, SWA_KV_W, SWA_KV_W, D, D])[:-1]
    wq_f, wk_f, wv_f, w_fl, wq_s, wk_s, wv_s, wg_f, wg_s = jnp.split(w_in, sp, axis=1)
    head_order = np.asarray(SWA_HEAD_ORDER)
    wq_s = wq_s.reshape(D, SWA_HEADS, HEAD_DIM)[:, head_order, :].reshape(D, SWA_W)
    pieces = (wq_f * (scale * LOG2E), wk_f, wv_f, wq_s * (scale * LOG2E),
              jnp.concatenate([wk_s, wv_s], axis=1),
              wg_f, wg_s, jnp.pad(w_fl, ((0, 0), (0, LANES - FOX_HEADS))))
    weights = {name: piece.astype(BF16) for name, piece in zip(W_NAMES, pieces)}
    half = HEAD_DIM // 2
    inv_freq = ROPE_THETA ** (-jnp.arange(half, dtype=F32) * 2.0 / HEAD_DIM)
    invf = jnp.tile(inv_freq, LANES // half).reshape(1, LANES)
    bf_pad = jnp.pad(b_forget.astype(F32), (0, LANES - FOX_HEADS)).reshape(1, LANES)
    sinks_perm = sinks.astype(F32)[head_order] * LOG2E
    wps = w_proj_swa.reshape(SWA_HEADS, HEAD_DIM, D)[head_order].reshape(SWA_W, D).astype(BF16)
    wpf = w_proj_fox.astype(BF16)
    wout = w_out.astype(BF16)
    w_r = jnp.pad(jnp.concatenate([w_group, w_expert], axis=1).astype(F32),
                  ((0, 0), (0, LANES - N_GROUPS - N_EXPERTS)))
    wr_hi = w_r.astype(BF16)
    wr_lo = (w_r - wr_hi.astype(F32)).astype(BF16)
    wr_cat = jnp.concatenate([wr_hi, wr_lo], axis=1)
    b_r = jnp.pad(jnp.concatenate([b_group, b_expert]).astype(F32),
                  (0, LANES - N_GROUPS - N_EXPERTS)).reshape(1, LANES)

    tm1 = _pick(seq, (512, 256, 128))
    qt, kf, vt, qs, ks, vs, g_f, g_s, qxt, kx = _in_proj(
        x2, pos2, attn_norm.reshape(1, D), weights, invf, bf_pad, b_gate.astype(F32),
        batch=batch, seq=seq, tm=tm1)
    tq = _pick(seq, (FOX_TQ, 128))
    tk = min(tq, FOX_TK)
    o_f = _fox_attn(qt, qxt, kf, kx, vt, batch=batch, seq=seq, tq=tq, tk=tk,
                    unroll=min(FOX_UNROLL, tq // tk))
    o_s = _swa_attn(sinks_perm, qs, ks, vs, batch=batch, seq=seq, tq=_pick(seq, (512, 256, 128)))

    tm3 = _pick(T, (512, 256, 128))
    x1, h2, route, route_t, counts = _mix_route(o_f, o_s, g_f, g_s, x2, wpf, wps, wout,
                                       ffn_norm.reshape(1, D), wr_cat, b_r, tm=tm3)

    P = 2 * T + N_EXPERTS * ROW_BLOCK
    n_blk = P // ROW_BLOCK
    dest, tbl, blk = _plan(counts, route_t, n_blk)
    dest0, dest1 = dest[ROUTE_E0], dest[ROUTE_E1]
    pad_end = tbl[1, :N_EXPERTS]
    bounds = jnp.concatenate([jnp.zeros((1,), jnp.int32), pad_end])
    blk_expert, blk_rows = blk[0, :n_blk], blk[1, :n_blk]
    n_used = pad_end[-1:] // ROW_BLOCK

    xs = _dispatch(bounds, dest0, dest1, h2, P, tb=_pick(T, (2048, 1024, 512, 256, 128)))
    ys = _experts(blk_expert, n_used, blk_rows, xs, w1, w3, w2)
    return _combine(dest0, dest1, x1, route, final_norm.reshape(1, D), ys,
                    tb=_pick(T, (512, 256, 128)))


def kernel(x, positions, attn_norm, w_in, b_forget, b_gate, attn_sinks, w_proj_fox, w_proj_swa,
           w_out, ffn_norm, w_group, b_group, w_expert, b_expert, w1, w3, w2, final_norm):
    B, S, D = x.shape
    depth = attn_norm.shape[0]
    x2 = x.reshape(B * S, D)
    pos2 = positions.reshape(B * S, 1).astype(jnp.int32)
    assert depth == 1, "a single layer is followed directly by the final norm"
    out = _layer(x2, pos2, B, S, attn_norm[0], w_in[0], b_forget[0], b_gate[0], attn_sinks[0],
                 w_proj_fox[0], w_proj_swa[0], w_out[0], ffn_norm[0], w_group[0], b_group[0],
                 w_expert[0], b_expert[0], w1[0], w3[0], w2[0], final_norm)
    return out.reshape(B, S, D)
```

```python
import functools

import numpy as np
import jax
import jax.numpy as jnp
from jax import lax
from jax.experimental import pallas as pl
from jax.experimental.pallas import tpu as pltpu

HEAD_DIM = 64
FOX_HEADS = 8
SWA_HEADS = 8
SWA_KV_HEADS = 2
WINDOW = 128
ROPE_THETA = 10000.0
N_GROUPS = 8
EXPERTS_PER_GROUP = 8
N_EXPERTS = N_GROUPS * EXPERTS_PER_GROUP
ROW_BLOCK = 512
SUBLANES = 8
EPS = 1e-6
LANES = 128
NEG_BIG = -1e30
LOG2E = 1.4426950408889634
BIAS_TERMS = 3
BIAS_LANES = 2 * BIAS_TERMS
FOX_TQ = 1024
FOX_TK = 256
FOX_UNROLL = 4
ONES_ROWS = 16

FOX_W = FOX_HEADS * HEAD_DIM
SWA_W = SWA_HEADS * HEAD_DIM
SWA_KV_W = SWA_KV_HEADS * HEAD_DIM
SWA_HEAD_ORDER = (0, 4, 1, 5, 2, 6, 3, 7)

F32 = jnp.float32
BF16 = jnp.bfloat16
VMEM_LIMIT = 56 * 1024 * 1024


def _cparams(sem):
    return pltpu.CompilerParams(dimension_semantics=sem, vmem_limit_bytes=VMEM_LIMIT)


def _split3(x):
    hi = x.astype(BF16)
    r1 = x - hi.astype(F32)
    mid = r1.astype(BF16)
    lo = (r1 - mid.astype(F32)).astype(BF16)
    return hi, mid, lo


W_NAMES = ("q_fox", "k_fox", "v_fox", "q_swa", "kv_swa", "gate_fox", "gate_swa", "forget")


def _in_proj_kernel(x_ref, pos_ref, g_ref, invf_ref, bf_ref, bg_ref, place_ref, ones_ref, *refs):
    w = dict(zip(W_NAMES, refs[:len(W_NAMES)]))
    (qf_ref, kf_ref, vf_ref, qs_ref, ks_ref, vs_ref, gf_ref, gs_ref, qx_ref, kx_ref,
     carry_ref) = refs[len(W_NAMES):]
    j = pl.program_id(1)
    tm = x_ref.shape[0]

    @pl.when(j == 0)
    def _():
        carry_ref[...] = jnp.zeros_like(carry_ref)

    x = x_ref[...]
    ms = jnp.mean(x * x, axis=-1, keepdims=True)
    h = ((x * lax.rsqrt(ms + EPS)) * g_ref[...]).astype(BF16)

    def mm(name):
        return jnp.dot(h, w[name][...], preferred_element_type=F32)

    def store_pairs_t(ref, y):
        for pr in range(ref.shape[0]):
            ref[pr] = y[:, pr * LANES:(pr + 1) * LANES].T.astype(BF16)

    store_pairs_t(qf_ref, mm("q_fox"))
    kf_ref[...] = mm("k_fox").astype(BF16)
    store_pairs_t(vf_ref, mm("v_fox"))

    ang = pos_ref[...].astype(F32) * invf_ref[...]
    cos = jnp.cos(ang)
    sin = jnp.sin(ang)
    n_rep = SWA_W // LANES
    cos_q = jnp.concatenate([cos] * n_rep, axis=1)
    sin_q = jnp.concatenate([sin] * n_rep, axis=1)
    def rot_half(y):
        width = y.shape[1]
        lane_y = lax.broadcasted_iota(jnp.int32, y.shape, 1)
        first = (lane_y & (HEAD_DIM - 1)) < HEAD_DIM // 2
        return jnp.where(first, -pltpu.roll(y, width - HEAD_DIM // 2, axis=1),
                         pltpu.roll(y, HEAD_DIM // 2, axis=1))

    q_s = mm("q_swa")
    kv_s = mm("kv_swa")
    k_s = kv_s[:, :SWA_KV_W]
    qs_ref[...] = (q_s * cos_q + rot_half(q_s) * sin_q).astype(BF16)
    ks_ref[...] = (k_s * cos + rot_half(k_s) * sin).astype(BF16)
    vs_ref[...] = kv_s[:, SWA_KV_W:].astype(BF16)

    gf_ref[...] = jax.nn.sigmoid(mm("gate_fox") + bg_ref[0:1, :]).astype(BF16)
    gs_ref[...] = jax.nn.sigmoid(mm("gate_swa") + bg_ref[1:2, :]).astype(BF16)

    z = mm("forget") + bf_ref[...]
    lane = lax.broadcasted_iota(jnp.int32, (tm, LANES), 1)
    lf = jnp.minimum(z, 0.0) - jnp.log1p(jnp.exp(-jnp.abs(z)))
    lf = jnp.where(lane < FOX_HEADS, lf, 0.0)
    row = lax.broadcasted_iota(jnp.int32, (tm, tm), 0)
    col = lax.broadcasted_iota(jnp.int32, (tm, tm), 1)
    tri = jnp.where(row >= col, 1.0, 0.0).astype(BF16)
    hi, mid, lo = _split3(lf)
    sums = jnp.dot(tri, jnp.concatenate([hi, mid, lo], axis=1), preferred_element_type=F32)
    c = (sums[:, :LANES] + sums[:, LANES:2 * LANES] + sums[:, 2 * LANES:]) + carry_ref[0:1, :]
    carry_ref[0:1, :] = c[tm - 1:tm, :]
    chi, cmid, clo = _split3(c * LOG2E)
    ext = jnp.dot(jnp.concatenate([chi, cmid, clo], axis=1), place_ref[...],
                  preferred_element_type=F32) + ones_ref[...]
    qx_ref[...] = ext[:, :LANES].T.astype(BF16)
    kx_ref[...] = ext[:, LANES:].astype(BF16)


def _bias_placement():
    place = np.zeros((BIAS_TERMS * LANES, 2 * LANES), np.float32)
    ones = np.zeros((1, 2 * LANES), np.float32)
    for h in range(FOX_HEADS):
        for t in range(BIAS_TERMS):
            place[t * LANES + h, BIAS_LANES * h + t] = 1.0
            place[t * LANES + h, LANES + BIAS_LANES * h + BIAS_TERMS + t] = -1.0
            ones[0, BIAS_LANES * h + BIAS_TERMS + t] = 1.0
            ones[0, LANES + BIAS_LANES * h + t] = 1.0
    return jnp.asarray(place, BF16), jnp.asarray(ones, F32)


def _in_proj(x2, pos2, g_attn, weights, invf, bf_pad, b_gate, *, batch, seq, tm):
    T, D = x2.shape
    nj = seq // tm
    w_list = [weights[name] for name in W_NAMES]
    place, ones = _bias_placement()
    row_blk = lambda w: pl.BlockSpec((tm, w), lambda b, j: (b * nj + j, 0))
    const = lambda shape: pl.BlockSpec(shape, lambda b, j: (0, 0))
    n_pair = FOX_HEADS // 2
    pair_t = pl.BlockSpec((None, n_pair, LANES, tm), lambda b, j: (b, 0, 0, j))
    out_shapes = (
        jax.ShapeDtypeStruct((batch, n_pair, LANES, seq), BF16), jax.ShapeDtypeStruct((T, FOX_W), BF16),
        jax.ShapeDtypeStruct((batch, n_pair, LANES, seq), BF16), jax.ShapeDtypeStruct((T, SWA_W), BF16),
        jax.ShapeDtypeStruct((T, SWA_KV_W), BF16), jax.ShapeDtypeStruct((T, SWA_KV_W), BF16),
        jax.ShapeDtypeStruct((T, D), BF16), jax.ShapeDtypeStruct((T, D), BF16),
        jax.ShapeDtypeStruct((batch, LANES, seq), BF16), jax.ShapeDtypeStruct((T, LANES), BF16),
    )
    return pl.pallas_call(
        _in_proj_kernel,
        grid=(batch, nj),
        in_specs=[row_blk(D), row_blk(1), const((1, D)), const((1, LANES)),
                  const((1, LANES)), const((2, D)), const(place.shape), const(ones.shape)]
                 + [const(wt.shape) for wt in w_list],
        out_specs=(pair_t, row_blk(FOX_W), pair_t, row_blk(SWA_W),
                   row_blk(SWA_KV_W), row_blk(SWA_KV_W), row_blk(D), row_blk(D),
                   pl.BlockSpec((None, LANES, tm), lambda b, j: (b, 0, j)), row_blk(LANES)),
        out_shape=out_shapes,
        scratch_shapes=[pltpu.VMEM((8, LANES), F32)],
        compiler_params=_cparams(("arbitrary", "arbitrary")),
        name="in_proj",
    )(x2, pos2, g_attn, invf, bf_pad, b_gate, place, ones, *w_list)


def _fox_kernel(q_ref, qx_ref, k_ref, kx_ref, vt_ref, o_ref, acc_ref, *, tk, unroll):
    p = pl.program_id(1)
    i = pl.program_id(2)
    tq = q_ref.shape[1]
    n_diag = tq // tk
    qt = q_ref[...]
    qxt = qx_ref[...]
    row = lax.broadcasted_iota(jnp.int32, (LANES, tq), 0)
    zero = jnp.zeros_like(qt)
    q_ext = []
    for hh in range(2):
        first = BIAS_LANES * (2 * p + hh)
        head_rows = (row < HEAD_DIM) if hh == 0 else (row >= HEAD_DIM)
        bias_rows = (row >= first) & (row < first + BIAS_LANES)
        q_ext.append(jnp.concatenate([jnp.where(head_rows, qt, zero),
                                      jnp.where(bias_rows, qxt, zero)], axis=0))

    acc_ref[...] = jnp.zeros_like(acc_ref)

    def scores(ks, c0, masked):
        k_ext = jnp.concatenate([k_ref[pl.ds(ks, tk), :], kx_ref[pl.ds(ks, tk), :]], axis=1)
        out = []
        for hh in range(2):
            s = jnp.dot(k_ext, q_ext[hh][:, c0:], preferred_element_type=F32)
            if masked:
                key = lax.broadcasted_iota(jnp.int32, s.shape, 0)
                qry = lax.broadcasted_iota(jnp.int32, s.shape, 1)
                s = jnp.where(key <= qry, s, -jnp.inf)
            out.append(s)
        return out

    ones_rows = jnp.ones((ONES_ROWS, tk), BF16)

    def accumulate(ks, c0, s_pair, stats):
        vt = jnp.concatenate([vt_ref[:, pl.ds(ks, tk)], ones_rows], axis=0)
        out = []
        for hh in range(2):
            m_all = stats[hh]
            m_old = m_all[:, c0:]
            s = s_pair[hh]
            m_new = jnp.maximum(m_old, jnp.max(s, axis=0, keepdims=True))
            alpha = jnp.exp2(m_old - m_new)
            pm = jnp.exp2(s - m_new)
            acc_ref[hh, :, c0:] = alpha * acc_ref[hh, :, c0:] + jnp.dot(
                vt, pm.astype(BF16), preferred_element_type=F32)
            if c0:
                m_new = jnp.concatenate([m_all[:, :c0], m_new], axis=1)
            out.append(m_new)
        return tuple(out)

    def run_tiles(first_tile, count, stats, diagonal):
        start = lambda u: pl.multiple_of((first_tile + u) * tk, tk)
        col0 = lambda u: u * tk if diagonal else 0
        s_next = scores(start(0), col0(0), diagonal)
        for u in range(count):
            s_cur = s_next
            if u + 1 < count:
                s_next = scores(start(u + 1), col0(u + 1), diagonal)
            stats = accumulate(start(u), col0(u), s_cur, stats)
        return stats

    init = tuple(jnp.full((1, tq), NEG_BIG, F32) for _ in range(2))
    stats = lax.fori_loop(0, i * (n_diag // unroll),
                          lambda jb, st: run_tiles(jb * unroll, unroll, st, False), init)
    stats = run_tiles(i * n_diag, n_diag, stats, True)

    del stats
    o_t = jnp.where(row < HEAD_DIM, acc_ref[0, 0:LANES, :] * (1.0 / acc_ref[0, LANES:LANES + 1, :]),
                    acc_ref[1, 0:LANES, :] * (1.0 / acc_ref[1, LANES:LANES + 1, :]))
    o_ref[...] = o_t.T.astype(BF16)


def _fox_attn(qt, qxt, kf, kx, vt, *, batch, seq, tq, tk, unroll):
    T = kf.shape[0]
    nq = seq // tq
    n_pair = FOX_HEADS // 2
    return pl.pallas_call(
        functools.partial(_fox_kernel, tk=tk, unroll=unroll),
        grid=(batch, n_pair, nq),
        in_specs=[
            pl.BlockSpec((None, None, LANES, tq), lambda b, p, i: (b, p, 0, i)),
            pl.BlockSpec((None, LANES, tq), lambda b, p, i: (b, 0, i)),
            pl.BlockSpec((seq, LANES), lambda b, p, i: (b, p)),
            pl.BlockSpec((seq, LANES), lambda b, p, i: (b, 0)),
            pl.BlockSpec((None, None, LANES, seq), lambda b, p, i: (b, p, 0, 0)),
        ],
        out_specs=pl.BlockSpec((tq, LANES), lambda b, p, i: (b * nq + i, p)),
        out_shape=jax.ShapeDtypeStruct((T, FOX_W), BF16),
        scratch_shapes=[pltpu.VMEM((2, LANES + ONES_ROWS, tq), F32)],
        compiler_params=_cparams(("arbitrary", "arbitrary", "arbitrary")),
        name="fox_attn",
    )(qt, qxt, kf, kx, vt)


def _swa_kernel(sink_ref, q_ref, kc_ref, kp_ref, vc_ref, vp_ref, o_ref):
    i = pl.program_id(1)
    tq = q_ref.shape[0]
    n_win = tq // WINDOW
    n_col = SWA_W // LANES
    lane = lax.broadcasted_iota(jnp.int32, (WINDOW, LANES), 1)
    low = lane < HEAD_DIM
    r = lax.broadcasted_iota(jnp.int32, (WINDOW, 2 * WINDOW), 0)
    c = lax.broadcasted_iota(jnp.int32, (WINDOW, 2 * WINDOW), 1)
    dist = WINDOW + r - c
    band = (dist >= 0) & (dist < WINDOW)
    for w in range(n_win):
        if w == 0:
            k_win = jnp.concatenate([kp_ref[...], kc_ref[0:WINDOW, :]], axis=0)
            v_win = jnp.concatenate([vp_ref[...], vc_ref[0:WINDOW, :]], axis=0)
            first = i == 0
            valid = band & ((c >= WINDOW) | jnp.logical_not(first))
        else:
            k_win = kc_ref[(w - 1) * WINDOW:(w + 1) * WINDOW, :]
            v_win = vc_ref[(w - 1) * WINDOW:(w + 1) * WINDOW, :]
            valid = band
        for col_i in range(n_col):
            qc = q_ref[w * WINDOW:(w + 1) * WINDOW, col_i * LANES:(col_i + 1) * LANES]
            outs = []
            for half in range(2):
                qm = jnp.where(low if half == 0 else jnp.logical_not(low), qc, jnp.zeros_like(qc))
                s = lax.dot_general(qm, k_win, (((1,), (1,)), ((), ())),
                                    preferred_element_type=F32)
                s = jnp.where(valid, s, -jnp.inf)
                sink = sink_ref[2 * col_i + half]
                m = jnp.maximum(jnp.max(s, axis=1, keepdims=True), sink)
                pm = jnp.exp2(s - m)
                denom = jnp.sum(pm, axis=1, keepdims=True) + jnp.exp2(sink - m)
                outs.append(jnp.dot(pm.astype(BF16), v_win, preferred_element_type=F32)
                            * (1.0 / denom))
            o_ref[w * WINDOW:(w + 1) * WINDOW, col_i * LANES:(col_i + 1) * LANES] = (
                jnp.where(low, outs[0], outs[1]).astype(BF16))


def _swa_attn(sinks_perm, qs, ks, vs, *, batch, seq, tq):
    T = qs.shape[0]
    nq = seq // tq
    n_win = tq // WINDOW
    nb = seq // WINDOW
    cur = lambda w: pl.BlockSpec((tq, w), lambda b, i: (b * nq + i, 0))
    prev = pl.BlockSpec((WINDOW, SWA_KV_W),
                        lambda b, i: (jnp.maximum(b * nb + i * n_win - 1, 0), 0))
    return pl.pallas_call(
        _swa_kernel,
        grid=(batch, nq),
        in_specs=[pl.BlockSpec(memory_space=pltpu.SMEM), cur(SWA_W), cur(SWA_KV_W), prev,
                  cur(SWA_KV_W), prev],
        out_specs=cur(SWA_W),
        out_shape=jax.ShapeDtypeStruct((T, SWA_W), BF16),
        compiler_params=_cparams(("arbitrary", "arbitrary")),
        name="swa_attn",
    )(sinks_perm, qs, ks, ks, vs, vs)


ROUTE_E0, ROUTE_E1, ROUTE_G0, ROUTE_G1, ROUTE_R0, ROUTE_R1 = range(6)
ROUTE_ROWS = 8


def _mix_route_kernel(of_ref, os_ref, gf_ref, gs_ref, x_ref, wpf_ref, wps_ref, wout_ref, gn_ref,
                      wr_ref, br_ref, x1_ref, h2_ref, route_ref, route_t_ref, cnt_ref,
                      carry_ref, logits_ref):
    step = pl.program_id(0)
    tm = x_ref.shape[0]

    @pl.when(step == 0)
    def _():
        carry_ref[...] = jnp.zeros_like(carry_ref)
        logits_ref[...] = jnp.zeros_like(logits_ref)

    logits = logits_ref[...]

    a = jnp.dot(of_ref[...], wpf_ref[...], preferred_element_type=F32)
    b = jnp.dot(os_ref[...], wps_ref[...], preferred_element_type=F32)
    merged = gf_ref[...].astype(F32) * a + gs_ref[...].astype(F32) * b
    y = jnp.dot(merged.astype(BF16), wout_ref[...], preferred_element_type=F32)
    x1 = x_ref[...] + y
    x1_ref[...] = x1
    ms = jnp.mean(x1 * x1, axis=-1, keepdims=True)
    h2 = (x1 * lax.rsqrt(ms + EPS)) * gn_ref[...]
    h2_ref[...] = h2

    h_hi = h2.astype(BF16)
    h_lo = (h2 - h_hi.astype(F32)).astype(BF16)
    wr = wr_ref[...]
    r_hi = jnp.dot(h_hi, wr, preferred_element_type=F32)
    logits_ref[...] = (r_hi[:, :LANES] + r_hi[:, LANES:]
                       + jnp.dot(h_lo, wr[:, :LANES], preferred_element_type=F32)) + br_ref[...]

    lane = lax.broadcasted_iota(jnp.int32, (tm, LANES), 1)
    is_g = lane < N_GROUPS
    gl = jnp.where(is_g, logits, -jnp.inf)
    gmax = jnp.max(gl, axis=1, keepdims=True)
    gexp = jnp.where(is_g, jnp.exp(gl - gmax), 0.0)
    gprob = gexp / jnp.sum(gexp, axis=1, keepdims=True)
    g_w = jnp.max(gprob, axis=1, keepdims=True)
    g_idx = jnp.min(jnp.where(is_g & (gprob == g_w), lane, LANES), axis=1, keepdims=True)

    e_lane = lane - N_GROUPS
    in_grp = (e_lane >= 0) & (e_lane < N_EXPERTS) & ((e_lane >> 3) == g_idx)
    el = jnp.where(in_grp, logits, -jnp.inf)
    v0 = jnp.max(el, axis=1, keepdims=True)
    i0 = jnp.min(jnp.where(in_grp & (el == v0), lane, LANES), axis=1, keepdims=True)
    el1 = jnp.where(lane == i0, -jnp.inf, el)
    v1 = jnp.max(el1, axis=1, keepdims=True)
    i1 = jnp.min(jnp.where(in_grp & (lane != i0) & (el1 == v1), lane, LANES), axis=1, keepdims=True)
    t = jnp.exp(v1 - v0)
    den = 1.0 + t
    gate0 = g_w * (1.0 / den)
    gate1 = g_w * (t / den)
    e0 = i0 - N_GROUPS
    e1 = i1 - N_GROUPS

    oh0 = jnp.where(lane == e0, 1.0, 0.0)
    oh1 = jnp.where(lane == e1, 1.0, 0.0)
    row = lax.broadcasted_iota(jnp.int32, (tm, tm), 0)
    col = lax.broadcasted_iota(jnp.int32, (tm, tm), 1)
    tri = jnp.where(row > col, 1.0, 0.0).astype(BF16)
    cs0 = jnp.sum(oh0, axis=0, keepdims=True)
    cs1 = jnp.sum(oh1, axis=0, keepdims=True)
    carry = carry_ref[0:1, :]
    pre = jnp.dot(tri, jnp.concatenate([oh0, oh1], axis=1).astype(BF16),
                  preferred_element_type=F32)
    pre0 = pre[:, :LANES] + carry
    pre1 = pre[:, LANES:] + (carry + cs0)
    rank0 = jnp.sum(oh0 * pre0, axis=1, keepdims=True)
    rank1 = jnp.sum(oh1 * pre1, axis=1, keepdims=True)
    new_carry = carry + cs0 + cs1
    carry_ref[0:1, :] = new_carry
    cnt_ref[...] = jnp.broadcast_to(new_carry, cnt_ref.shape)

    route = jnp.where(lane == ROUTE_E0, e0.astype(F32), 0.0)
    route = jnp.where(lane == ROUTE_E1, e1.astype(F32), route)
    route = jnp.where(lane == ROUTE_G0, gate0, route)
    route = jnp.where(lane == ROUTE_G1, gate1, route)
    route = jnp.where(lane == ROUTE_R0, rank0, route)
    route = jnp.where(lane == ROUTE_R1, rank1, route)
    route_ref[...] = route
    route_t_ref[...] = route.T[0:route_t_ref.shape[0], :]

    @pl.when(step == 0)
    def _():
        carry_ref[...] = jnp.zeros_like(carry_ref)


def _mix_route(o_f, o_s, g_f, g_s, x2, wpf, wps, wout, g_ffn, wr_cat, b_r, *, tm):
    T, D = x2.shape
    nb = T // tm
    cur = lambda i: jnp.minimum(i, nb - 1)
    prev = lambda i: jnp.maximum(i - 1, 0)
    row_blk = lambda w: pl.BlockSpec((tm, w), lambda i: (cur(i), 0))
    const = lambda shape: pl.BlockSpec(shape, lambda i: (0, 0))
    return pl.pallas_call(
        _mix_route_kernel,
        grid=(nb + 1,),
        in_specs=[row_blk(FOX_W), row_blk(SWA_W), row_blk(D), row_blk(D), row_blk(D),
                  const((FOX_W, D)), const((SWA_W, D)), const((D, D)), const((1, D)),
                  const((D, 2 * LANES)), const((1, LANES))],
        out_specs=(row_blk(D), row_blk(D), pl.BlockSpec((tm, LANES), lambda i: (prev(i), 0)),
                   pl.BlockSpec((ROUTE_ROWS, tm), lambda i: (0, prev(i))), const((8, LANES))),
        out_shape=(jax.ShapeDtypeStruct((T, D), F32), jax.ShapeDtypeStruct((T, D), F32),
                   jax.ShapeDtypeStruct((T, LANES), F32), jax.ShapeDtypeStruct((ROUTE_ROWS, T), F32),
                   jax.ShapeDtypeStruct((8, LANES), F32)),
        scratch_shapes=[pltpu.VMEM((8, LANES), F32), pltpu.VMEM((tm, LANES), F32)],
        compiler_params=_cparams(("arbitrary",)),
        name="mix_route",
    )(o_f, o_s, g_f, g_s, x2, wpf, wps, wout, g_ffn, wr_cat, b_r)


def _plan_kernel(cnt_ref, route_t_ref, dest_ref, tbl_ref, blk_ref):
    lane = lax.broadcasted_iota(jnp.int32, cnt_ref.shape, 1)
    cnt = jnp.where(lane < N_EXPERTS, cnt_ref[...], 0.0)
    padded = jnp.floor((cnt + (ROW_BLOCK - 1)) * (1.0 / ROW_BLOCK)) * ROW_BLOCK
    pad_end = padded
    shift = 1
    while shift < N_EXPERTS:
        pad_end = pad_end + jnp.where(lane >= shift, pltpu.roll(pad_end, shift, axis=1), 0.0)
        shift *= 2
    pad_start = pad_end - padded
    row = lax.broadcasted_iota(jnp.int32, cnt_ref.shape, 0)
    tbl_ref[...] = jnp.where(row == 0, pad_start, pad_end).astype(jnp.int32)

    fields = route_t_ref[...]
    start_of = jnp.zeros_like(fields)
    for e in range(N_EXPERTS):
        start_of = jnp.where(fields == float(e), pad_start[0:1, e:e + 1], start_of)
    ranks = pltpu.roll(fields, ROUTE_ROWS - ROUTE_R0, axis=0)
    dest_ref[...] = (start_of + ranks).astype(jnp.int32)

    ends = pad_end.T[0:N_EXPERTS, 0:1]
    n_col = blk_ref.shape[1]
    blk_pos = (lax.broadcasted_iota(jnp.int32, (N_EXPERTS, n_col), 1) * ROW_BLOCK).astype(F32)
    owner = jnp.minimum(jnp.sum(jnp.where(ends <= blk_pos, 1.0, 0.0), axis=0, keepdims=True),
                        N_EXPERTS - 1.0)
    token_end = (pad_start + cnt).T[0:N_EXPERTS, 0:1]
    expert = lax.broadcasted_iota(jnp.int32, (N_EXPERTS, n_col), 0).astype(F32)
    owner_end = jnp.sum(jnp.where(expert == owner, token_end, 0.0), axis=0, keepdims=True)
    valid = jnp.clip(owner_end - blk_pos[0:1, :], 0.0, float(ROW_BLOCK))
    blk_row = lax.broadcasted_iota(jnp.int32, blk_ref.shape, 0)
    blk_ref[...] = jnp.where(blk_row == 0, owner, valid).astype(jnp.int32)


def _plan(counts, route_t, n_blk):
    T = route_t.shape[1]
    n_col = (n_blk + LANES - 1) // LANES * LANES
    full = lambda shape: pl.BlockSpec(shape, lambda i: (0, 0))
    return pl.pallas_call(
        _plan_kernel,
        grid=(1,),
        in_specs=[full((8, LANES)), full((ROUTE_ROWS, T))],
        out_specs=(full((ROUTE_ROWS, T)), full((8, LANES)), full((8, n_col))),
        out_shape=(jax.ShapeDtypeStruct((ROUTE_ROWS, T), jnp.int32),
                   jax.ShapeDtypeStruct((8, LANES), jnp.int32),
                   jax.ShapeDtypeStruct((8, n_col), jnp.int32)),
        compiler_params=_cparams(("arbitrary",)),
        name="plan",
    )(counts, route_t)


def _dispatch_kernel(bounds_ref, dest0_ref, dest1_ref, h_ref, xs_ref, zbuf, sem, zsem):
    step = pl.program_id(0)
    tb = h_ref.shape[0]

    @pl.when(step == 0)
    def _():
        zbuf[...] = jnp.zeros_like(zbuf)

        def tail_copy(e):
            tail = pl.multiple_of(bounds_ref[e + 1] - ROW_BLOCK, ROW_BLOCK)
            return pltpu.make_async_copy(zbuf, xs_ref.at[pl.ds(tail, ROW_BLOCK), :], zsem)

        def for_nonempty(action):
            def body(e, carry):
                @pl.when(bounds_ref[e + 1] > bounds_ref[e])
                def _():
                    action(tail_copy(e))
                return carry
            lax.fori_loop(0, N_EXPERTS, body, 0)

        n_rows = xs_ref.shape[0]
        first_unused = bounds_ref[N_EXPERTS] // ROW_BLOCK

        def unused_copy(blk):
            return pltpu.make_async_copy(
                zbuf, xs_ref.at[pl.ds(pl.multiple_of(blk * ROW_BLOCK, ROW_BLOCK), ROW_BLOCK), :], zsem)

        def for_unused(action):
            def body(blk, carry):
                action(unused_copy(blk))
                return carry
            lax.fori_loop(first_unused, n_rows // ROW_BLOCK, body, 0)

        for_nonempty(lambda cp: cp.start())
        for_unused(lambda cp: cp.start())
        for_nonempty(lambda cp: cp.wait())
        for_unused(lambda cp: cp.wait())

    def start(j, carry):
        base = pl.multiple_of(j * SUBLANES, SUBLANES)
        tile = h_ref.at[pl.ds(base, SUBLANES), :]
        for u in range(SUBLANES):
            for k, dest_ref in enumerate((dest0_ref, dest1_ref)):
                d = dest_ref[base + u]
                pltpu.make_async_copy(tile.at[pl.ds(u, 1), :], xs_ref.at[pl.ds(d, 1), :],
                                      sem).start(priority=k)
        return carry

    lax.fori_loop(0, tb // SUBLANES, start, 0)
    for _ in range(2):
        pltpu.make_async_copy(h_ref, xs_ref.at[pl.ds(0, tb), :], sem).wait()


def _dispatch(bounds, dest0, dest1, h2, n_rows, *, tb):
    T, D = h2.shape
    smem_blk = pl.BlockSpec((tb,), lambda i, bd: (i,), memory_space=pltpu.SMEM)
    grid_spec = pltpu.PrefetchScalarGridSpec(
        num_scalar_prefetch=1,
        grid=(T // tb,),
        in_specs=[smem_blk, smem_blk, pl.BlockSpec((tb, D), lambda i, bd: (i, 0))],
        out_specs=pl.BlockSpec(memory_space=pl.ANY),
        scratch_shapes=[pltpu.VMEM((ROW_BLOCK, D), F32), pltpu.SemaphoreType.DMA(()),
                        pltpu.SemaphoreType.DMA(())],
    )
    return pl.pallas_call(
        _dispatch_kernel,
        grid_spec=grid_spec,
        out_shape=jax.ShapeDtypeStruct((n_rows, D), F32),
        compiler_params=_cparams(("arbitrary",)),
        name="dispatch",
    )(bounds, dest0, dest1, h2)


def _experts_kernel(be_ref, nu_ref, rows_ref, xs_ref, w1_ref, w3_ref, w2_ref, ys_ref,
                    w1f, w3f, w2f, w1b, w3b, w2b, region_ref, wsem):
    i = pl.program_id(0)
    n_used = nu_ref[0]
    e = be_ref[i]
    used = i < n_used
    first = (i == 0) | (e != be_ref[jnp.maximum(i - 1, 0)])

    def weight_copies(expert, slot):
        return (pltpu.make_async_copy(w1_ref.at[expert], w1f.at[slot], wsem.at[slot]),
                pltpu.make_async_copy(w3_ref.at[expert], w3f.at[slot], wsem.at[slot]),
                pltpu.make_async_copy(w2_ref.at[expert], w2f.at[slot], wsem.at[slot]))

    @pl.when(i == 0)
    def _():
        region_ref[0] = 0
        for cp in weight_copies(e, 0):
            cp.start()

    @pl.when(used & first)
    def _():
        slot = region_ref[0] % 2
        region_ref[0] = region_ref[0] + 1
        for cp in weight_copies(e, slot):
            cp.wait()
        w1b[...] = w1f[slot].astype(BF16)
        w3b[...] = w3f[slot].astype(BF16)
        w2b[...] = w2f[slot].astype(BF16)
        nxt = lax.while_loop(lambda j: (j < n_used) & (be_ref[jnp.minimum(j, n_used - 1)] == e),
                             lambda j: j + 1, i + 1)

        @pl.when(nxt < n_used)
        def _():
            for cp in weight_copies(be_ref[jnp.minimum(nxt, n_used - 1)], 1 - slot):
                cp.start()

    def swiglu(n_rows):
        xb = xs_ref[0:n_rows, :].astype(BF16)
        de = w1b.shape[1]
        n_part = 2
        part = de // n_part
        acts = []
        for q in range(n_part):
            cols = slice(q * part, (q + 1) * part)
            h1 = jnp.dot(xb, w1b[:, cols], preferred_element_type=F32)
            h3 = jnp.dot(xb, w3b[:, cols], preferred_element_type=F32)
            acts.append(((h1 * jax.nn.sigmoid(h1)) * h3).astype(BF16))
        y = jnp.dot(acts[0], w2b[0:part, :], preferred_element_type=F32)
        for q in range(1, n_part):
            y = y + jnp.dot(acts[q], w2b[q * part:(q + 1) * part, :], preferred_element_type=F32)
        ys_ref[0:n_rows, :] = y

    n_rows = ys_ref.shape[0]
    quarter = n_rows // 4
    n_quarters = jnp.clip((rows_ref[i] + quarter - 1) // quarter, 1, 4)
    for nq in range(1, 5):
        @pl.when(used & (n_quarters == nq))
        def _(nq=nq):
            swiglu(nq * quarter)
            if nq < 4:
                ys_ref[nq * quarter:, :] = jnp.zeros((n_rows - nq * quarter, ys_ref.shape[1]), F32)

    @pl.when(jnp.logical_not(used))
    def _():
        ys_ref[...] = jnp.zeros_like(ys_ref)


def _experts(blk_expert, n_used, blk_rows, xs, w1, w3, w2):
    P, D = xs.shape
    de = w1.shape[-1]
    n_blk = P // ROW_BLOCK
    grid_spec = pltpu.PrefetchScalarGridSpec(
        num_scalar_prefetch=3,
        grid=(n_blk,),
        in_specs=[pl.BlockSpec((ROW_BLOCK, D),
                               lambda i, be, nu, br: (jnp.minimum(i, jnp.maximum(nu[0] - 1, 0)), 0)),
                  pl.BlockSpec(memory_space=pl.ANY), pl.BlockSpec(memory_space=pl.ANY),
                  pl.BlockSpec(memory_space=pl.ANY)],
        out_specs=pl.BlockSpec((ROW_BLOCK, D), lambda i, be, nu, br: (i, 0)),
        scratch_shapes=[pltpu.VMEM((2, D, de), F32), pltpu.VMEM((2, D, de), F32),
                        pltpu.VMEM((2, de, D), F32),
                        pltpu.VMEM((D, de), BF16), pltpu.VMEM((D, de), BF16),
                        pltpu.VMEM((de, D), BF16),
                        pltpu.SMEM((1,), jnp.int32), pltpu.SemaphoreType.DMA((2,))],
    )
    return pl.pallas_call(
        _experts_kernel,
        grid_spec=grid_spec,
        out_shape=jax.ShapeDtypeStruct((P, D), F32),
        compiler_params=_cparams(("arbitrary",)),
        name="experts",
    )(blk_expert, n_used, blk_rows, xs, w1, w3, w2)


def _combine_kernel(d0_cur, d1_cur, d0_next, d1_next, x1_ref, route_ref, gfin_ref, ys_ref, o_ref,
                    ybuf, sem):
    i = pl.program_id(0)
    tb = x1_ref.shape[0]
    slot = i % 2

    def issue(dest_refs, buf_slot):
        def start(j, carry):
            base = pl.multiple_of(j * SUBLANES, SUBLANES)
            for k, dest_ref in enumerate(dest_refs):
                tile = ybuf.at[buf_slot, k, pl.ds(base, SUBLANES), :]
                for u in range(SUBLANES):
                    d = dest_ref[base + u]
                    pltpu.make_async_copy(ys_ref.at[pl.ds(d, 1), :], tile.at[pl.ds(u, 1), :],
                                          sem.at[buf_slot]).start(priority=k)
            return carry
        lax.fori_loop(0, tb // SUBLANES, start, 0)

    @pl.when(i == 0)
    def _():
        issue((d0_cur, d1_cur), 0)

    @pl.when(i + 1 < pl.num_programs(0))
    def _():
        issue((d0_next, d1_next), 1 - slot)

    for k in range(2):
        pltpu.make_async_copy(ys_ref.at[pl.ds(0, tb), :], ybuf.at[slot, k], sem.at[slot]).wait()

    route = route_ref[...]
    lane = lax.broadcasted_iota(jnp.int32, route.shape, 1)
    g0 = jnp.sum(jnp.where(lane == ROUTE_G0, route, 0.0), axis=1, keepdims=True)
    g1 = jnp.sum(jnp.where(lane == ROUTE_G1, route, 0.0), axis=1, keepdims=True)
    moe = g0 * ybuf[slot, 0] + g1 * ybuf[slot, 1]
    x2 = x1_ref[...] + moe
    ms = jnp.mean(x2 * x2, axis=-1, keepdims=True)
    o_ref[...] = (x2 * lax.rsqrt(ms + EPS)) * gfin_ref[...]


def _combine(dest0, dest1, x1, route, g_final, ys, *, tb):
    T, D = x1.shape
    nb = T // tb
    cur = pl.BlockSpec((tb,), lambda i: (i,), memory_space=pltpu.SMEM)
    nxt = pl.BlockSpec((tb,), lambda i: (jnp.minimum(i + 1, nb - 1),), memory_space=pltpu.SMEM)
    return pl.pallas_call(
        _combine_kernel,
        grid=(nb,),
        in_specs=[cur, cur, nxt, nxt,
                  pl.BlockSpec((tb, D), lambda i: (i, 0)),
                  pl.BlockSpec((tb, LANES), lambda i: (i, 0)),
                  pl.BlockSpec((1, D), lambda i: (0, 0)),
                  pl.BlockSpec(memory_space=pl.ANY)],
        out_specs=pl.BlockSpec((tb, D), lambda i: (i, 0)),
        out_shape=jax.ShapeDtypeStruct((T, D), F32),
        scratch_shapes=[pltpu.VMEM((2, 2, tb, D), F32), pltpu.SemaphoreType.DMA((2,))],
        compiler_params=_cparams(("arbitrary",)),
        name="combine",
    )(dest0, dest1, dest0, dest1, x1, route, g_final, ys)


def _pick(n, prefs):
    for p in prefs:
        if n % p == 0:
            return p
    return n


def _layer(x2, pos2, batch, seq, attn_norm, w_in, b_forget, b_gate, sinks, w_proj_fox, w_proj_swa,
           w_out, ffn_norm, w_group, b_group, w_expert, b_expert, w1, w3, w2, final_norm):
    T, D = x2.shape
    scale = HEAD_DIM ** -0.5

    sp = np.cumsum([FOX_W, FOX_W, FOX_W, FOX_HEADS, SWA_W, SWA_KV_W, SWA_KV_W, D, D])[:-1]
    wq_f, wk_f, wv_f, w_fl, wq_s, wk_s, wv_s, wg_f, wg_s = jnp.split(w_in, sp, axis=1)
    head_order = np.asarray(SWA_HEAD_ORDER)
    wq_s = wq_s.reshape(D, SWA_HEADS, HEAD_DIM)[:, head_order, :].reshape(D, SWA_W)
    pieces = (wq_f * (scale * LOG2E), wk_f, wv_f, wq_s * (scale * LOG2E),
              jnp.concatenate([wk_s, wv_s], axis=1),
              wg_f, wg_s, jnp.pad(w_fl, ((0, 0), (0, LANES - FOX_HEADS))))
    weights = {name: piece.astype(BF16) for name, piece in zip(W_NAMES, pieces)}
    half = HEAD_DIM // 2
    inv_freq = ROPE_THETA ** (-jnp.arange(half, dtype=F32) * 2.0 / HEAD_DIM)
    invf = jnp.tile(inv_freq, LANES // half).reshape(1, LANES)
    bf_pad = jnp.pad(b_forget.astype(F32), (0, LANES - FOX_HEADS)).reshape(1, LANES)
    sinks_perm = sinks.astype(F32)[head_order] * LOG2E
    wps = w_proj_swa.reshape(SWA_HEADS, HEAD_DIM, D)[head_order].reshape(SWA_W, D).astype(BF16)
    wpf = w_proj_fox.astype(BF16)
    wout = w_out.astype(BF16)
    w_r = jnp.pad(jnp.concatenate([w_group, w_expert], axis=1).astype(F32),
                  ((0, 0), (0, LANES - N_GROUPS - N_EXPERTS)))
    wr_hi = w_r.astype(BF16)
    wr_lo = (w_r - wr_hi.astype(F32)).astype(BF16)
    wr_cat = jnp.concatenate([wr_hi, wr_lo], axis=1)
    b_r = jnp.pad(jnp.concatenate([b_group, b_expert]).astype(F32),
                  (0, LANES - N_GROUPS - N_EXPERTS)).reshape(1, LANES)

    tm1 = _pick(seq, (512, 256, 128))
    qt, kf, vt, qs, ks, vs, g_f, g_s, qxt, kx = _in_proj(
        x2, pos2, attn_norm.reshape(1, D), weights, invf, bf_pad, b_gate.astype(F32),
        batch=batch, seq=seq, tm=tm1)
    tq = _pick(seq, (FOX_TQ, 128))
    tk = min(tq, FOX_TK)
    o_f = _fox_attn(qt, qxt, kf, kx, vt, batch=batch, seq=seq, tq=tq, tk=tk,
                    unroll=min(FOX_UNROLL, tq // tk))
    o_s = _swa_attn(sinks_perm, qs, ks, vs, batch=batch, seq=seq, tq=_pick(seq, (512, 256, 128)))

    tm3 = _pick(T, (512, 256, 128))
    x1, h2, route, route_t, counts = _mix_route(o_f, o_s, g_f, g_s, x2, wpf, wps, wout,
                                       ffn_norm.reshape(1, D), wr_cat, b_r, tm=tm3)

    P = 2 * T + N_EXPERTS * ROW_BLOCK
    n_blk = P // ROW_BLOCK
    dest, tbl, blk = _plan(counts, route_t, n_blk)
    dest0, dest1 = dest[ROUTE_E0], dest[ROUTE_E1]
    pad_end = tbl[1, :N_EXPERTS]
    bounds = jnp.concatenate([jnp.zeros((1,), jnp.int32), pad_end])
    blk_expert, blk_rows = blk[0, :n_blk], blk[1, :n_blk]
    n_used = pad_end[-1:] // ROW_BLOCK

    xs = _dispatch(bounds, dest0, dest1, h2, P, tb=_pick(T, (2048, 1024, 512, 256, 128)))
    ys = _experts(blk_expert, n_used, blk_rows, xs, w1, w3, w2)
    return _combine(dest0, dest1, x1, route, final_norm.reshape(1, D), ys,
                    tb=_pick(T, (512, 256, 128)))


def kernel(x, positions, attn_norm, w_in, b_forget, b_gate, attn_sinks, w_proj_fox, w_proj_swa,
           w_out, ffn_norm, w_group, b_group, w_expert, b_expert, w1, w3, w2, final_norm):
    B, S, D = x.shape
    depth = attn_norm.shape[0]
    x2 = x.reshape(B * S, D)
    pos2 = positions.reshape(B * S, 1).astype(jnp.int32)
    assert depth == 1, "a single layer is followed directly by the final norm"
    out = _layer(x2, pos2, B, S, attn_norm[0], w_in[0], b_forget[0], b_gate[0], attn_sinks[0],
                 w_proj_fox[0], w_proj_swa[0], w_out[0], ffn_norm[0], w_group[0], b_group[0],
                 w_expert[0], b_expert[0], w1[0], w3[0], w2[0], final_norm)
    return out.reshape(B, S, D)
```
